```python
import jax, jax.numpy as jnp
from jax import lax
import numpy as np

D_MODEL = 1024
BATCH = 1
SEQ = 16384
DEPTH = 4
DEC_BATCH = 32
DEC_SEQ = 32
PAST_LEN = 4096

CHUNK = 64
N_META = 16
N_MIXERS = 2
N_A_LAYERS = (DEPTH + 1) // 2
N_B_LAYERS = DEPTH // 2
A_EXPAND = 128
A_HEADS = D_MODEL // A_EXPAND
A_DK = A_EXPAND
A_DV = D_MODEL // A_HEADS
A_FDIM = A_HEADS * A_DK
A_IN = 2 * A_FDIM + 2 * D_MODEL
B_HEADS = 16
B_DH = D_MODEL // B_HEADS
B_IN = 3 * D_MODEL + B_HEADS
Q_BLOCK = 128
D_FF = -(-8 * D_MODEL // (3 * 256)) * 256
EPS = 1e-6
NEG = -1e30
LB_FLOOR = 1e-30

kernel_name = 'hgrn2_fox_sandwich_meta_stream_step'

F32 = jnp.float32


def _rmsnorm(x, w):
    x32 = x.astype(F32)
    y = x32 * lax.rsqrt(jnp.mean(x32 * x32, axis=-1, keepdims=True) + EPS)
    return (y * w.astype(F32)).astype(x.dtype)


def _swiglu(h, w_gu, w_down):
    a, u = jnp.split(h @ w_gu, 2, axis=-1)
    return (jax.nn.silu(a) * u) @ w_down


def _hgrn2_chunks(q, k, v, g, s0, chunk):
    b, l, h, _ = q.shape
    n = l // chunk

    def split(t):
        return jnp.moveaxis(t.reshape(b, n, chunk, h, t.shape[-1]), 1, 0)

    causal = jnp.tril(jnp.ones((chunk, chunk), bool))[None, :, :, None, None]

    def step(s, inp):
        qc, kc, vc, gc = inp
        gcum = jnp.cumsum(gc, axis=1)
        o_inter = jnp.einsum('bchk,bhkv->bchv', qc * jnp.exp(gcum), s)
        diff = gcum[:, :, None] - gcum[:, None, :]
        decay = jnp.exp(jnp.where(causal, diff, NEG))
        att = jnp.einsum('btshk,bshk->bhts', qc[:, :, None] * decay, kc)
        o_intra = jnp.einsum('bhts,bshv->bthv', att, vc)
        g_last = gcum[:, -1]
        k_dec = kc * jnp.exp(g_last[:, None] - gcum)
        s_new = jnp.exp(g_last)[..., None] * s + jnp.einsum('bchk,bchv->bhkv', k_dec, vc)
        return s_new, o_inter + o_intra

    s_fin, o = lax.scan(step, s0, (split(q), split(k), split(v), split(g)))
    o = jnp.moveaxis(o, 0, 1).reshape(b, l, h, v.shape[-1])
    return o, s_fin


def _hgrn2_mixer(h, w_in, lb, g_norm, w_out, s0, segments):
    b, l, _ = h.shape
    q, fl, inp, gate = jnp.split(h @ w_in, [A_FDIM, 2 * A_FDIM, 2 * A_FDIM + D_MODEL], axis=-1)
    fl = fl.astype(F32)
    lb32 = lb.astype(F32)
    log_lb = jnp.log(jnp.maximum(lb32, LB_FLOOR))
    logf = jnp.logaddexp(log_lb, jnp.log1p(-lb32) + jax.nn.log_sigmoid(fl))
    k = (1.0 - lb32) * jax.nn.sigmoid(-fl)
    q = jax.nn.silu(q.astype(F32))
    q = q.reshape(b, l, A_HEADS, A_DK)
    k = k.reshape(b, l, A_HEADS, A_DK)
    logf = logf.reshape(b, l, A_HEADS, A_DK)
    v = inp.astype(F32).reshape(b, l, A_HEADS, A_DV)
    outs = []
    s = s0.astype(F32)
    start = 0
    for length, block in segments:
        sl = slice(start, start + length)
        o_seg, s = _hgrn2_chunks(q[:, sl], k[:, sl], v[:, sl], logf[:, sl], s, block)
        outs.append(o_seg)
        start += length
    o = jnp.concatenate(outs, axis=1) if len(outs) > 1 else outs[0]
    o = o * lax.rsqrt(jnp.mean(o * o, axis=-1, keepdims=True) + EPS)
    o = o * g_norm.astype(F32).reshape(A_HEADS, A_DV)
    o = o.reshape(b, l, D_MODEL) * jax.nn.silu(gate.astype(F32))
    return o.astype(h.dtype) @ w_out, s


def _fox_project(h, w_in, b_f):
    b, l, _ = h.shape
    q, k, v, fl = jnp.split(h @ w_in, [D_MODEL, 2 * D_MODEL, 3 * D_MODEL], axis=-1)
    logf = jax.nn.log_sigmoid(fl.astype(F32) + b_f.astype(F32))
    shp = (b, l, B_HEADS, B_DH)
    return q.reshape(shp), k.reshape(shp), v.reshape(shp), logf


def _fox_block(q, c_q, pos_q, k, v, c_k, pos_k):
    s = jnp.einsum('bqhd,bkhd->bhqk', q, k).astype(F32) * (B_DH ** -0.5)
    bias = jnp.swapaxes(c_q, 1, 2)[..., :, None] - jnp.swapaxes(c_k, 1, 2)[..., None, :]
    mask = pos_k[None, :] <= pos_q[:, None]
    p = jax.nn.softmax(jnp.where(mask, s + bias, NEG), axis=-1)
    return jnp.einsum('bhqk,bkhd->bqhd', p.astype(v.dtype), v)


def _fox_prompt(h, w_in, b_f, w_out):
    b, l, _ = h.shape
    q, k, v, logf = _fox_project(h, w_in, b_f)
    lp = -(-l // Q_BLOCK) * Q_BLOCK
    pad = ((0, 0), (0, lp - l), (0, 0), (0, 0))
    qp, kp, vp = jnp.pad(q, pad), jnp.pad(k, pad), jnp.pad(v, pad)
    c = jnp.cumsum(jnp.pad(logf, pad[:3]), axis=1)
    pos = jnp.arange(lp)

    def blk(i):
        st = i * Q_BLOCK
        qb = lax.dynamic_slice_in_dim(qp, st, Q_BLOCK, axis=1)
        cb = lax.dynamic_slice_in_dim(c, st, Q_BLOCK, axis=1)
        return _fox_block(qb, cb, st + jnp.arange(Q_BLOCK), kp, vp, c, pos)

    o = lax.map(blk, jnp.arange(lp // Q_BLOCK))
    o = jnp.moveaxis(o, 0, 1).reshape(b, lp, D_MODEL)[:, :l]
    return o @ w_out, k, v, logf


def _fox_sample(h, w_in, b_f, w_out, ck, cv, clogf):
    b, s_len, _ = h.shape
    past = ck.shape[1]
    q, k, v, logf = _fox_project(h, w_in, b_f)
    k_all = jnp.concatenate([ck.astype(k.dtype), k], axis=1)
    v_all = jnp.concatenate([cv.astype(v.dtype), v], axis=1)
    c = jnp.cumsum(jnp.concatenate([clogf.astype(F32), logf], axis=1), axis=1)
    pos_k = jnp.arange(past + s_len)
    o = _fox_block(q, c[:, past:], past + jnp.arange(s_len), k_all, v_all, c, pos_k)
    return o.reshape(b, s_len, D_MODEL) @ w_out, k, v, logf


def setup_inputs(seed: int = 0) -> dict:
    key = jax.random.key(seed)
    ks = jax.random.split(key, 24)
    nrm = jax.random.normal
    d = D_MODEL
    return {
        'x_prompt': nrm(ks[0], (BATCH, SEQ, d), F32),
        'x_sample': nrm(ks[1], (DEC_BATCH, DEC_SEQ, d), F32),
        'state_hgrn': 0.3 * nrm(ks[2], (N_A_LAYERS, DEC_BATCH, A_HEADS, A_DK, A_DV), F32),
        'cache_k': nrm(ks[3], (N_B_LAYERS, DEC_BATCH, PAST_LEN, B_HEADS, B_DH), F32),
        'cache_v': nrm(ks[4], (N_B_LAYERS, DEC_BATCH, PAST_LEN, B_HEADS, B_DH), F32),
        'cache_logf': jax.nn.log_sigmoid(2.0 + nrm(ks[5], (N_B_LAYERS, DEC_BATCH, PAST_LEN, B_HEADS), F32)),
        'meta_tokens': nrm(ks[6], (N_META, d), F32),
        'norm_pre_mix': 1.0 + 0.1 * nrm(ks[7], (DEPTH, d), F32),
        'norm_post_mix': 1.0 + 0.1 * nrm(ks[8], (DEPTH, d), F32),
        'norm_pre_ffn': 1.0 + 0.1 * nrm(ks[9], (DEPTH, d), F32),
        'norm_post_ffn': 1.0 + 0.1 * nrm(ks[10], (DEPTH, d), F32),
        'a_w_in': nrm(ks[11], (N_A_LAYERS, d, A_IN), F32) * d ** -0.5,
        'a_lb_logits': 0.5 * nrm(ks[12], (N_A_LAYERS, A_FDIM), F32),
        'a_g_norm': 1.0 + 0.1 * nrm(ks[13], (N_A_LAYERS, d), F32),
        'a_w_out': nrm(ks[14], (N_A_LAYERS, d, d), F32) * d ** -0.5,
        'b_w_in': nrm(ks[15], (N_B_LAYERS, d, B_IN), F32) * d ** -0.5,
        'b_f': 2.0 + 0.1 * nrm(ks[16], (N_B_LAYERS, B_HEADS), F32),
        'b_w_out': nrm(ks[17], (N_B_LAYERS, d, d), F32) * d ** -0.5,
        'ffn_w_gu': nrm(ks[18], (DEPTH, d, 2 * D_FF), F32) * d ** -0.5,
        'ffn_w_down': nrm(ks[19], (DEPTH, D_FF, d), F32) * D_FF ** -0.5,
    }


def reference(x_prompt, x_sample, state_hgrn, cache_k, cache_v, cache_logf, meta_tokens,
              norm_pre_mix, norm_post_mix, norm_pre_ffn, norm_post_ffn,
              a_w_in, a_lb_logits, a_g_norm, a_w_out, b_w_in, b_f, b_w_out,
              ffn_w_gu, ffn_w_down):
    sm = jax.nn.softmax(a_lb_logits.astype(F32), axis=0)
    lb_all = jnp.cumsum(sm, axis=0) - sm[0]

    b = x_prompt.shape[0]
    meta = jnp.broadcast_to(meta_tokens.astype(x_prompt.dtype)[None], (b, N_META, D_MODEL))
    hp = jnp.concatenate([meta, x_prompt], axis=1)
    hs = x_sample
    seg_p = ((N_META, N_META), (hp.shape[1] - N_META, CHUNK))
    seg_s = ((hs.shape[1], hs.shape[1]),)

    st_p, st_s, k_p, v_p, lf_p, k_s, v_s, lf_s = [], [], [], [], [], [], [], []
    for i in range(DEPTH):
        j = i // N_MIXERS
        np_ = _rmsnorm(hp, norm_pre_mix[i])
        ns_ = _rmsnorm(hs, norm_pre_mix[i])
        if i % N_MIXERS == 0:
            s0 = jnp.zeros((b, A_HEADS, A_DK, A_DV), F32)
            mp, sp = _hgrn2_mixer(np_, a_w_in[j], lb_all[j], a_g_norm[j], a_w_out[j], s0, seg_p)
            ms, ss = _hgrn2_mixer(ns_, a_w_in[j], lb_all[j], a_g_norm[j], a_w_out[j], state_hgrn[j], seg_s)
            st_p.append(sp.astype(state_hgrn.dtype))
            st_s.append(ss.astype(state_hgrn.dtype))
        else:
            mp, kp_, vp_, lp_ = _fox_prompt(np_, b_w_in[j], b_f[j], b_w_out[j])
            ms, ks_, vs_, ls_ = _fox_sample(ns_, b_w_in[j], b_f[j], b_w_out[j],
                                            cache_k[j], cache_v[j], cache_logf[j])
            k_p.append(kp_); v_p.append(vp_); lf_p.append(lp_)
            k_s.append(ks_); v_s.append(vs_); lf_s.append(ls_)
        hp = hp + _rmsnorm(mp, norm_post_mix[i])
        hs = hs + _rmsnorm(ms, norm_post_mix[i])
        hp = hp + _rmsnorm(_swiglu(_rmsnorm(hp, norm_pre_ffn[i]), ffn_w_gu[i], ffn_w_down[i]), norm_post_ffn[i])
        hs = hs + _rmsnorm(_swiglu(_rmsnorm(hs, norm_pre_ffn[i]), ffn_w_gu[i], ffn_w_down[i]), norm_post_ffn[i])

    y_prompt = hp[:, N_META:]
    y_sample = hs
    new_state_hgrn_p = jnp.stack(st_p)
    new_k_p = jnp.stack(k_p)
    new_v_p = jnp.stack(v_p)
    new_logf_p = jnp.stack(lf_p)
    new_state_hgrn_s = jnp.stack(st_s)
    new_k_s = jnp.stack(k_s)
    new_v_s = jnp.stack(v_s)
    new_logf_s = jnp.stack(lf_s)
    return (y_prompt, y_sample, new_state_hgrn_p, new_k_p, new_v_p, new_logf_p,
            new_state_hgrn_s, new_k_s, new_v_s, new_logf_s)
```

```python
import functools

import jax
import jax.numpy as jnp
from jax import lax
from jax.experimental import pallas as pl
from jax.experimental.pallas import tpu as pltpu

F32 = jnp.float32
BF16 = jnp.bfloat16
EPS = 1e-6
NEG = -1e30
LB_FLOOR = 1e-30

LANES = 128
VMEM_LIMIT = 56 * 1024 * 1024
HGRN_SUB = 16
HIGHEST = lax.Precision.HIGHEST


def _params(n_grid):
    return pltpu.CompilerParams(dimension_semantics=("arbitrary",) * n_grid,
                                vmem_limit_bytes=VMEM_LIMIT)


def _resident(shape):
    nd = len(shape)
    return pl.BlockSpec(shape, lambda *_: (0,) * nd, pipeline_mode=pl.Buffered(1))


def _rmsnorm(x, w):
    return x * lax.rsqrt(jnp.mean(x * x, axis=-1, keepdims=True) + EPS) * w


def _sigmoid(x):
    return 1.0 / (1.0 + jnp.exp(-x))


def _log_sigmoid(x):
    return jnp.minimum(x, 0.0) - jnp.log1p(jnp.exp(-jnp.abs(x)))


def _mm(a, b):
    return jnp.dot(a, b, preferred_element_type=F32)


def _mm_nt(a, b):
    return lax.dot_general(a, b, (((1,), (1,)), ((), ())), preferred_element_type=F32)


def _mm_tn(a, b):
    return lax.dot_general(a, b, (((0,), (0,)), ((), ())), preferred_element_type=F32)


def _hgrn_proj_body(x_ref, nw_ref, w_ref, lb_ref, q_ref, k_ref, g_ref, v_ref, gate_ref):
    f = q_ref.shape[1]
    d = v_ref.shape[1]
    xn = _rmsnorm(x_ref[...], nw_ref[...]).astype(BF16)
    q = _mm(xn, w_ref[:, 0:f])
    q_ref[...] = q * _sigmoid(q)
    fl = _mm(xn, w_ref[:, f:2 * f])
    e = jnp.exp(-jnp.abs(fl))
    r = 1.0 / (1.0 + e)
    a = lb_ref[0:1, :]
    b = lb_ref[1:2, :] + (jnp.minimum(fl, 0.0) - jnp.log1p(e))
    g_ref[...] = jnp.maximum(a, b) + jnp.log1p(jnp.exp(-jnp.abs(a - b)))
    k_ref[...] = lb_ref[2:3, :] * jnp.where(fl >= 0.0, e * r, r)
    v_ref[...] = _mm(xn, w_ref[:, 2 * f:2 * f + d])
    gt = _mm(xn, w_ref[:, 2 * f + d:])
    gate_ref[...] = gt * _sigmoid(gt)


def _hgrn_proj(x, nw, w_bf, lbp, tm):
    n, d = x.shape
    f = lbp.shape[1]
    row = lambda w: pl.BlockSpec((tm, w), lambda i: (i, 0))
    return pl.pallas_call(
        _hgrn_proj_body,
        grid=(n // tm,),
        in_specs=[row(d), _resident((1, d)), _resident(w_bf.shape), _resident(lbp.shape)],
        out_specs=[row(f), row(f), row(f), row(d), row(d)],
        out_shape=[jax.ShapeDtypeStruct((n, w), F32) for w in (f, f, f, d, d)],
        compiler_params=_params(1),
        name="hgrn_proj",
    )(x, nw, w_bf, lbp)


def _hgrn_chunk(q, k, g, v, st, c):
    sub = HGRN_SUB
    r_i = lax.broadcasted_iota(jnp.int32, (c, c), 0)
    c_i = lax.broadcasted_iota(jnp.int32, (c, c), 1)
    tril = jnp.where(r_i >= c_i, 1.0, 0.0).astype(F32)
    gc = jnp.dot(tril, g, precision=HIGHEST, preferred_element_type=F32)
    g_last = gc[c - 1:c, :]
    o_inter = _mm_nt((q * jnp.exp(gc)).astype(BF16), st.astype(BF16))
    sub_row = lax.broadcasted_iota(jnp.int32, (sub, 1), 0)
    outs = []
    for i in range(c // sub):
        r0 = i * sub
        gi = gc[r0:r0 + sub]
        qi = q[r0:r0 + sub]
        ki = k[r0:r0 + sub]
        vi = v[r0:r0 + sub]
        oi = o_inter[r0:r0 + sub]
        if i > 0:
            g_ref_row = gc[r0:r0 + 1]
            qs = (qi * jnp.exp(gi - g_ref_row)).astype(BF16)
            ks = (k[:r0] * jnp.exp(g_ref_row - gc[:r0])).astype(BF16)
            att = _mm_nt(qs, ks)
            oi = oi + _mm(att.astype(BF16), v[:r0].astype(BF16))
        for s in range(sub):
            dec = jnp.exp(jnp.where(sub_row >= s, gi - gi[s:s + 1], NEG))
            col = jnp.sum(qi * dec * ki[s:s + 1], axis=-1, keepdims=True)
            oi = oi + col * vi[s:s + 1]
        outs.append(oi)
    o = jnp.concatenate(outs, axis=0) if len(outs) > 1 else outs[0]
    kd = (k * jnp.exp(g_last - gc)).astype(BF16)
    st_new = st * jnp.exp(g_last) + _mm_tn(v.astype(BF16), kd)
    return o, st_new


def _hgrn_rec_body(*refs, c, r, has_meta):
    if has_meta:
        (q_ref, k_ref, g_ref, v_ref, s0_ref, qm_ref, km_ref, gm_ref, vm_ref,
         o_ref, s_ref, om_ref, st_ref) = refs
    else:
        q_ref, k_ref, g_ref, v_ref, s0_ref, o_ref, s_ref, st_ref = refs
    blk = pl.program_id(2)

    @pl.when(blk == 0)
    def _():
        st = s0_ref[0, 0].T
        if has_meta:
            o_m, st = _hgrn_chunk(qm_ref[...], km_ref[...], gm_ref[...], vm_ref[...],
                                  st, qm_ref.shape[0])
            om_ref[...] = o_m
        st_ref[...] = st

    def body(j, carry):
        sl = pl.ds(pl.multiple_of(j * c, c), c)
        o, st = _hgrn_chunk(q_ref[sl, :], k_ref[sl, :], g_ref[sl, :], v_ref[sl, :],
                            st_ref[...], c)
        o_ref[sl, :] = o
        st_ref[...] = st
        return carry

    lax.fori_loop(0, r // c, body, 0)

    @pl.when(blk == pl.num_programs(2) - 1)
    def _():
        s_ref[0, 0] = st_ref[...].T


def _hgrn_rec(q, k, g, v, s0, *, n_seq, seq_len, r, c, meta=None):
    _, h, dk, dv = s0.shape
    nblk = seq_len // r
    tok = lambda w: pl.BlockSpec((r, w), lambda b, hh, i: (b * nblk + i, hh))
    st_spec = pl.BlockSpec((1, 1, dk, dv), lambda b, hh, i: (b, hh, 0, 0))
    in_specs = [tok(dk), tok(dk), tok(dk), tok(dv), st_spec]
    args = [q, k, g, v, s0]
    out_specs = [tok(dv), st_spec]
    out_shape = [jax.ShapeDtypeStruct((n_seq * seq_len, h * dv), F32),
                 jax.ShapeDtypeStruct(s0.shape, F32)]
    if meta is not None:
        qm, km, gm, vm, n_meta, meta_blk = meta
        mspec = lambda w: pl.BlockSpec((n_meta, w), lambda b, hh, i: (meta_blk, hh))
        in_specs += [mspec(dk), mspec(dk), mspec(dk), mspec(dv)]
        args += [qm, km, gm, vm]
        out_specs.append(pl.BlockSpec((n_meta, dv), lambda b, hh, i: (0, hh)))
        out_shape.append(jax.ShapeDtypeStruct((n_meta, h * dv), F32))
    return pl.pallas_call(
        functools.partial(_hgrn_rec_body, c=c, r=r, has_meta=meta is not None),
        grid=(n_seq, h, nblk),
        in_specs=in_specs,
        out_specs=out_specs,
        out_shape=out_shape,
        scratch_shapes=[pltpu.VMEM((dv, dk), F32)],
        compiler_params=_params(3),
        name="hgrn_rec",
    )(*args)


def _mix_out_body(*refs, head_dim):
    if head_dim:
        o_ref, gate_ref, x_ref, gn_ref, w_ref, pw_ref, y_ref = refs
        o = o_ref[...]
        parts = []
        for h in range(o.shape[1] // head_dim):
            oh = o[:, h * head_dim:(h + 1) * head_dim]
            parts.append(oh * lax.rsqrt(jnp.mean(oh * oh, axis=-1, keepdims=True) + EPS))
        o = (jnp.concatenate(parts, axis=-1) * gn_ref[...] * gate_ref[...]).astype(BF16)
    else:
        o_ref, x_ref, w_ref, pw_ref, y_ref = refs
        o = o_ref[...]
    m = _mm(o, w_ref[...])
    y_ref[...] = x_ref[...] + _rmsnorm(m, pw_ref[...])


def _mix_out(o, gate, x, gn, w_bf, pw, tm, head_dim):
    n, d = x.shape
    row = pl.BlockSpec((tm, d), lambda i: (i, 0))
    if head_dim:
        in_specs = [row, row, row, _resident((1, d)), _resident(w_bf.shape), _resident((1, d))]
        args = (o, gate, x, gn, w_bf, pw)
    else:
        in_specs = [row, row, _resident(w_bf.shape), _resident((1, d))]
        args = (o, x, w_bf, pw)
    return pl.pallas_call(
        functools.partial(_mix_out_body, head_dim=head_dim),
        grid=(n // tm,),
        in_specs=in_specs,
        out_specs=row,
        out_shape=jax.ShapeDtypeStruct((n, d), F32),
        compiler_params=_params(1),
        name="mix_out",
    )(*args)


def _ffn_body(x_ref, nw_ref, wgu_ref, wd_ref, pw_ref, y_ref, *, dff, fc):
    x = x_ref[...]
    xn = _rmsnorm(x, nw_ref[...]).astype(BF16)
    acc = jnp.zeros(x.shape, F32)
    for c0 in range(0, dff, fc):
        a = _mm(xn, wgu_ref[:, c0:c0 + fc])
        u = _mm(xn, wgu_ref[:, dff + c0:dff + c0 + fc])
        hid = (a * _sigmoid(a) * u).astype(BF16)
        acc = acc + _mm(hid, wd_ref[c0:c0 + fc, :])
    y_ref[...] = x + _rmsnorm(acc, pw_ref[...])


def _ffn(x, nw, wgu_bf, wd_bf, pw, tm):
    n, d = x.shape
    dff = wd_bf.shape[0]
    fc = dff // 2 if (dff // 2) % LANES == 0 else dff
    row = pl.BlockSpec((tm, d), lambda i: (i, 0))
    return pl.pallas_call(
        functools.partial(_ffn_body, dff=dff, fc=fc),
        grid=(n // tm,),
        in_specs=[row, _resident((1, d)), _resident(wgu_bf.shape), _resident(wd_bf.shape),
                  _resident((1, d))],
        out_specs=row,
        out_shape=jax.ShapeDtypeStruct((n, d), F32),
        compiler_params=_params(1),
        name="ffn",
    )(x, nw, wgu_bf, wd_bf, pw)


def _fox_proj_body(x_ref, nw_ref, w_ref, wf_ref, bf_ref, k_ref, v_ref, lf_ref,
                   qb_ref, kb_ref, vb_ref, *, scale):
    d = k_ref.shape[1]
    nh = lf_ref.shape[1]
    xn = _rmsnorm(x_ref[...], nw_ref[...]).astype(BF16)
    qb_ref[...] = (_mm(xn, w_ref[:, 0:d]) * scale).astype(BF16)
    k = _mm(xn, w_ref[:, d:2 * d])
    k_ref[...] = k
    kb_ref[...] = k.astype(BF16)
    v = _mm(xn, w_ref[:, 2 * d:3 * d])
    v_ref[...] = v
    vb_ref[...] = v.astype(BF16)
    lf = _log_sigmoid(_mm(xn, wf_ref[...]) + bf_ref[...])
    lf_ref[...] = lf[:, 0:nh]


def _fox_proj(x, nw, w_bf, wf_bf, bf_row, nh, tm):
    n, d = x.shape
    row = lambda w: pl.BlockSpec((tm, w), lambda i: (i, 0))
    scale = float(d // nh) ** -0.5
    return pl.pallas_call(
        functools.partial(_fox_proj_body, scale=scale),
        grid=(n // tm,),
        in_specs=[row(d), _resident((1, d)), _resident(w_bf.shape), _resident(wf_bf.shape),
                  _resident(bf_row.shape)],
        out_specs=[row(d), row(d), row(nh), row(d), row(d), row(d)],
        out_shape=[jax.ShapeDtypeStruct((n, d), F32), jax.ShapeDtypeStruct((n, d), F32),
                   jax.ShapeDtypeStruct((n, nh), F32), jax.ShapeDtypeStruct((n, d), BF16),
                   jax.ShapeDtypeStruct((n, d), BF16), jax.ShapeDtypeStruct((n, d), BF16)],
        compiler_params=_params(1),
        name="fox_proj",
    )(x, nw, w_bf, wf_bf, bf_row)


def _lane_cumsum_body(x_ref, y_ref, *, reverse_exclusive):
    rows, n = x_ref.shape
    nb = n // LANES
    t_i = lax.broadcasted_iota(jnp.int32, (LANES, LANES), 0)
    k_i = lax.broadcasted_iota(jnp.int32, (LANES, LANES), 1)
    sel = (t_i > k_i) if reverse_exclusive else (t_i <= k_i)
    tri = jnp.where(sel, 1.0, 0.0).astype(F32)

    def body(i, carry):
        b = (nb - 1 - i) if reverse_exclusive else i
        sl = pl.ds(pl.multiple_of(b * LANES, LANES), LANES)
        xb = x_ref[:, sl]
        y_ref[:, sl] = jnp.dot(xb, tri, precision=HIGHEST, preferred_element_type=F32) + carry
        return carry + jnp.sum(xb, axis=-1, keepdims=True)

    lax.fori_loop(0, nb, body, jnp.zeros((rows, 1), F32))


def _lane_cumsum(x, reverse_exclusive, tr):
    rows, n = x.shape
    spec = pl.BlockSpec((tr, n), lambda i: (i, 0))
    return pl.pallas_call(
        functools.partial(_lane_cumsum_body, reverse_exclusive=reverse_exclusive),
        grid=(rows // tr,),
        in_specs=[spec],
        out_specs=spec,
        out_shape=jax.ShapeDtypeStruct((rows, n), F32),
        compiler_params=_params(1),
        name="lane_cumsum",
    )(x)


def _fox_attn_body(c_ref, q_ref, k_ref, v_ref, qm_ref, km_ref, vm_ref, o_ref, om_ref,
                   m_s, l_s, acc_s, *, tq, tk, dh, n_meta):
    i = pl.program_id(1)
    lane = lax.broadcasted_iota(jnp.int32, (1, 2 * dh), 1)
    head_sel = [lane < dh, lane >= dh]

    @pl.when(i == 0)
    def _():
        r_i = lax.broadcasted_iota(jnp.int32, (n_meta, n_meta), 0)
        c_i = lax.broadcasted_iota(jnp.int32, (n_meta, n_meta), 1)
        out = jnp.zeros((n_meta, 2 * dh), F32)
        for a in range(2):
            qa = jnp.where(head_sel[a], qm_ref[...], jnp.zeros_like(qm_ref[...]))
            x = _mm_nt(qa, km_ref[...]) - c_ref[a, :, 0:n_meta]
            x = jnp.where(c_i <= r_i, x, NEG)
            p = jnp.exp(x - jnp.max(x, axis=-1, keepdims=True))
            oa = _mm(p.astype(BF16), vm_ref[...]) / jnp.sum(p, axis=-1, keepdims=True)
            out = jnp.where(head_sel[a], oa, out)
        om_ref[...] = out.astype(BF16)

    row_g = i * tq + lax.broadcasted_iota(jnp.int32, (tq, tk), 0)
    col_l = lax.broadcasted_iota(jnp.int32, (tq, tk), 1)
    for a in range(2):
        qa = jnp.where(head_sel[a], q_ref[...], jnp.zeros_like(q_ref[...]))
        q0 = pl.multiple_of(LANES + i * tq, LANES)
        c_first = c_ref[a, :, pl.ds(q0, LANES)][:, 0:1]
        x = _mm_nt(qa, km_ref[...]) + (c_first - c_ref[a, :, 0:n_meta])
        m0 = jnp.max(x, axis=-1, keepdims=True)
        p = jnp.exp(x - m0)
        m_s[a] = m0
        l_s[a] = jnp.sum(p, axis=-1, keepdims=True)
        acc_s[a] = _mm(p.astype(BF16), vm_ref[...])

        def body(j, carry):
            k0 = pl.multiple_of(j * tk, tk)
            kj = k_ref[pl.ds(k0, tk), :]
            vj = v_ref[pl.ds(k0, tk), :]
            cj = c_ref[a, :, pl.ds(pl.multiple_of(LANES + j * tk, LANES), tk)]
            x = _mm_nt(qa, kj) + (c_first - cj)
            x = jnp.where(col_l + j * tk <= row_g, x, NEG)
            m_prev = m_s[a]
            m_new = jnp.maximum(m_prev, jnp.max(x, axis=-1, keepdims=True))
            alpha = jnp.exp(m_prev - m_new)
            p = jnp.exp(x - m_new)
            l_s[a] = alpha * l_s[a] + jnp.sum(p, axis=-1, keepdims=True)
            acc_s[a] = alpha * acc_s[a] + _mm(p.astype(BF16), vj)
            m_s[a] = m_new
            return carry

        lax.fori_loop(0, (i * tq) // tk + tq // tk, body, 0)

    o0 = acc_s[0] / l_s[0]
    o1 = acc_s[1] / l_s[1]
    o_ref[...] = jnp.where(head_sel[0], o0, o1).astype(BF16)


def _fox_attn(c3, qb, kb, vb, qb_aux, kb_aux, vb_aux, n_meta, meta_blk, dh, tq, tk):
    n, d = qb.shape
    npair = d // (2 * dh)
    nl = c3.shape[2]
    tile = pl.BlockSpec((tq, 2 * dh), lambda p, i: (i, p))
    full = pl.BlockSpec((n, 2 * dh), lambda p, i: (0, p))
    mspec = pl.BlockSpec((n_meta, 2 * dh), lambda p, i: (meta_blk, p))
    return pl.pallas_call(
        functools.partial(_fox_attn_body, tq=tq, tk=tk, dh=dh, n_meta=n_meta),
        grid=(npair, n // tq),
        in_specs=[pl.BlockSpec((2, 1, nl), lambda p, i: (p, 0, 0)), tile, full, full,
                  mspec, mspec, mspec],
        out_specs=[tile, pl.BlockSpec((n_meta, 2 * dh), lambda p, i: (0, p))],
        out_shape=[jax.ShapeDtypeStruct((n, d), BF16), jax.ShapeDtypeStruct((n_meta, d), BF16)],
        scratch_shapes=[pltpu.VMEM((2, tq, 1), F32), pltpu.VMEM((2, tq, 1), F32),
                        pltpu.VMEM((2, tq, 2 * dh), F32)],
        compiler_params=_params(2),
        name="fox_attn",
    )(c3, qb, kb, vb, qb_aux, kb_aux, vb_aux)


def _fox_sample_body(q_ref, kn_ref, vn_ref, cn_ref, kc_ref, vc_ref, suf_ref, o_ref,
                     qbd_s, m_s, l_s, al_s, acc_s, p_s, *, nh, dh, s_len):
    t = pl.program_id(1)
    d = nh * dh
    lane_head = lax.broadcasted_iota(jnp.int32, (s_len, d), 1) // dh

    @pl.when(t == 0)
    def _():
        q = q_ref[...]
        for h in range(nh):
            qbd_s[h * s_len:(h + 1) * s_len, :] = jnp.where(lane_head == h, q, jnp.zeros_like(q))
        s = _mm_nt(qbd_s[...], kn_ref[...])
        r_i = lax.broadcasted_iota(jnp.int32, (s_len, s_len), 0)
        c_i = lax.broadcasted_iota(jnp.int32, (s_len, s_len), 1)
        for h in range(nh):
            rows = slice(h * s_len, (h + 1) * s_len)
            x = s[rows, :] - cn_ref[0, h:h + 1, 0:s_len]
            x = jnp.where(c_i <= r_i, x, NEG)
            m = jnp.max(x, axis=-1, keepdims=True)
            p = jnp.exp(x - m)
            m_s[rows, :] = m
            l_s[rows, :] = jnp.sum(p, axis=-1, keepdims=True)
            p_s[rows, 0:s_len] = p.astype(BF16)
        acc_s[...] = _mm(p_s[:, 0:s_len], vn_ref[...])

    s = _mm_nt(qbd_s[...], kc_ref[0].astype(BF16))
    for h in range(nh):
        rows = slice(h * s_len, (h + 1) * s_len)
        x = s[rows, :] + suf_ref[0, h:h + 1, :]
        m_prev = m_s[rows, :]
        m_new = jnp.maximum(m_prev, jnp.max(x, axis=-1, keepdims=True))
        alpha = jnp.exp(m_prev - m_new)
        p = jnp.exp(x - m_new)
        l_s[rows, :] = alpha * l_s[rows, :] + jnp.sum(p, axis=-1, keepdims=True)
        m_s[rows, :] = m_new
        al_s[rows, :] = alpha
        p_s[rows, :] = p.astype(BF16)
    acc_s[...] = al_s[...] * acc_s[...] + _mm(p_s[...], vc_ref[0].astype(BF16))

    @pl.when(t == pl.num_programs(1) - 1)
    def _():
        out = jnp.zeros((s_len, d), F32)
        for h in range(nh):
            rows = slice(h * s_len, (h + 1) * s_len)
            out = jnp.where(lane_head == h, acc_s[rows, :] / l_s[rows, :], out)
        o_ref[...] = out.astype(BF16)


def _fox_sample(qb, kb, vb, cn, kc, vc, suf, nh, s_len, tk):
    nb, past, d = kc.shape
    dh = d // nh
    tok = pl.BlockSpec((s_len, d), lambda b, t: (b, 0))
    cache = pl.BlockSpec((1, tk, d), lambda b, t: (b, t, 0))
    return pl.pallas_call(
        functools.partial(_fox_sample_body, nh=nh, dh=dh, s_len=s_len),
        grid=(nb, past // tk),
        in_specs=[tok, tok, tok, pl.BlockSpec((1, nh, LANES), lambda b, t: (b, 0, 0)),
                  cache, cache, pl.BlockSpec((1, nh, tk), lambda b, t: (b, 0, t))],
        out_specs=tok,
        out_shape=jax.ShapeDtypeStruct((nb * s_len, d), BF16),
        scratch_shapes=[pltpu.VMEM((nh * s_len, d), BF16), pltpu.VMEM((nh * s_len, 1), F32),
                        pltpu.VMEM((nh * s_len, 1), F32), pltpu.VMEM((nh * s_len, 1), F32),
                        pltpu.VMEM((nh * s_len, d), F32), pltpu.VMEM((nh * s_len, tk), BF16)],
        compiler_params=_params(2),
        name="fox_sample",
    )(qb, kb, vb, cn, kc, vc, suf)


def _largest_tile(n, cap):
    t = min(n, cap)
    while n % t:
        t //= 2
    return t


def kernel(x_prompt, x_sample, state_hgrn, cache_k, cache_v, cache_logf, meta_tokens,
           norm_pre_mix, norm_post_mix, norm_pre_ffn, norm_post_ffn,
           a_w_in, a_lb_logits, a_g_norm, a_w_out, b_w_in, b_f, b_w_out,
           ffn_w_gu, ffn_w_down):
    batch, seq, d = x_prompt.shape
    assert batch == 1, "one prompt stream per step"
    n_dec, s_len, _ = x_sample.shape
    n_meta = meta_tokens.shape[0]
    depth = norm_pre_mix.shape[0]
    _, _, a_heads, a_dk, a_dv = state_hgrn.shape
    _, _, past, b_heads, b_dh = cache_k.shape
    ns = n_dec * s_len
    assert s_len % HGRN_SUB == 0 and n_meta % HGRN_SUB == 0 and ns % n_meta == 0
    assert 2 * b_dh == LANES and a_dk == LANES and a_dv == LANES

    tm = _largest_tile(seq, 512)
    n_aux = ns + n_meta
    meta_blk = ns // n_meta
    rec_r = _largest_tile(seq, 512)
    rec_c = _largest_tile(rec_r, 64)
    tq = _largest_tile(seq, 512)
    tk_cache = _largest_tile(past, 1024)

    sm = jax.nn.softmax(a_lb_logits.astype(F32), axis=0)
    lb_all = jnp.cumsum(sm, axis=0) - sm[0]

    xm = x_prompt.reshape(seq, d)
    xa = jnp.concatenate([x_sample.reshape(ns, d), meta_tokens.astype(F32)], axis=0)
    row = lambda w: w.reshape(1, -1).astype(F32)

    st_p, st_s, k_p, v_p, lf_p, k_s, v_s, lf_s = [], [], [], [], [], [], [], []
    for i in range(depth):
        j = i // 2
        if i % 2 == 0:
            lb = lb_all[j]
            lbp = jnp.stack([jnp.log(jnp.maximum(lb, LB_FLOOR)), jnp.log1p(-lb), 1.0 - lb])
            w_in = a_w_in[j].astype(BF16)
            w_out = a_w_out[j].astype(BF16)
            qm, km, gm, vm, gtm = _hgrn_proj(xm, row(norm_pre_mix[i]), w_in, lbp, tm)
            qa, ka, ga, va, gta = _hgrn_proj(xa, row(norm_pre_mix[i]), w_in, lbp, n_aux)
            o_s, s_s = _hgrn_rec(qa, ka, ga, va, state_hgrn[j].astype(F32),
                                 n_seq=n_dec, seq_len=s_len, r=s_len, c=s_len)
            o_m, s_p, o_meta = _hgrn_rec(qm, km, gm, vm,
                                         jnp.zeros((1, a_heads, a_dk, a_dv), F32),
                                         n_seq=1, seq_len=seq, r=rec_r, c=rec_c,
                                         meta=(qa, ka, ga, va, n_meta, meta_blk))
            st_p.append(s_p.astype(state_hgrn.dtype))
            st_s.append(s_s.astype(state_hgrn.dtype))
            o_a = jnp.concatenate([o_s, o_meta], axis=0)
            xm = _mix_out(o_m, gtm, xm, row(a_g_norm[j]), w_out, row(norm_post_mix[i]), tm, a_dv)
            xa = _mix_out(o_a, gta, xa, row(a_g_norm[j]), w_out, row(norm_post_mix[i]), n_aux, a_dv)
        else:
            w_in = b_w_in[j][:, :3 * d].astype(BF16)
            w_f = jnp.pad(b_w_in[j][:, 3 * d:], ((0, 0), (0, LANES - b_heads))).astype(BF16)
            bf_row = jnp.pad(b_f[j].astype(F32), (0, LANES - b_heads)).reshape(1, LANES)
            w_out = b_w_out[j].astype(BF16)
            kf_m, vf_m, lf_m, qb_m, kb_m, vb_m = _fox_proj(
                xm, row(norm_pre_mix[i]), w_in, w_f, bf_row, b_heads, tm)
            kf_a, vf_a, lf_a, qb_a, kb_a, vb_a = _fox_proj(
                xa, row(norm_pre_mix[i]), w_in, w_f, bf_row, b_heads, n_aux)
            lf_t = jnp.concatenate(
                [jnp.pad(lf_a[ns:].T, ((0, 0), (0, LANES - n_meta))), lf_m.T], axis=1)
            c_all = _lane_cumsum(lf_t, False, b_heads)
            o_m, o_meta = _fox_attn(c_all.reshape(b_heads, 1, -1), qb_m, kb_m, vb_m,
                                    qb_a, kb_a, vb_a, n_meta, meta_blk, b_dh, tq, tq)
            cl_t = jnp.swapaxes(cache_logf[j].astype(F32), 1, 2).reshape(n_dec * b_heads, past)
            tr = _largest_tile(n_dec * b_heads, LANES)
            suf = _lane_cumsum(cl_t, True, tr).reshape(n_dec, b_heads, past)
            ln_t = jnp.swapaxes(lf_a[:ns].reshape(n_dec, s_len, b_heads), 1, 2)
            ln_t = jnp.pad(ln_t, ((0, 0), (0, 0), (0, LANES - s_len)))
            cn = _lane_cumsum(ln_t.reshape(n_dec * b_heads, LANES), False, tr)
            o_s = _fox_sample(qb_a, kb_a, vb_a, cn.reshape(n_dec, b_heads, LANES),
                              cache_k[j].reshape(n_dec, past, d), cache_v[j].reshape(n_dec, past, d),
                              suf, b_heads, s_len, tk_cache)
            o_a = jnp.concatenate([o_s, o_meta], axis=0)
            xm = _mix_out(o_m, None, xm, None, w_out, row(norm_post_mix[i]), tm, 0)
            xa = _mix_out(o_a, None, xa, None, w_out, row(norm_post_mix[i]), n_aux, 0)
            shp = lambda t, n: t.reshape(-1, n, b_heads, b_dh)
            k_p.append(shp(jnp.concatenate([kf_a[ns:], kf_m], axis=0), n_meta + seq))
            v_p.append(shp(jnp.concatenate([vf_a[ns:], vf_m], axis=0), n_meta + seq))
            lf_p.append(jnp.concatenate([lf_a[ns:], lf_m], axis=0).reshape(1, n_meta + seq, b_heads))
            k_s.append(shp(kf_a[:ns], s_len))
            v_s.append(shp(vf_a[:ns], s_len))
            lf_s.append(lf_a[:ns].reshape(n_dec, s_len, b_heads))
        wgu = ffn_w_gu[i].astype(BF16)
        wd = ffn_w_down[i].astype(BF16)
        xm = _ffn(xm, row(norm_pre_ffn[i]), wgu, wd, row(norm_post_ffn[i]), tm)
        xa = _ffn(xa, row(norm_pre_ffn[i]), wgu, wd, row(norm_post_ffn[i]), n_aux)

    return (xm.reshape(1, seq, d), xa[:ns].reshape(n_dec, s_len, d),
            jnp.stack(st_p), jnp.stack(k_p), jnp.stack(v_p), jnp.stack(lf_p),
            jnp.stack(st_s), jnp.stack(k_s), jnp.stack(v_s), jnp.stack(lf_s))
```

```python
import functools

import jax
import jax.numpy as jnp
from jax import lax
from jax.experimental import pallas as pl
from jax.experimental.pallas import tpu as pltpu

F32 = jnp.float32
BF16 = jnp.bfloat16
EPS = 1e-6
NEG = -1e30
LB_FLOOR = 1e-30

LANES = 128
VMEM_LIMIT = 56 * 1024 * 1024
HGRN_SUB = 16
HIGHEST = lax.Precision.HIGHEST
LOG2E = 1.4426950408889634


def _params(n_grid):
    return pltpu.CompilerParams(dimension_semantics=("arbitrary",) * n_grid,
                                vmem_limit_bytes=VMEM_LIMIT)


def _resident(shape):
    nd = len(shape)
    return pl.BlockSpec(shape, lambda *_: (0,) * nd, pipeline_mode=pl.Buffered(1))


def _rmsnorm(x, w):
    return x * lax.rsqrt(jnp.mean(x * x, axis=-1, keepdims=True) + EPS) * w


def _sigmoid(x):
    return 1.0 / (1.0 + jnp.exp(-x))


def _log_sigmoid(x):
    return jnp.minimum(x, 0.0) - jnp.log1p(jnp.exp(-jnp.abs(x)))


def _mm(a, b):
    return jnp.dot(a, b, preferred_element_type=F32)


def _mm_nt(a, b):
    return lax.dot_general(a, b, (((1,), (1,)), ((), ())), preferred_element_type=F32)


def _mm_tn(a, b):
    return lax.dot_general(a, b, (((0,), (0,)), ((), ())), preferred_element_type=F32)


def _hgrn_proj_body(x_ref, nw_ref, w_ref, lb_ref, q_ref, k_ref, g_ref, v_ref, gate_ref):
    f = q_ref.shape[1]
    d = v_ref.shape[1]
    xn = _rmsnorm(x_ref[...], nw_ref[...]).astype(BF16)
    q = _mm(xn, w_ref[:, 0:f])
    q_ref[...] = q * _sigmoid(q)
    fl = _mm(xn, w_ref[:, f:2 * f])
    e = jnp.exp(-jnp.abs(fl))
    r = 1.0 / (1.0 + e)
    a = lb_ref[0:1, :]
    b = lb_ref[1:2, :] + (jnp.minimum(fl, 0.0) - jnp.log1p(e))
    g_ref[...] = jnp.maximum(a, b) + jnp.log1p(jnp.exp(-jnp.abs(a - b)))
    k_ref[...] = lb_ref[2:3, :] * jnp.where(fl >= 0.0, e * r, r)
    v_ref[...] = _mm(xn, w_ref[:, 2 * f:2 * f + d])
    gt = _mm(xn, w_ref[:, 2 * f + d:])
    gate_ref[...] = gt * _sigmoid(gt)


def _hgrn_proj(x, nw, w_bf, lbp, tm):
    n, d = x.shape
    f = lbp.shape[1]
    row = lambda w: pl.BlockSpec((tm, w), lambda i: (i, 0))
    return pl.pallas_call(
        _hgrn_proj_body,
        grid=(n // tm,),
        in_specs=[row(d), _resident((1, d)), _resident(w_bf.shape), _resident(lbp.shape)],
        out_specs=[row(f), row(f), row(f), row(d), row(d)],
        out_shape=[jax.ShapeDtypeStruct((n, w), F32) for w in (f, f, f, d, d)],
        compiler_params=_params(1),
        name="hgrn_proj",
    )(x, nw, w_bf, lbp)


def _hgrn_chunk(q, k, g, v, st, c):
    sub = HGRN_SUB
    r_i = lax.broadcasted_iota(jnp.int32, (c, c), 0)
    c_i = lax.broadcasted_iota(jnp.int32, (c, c), 1)
    tril = jnp.where(r_i >= c_i, 1.0, 0.0).astype(F32)
    gc = jnp.dot(tril, g, precision=HIGHEST, preferred_element_type=F32)
    g_last = gc[c - 1:c, :]
    o_inter = _mm_nt((q * jnp.exp(gc)).astype(BF16), st.astype(BF16))
    sub_row = lax.broadcasted_iota(jnp.int32, (sub, 1), 0)
    outs = []
    for i in range(c // sub):
        r0 = i * sub
        gi = gc[r0:r0 + sub]
        qi = q[r0:r0 + sub]
        ki = k[r0:r0 + sub]
        vi = v[r0:r0 + sub]
        oi = o_inter[r0:r0 + sub]
        if i > 0:
            g_ref_row = gc[r0:r0 + 1]
            qs = (qi * jnp.exp(gi - g_ref_row)).astype(BF16)
            ks = (k[:r0] * jnp.exp(g_ref_row - gc[:r0])).astype(BF16)
            att = _mm_nt(qs, ks)
            oi = oi + _mm(att.astype(BF16), v[:r0].astype(BF16))
        for s in range(sub):
            dec = jnp.exp(jnp.where(sub_row >= s, gi - gi[s:s + 1], NEG))
            col = jnp.sum(qi * dec * ki[s:s + 1], axis=-1, keepdims=True)
            oi = oi + col * vi[s:s + 1]
        outs.append(oi)
    o = jnp.concatenate(outs, axis=0) if len(outs) > 1 else outs[0]
    kd = (k * jnp.exp(g_last - gc)).astype(BF16)
    st_new = st * jnp.exp(g_last) + _mm_tn(v.astype(BF16), kd)
    return o, st_new


def _hgrn_rec_body(*refs, c, r, has_meta):
    if has_meta:
        (q_ref, k_ref, g_ref, v_ref, s0_ref, qm_ref, km_ref, gm_ref, vm_ref,
         o_ref, s_ref, om_ref, st_ref) = refs
    else:
        q_ref, k_ref, g_ref, v_ref, s0_ref, o_ref, s_ref, st_ref = refs
    blk = pl.program_id(2)

    @pl.when(blk == 0)
    def _():
        st = s0_ref[0, 0].T
        if has_meta:
            o_m, st = _hgrn_chunk(qm_ref[...], km_ref[...], gm_ref[...], vm_ref[...],
                                  st, qm_ref.shape[0])
            om_ref[...] = o_m
        st_ref[...] = st

    def body(j, carry):
        sl = pl.ds(pl.multiple_of(j * c, c), c)
        o, st = _hgrn_chunk(q_ref[sl, :], k_ref[sl, :], g_ref[sl, :], v_ref[sl, :],
                            st_ref[...], c)
        o_ref[sl, :] = o
        st_ref[...] = st
        return carry

    lax.fori_loop(0, r // c, body, 0)

    @pl.when(blk == pl.num_programs(2) - 1)
    def _():
        s_ref[0, 0] = st_ref[...].T


def _hgrn_rec(q, k, g, v, s0, *, n_seq, seq_len, r, c, meta=None):
    _, h, dk, dv = s0.shape
    nblk = seq_len // r
    tok = lambda w: pl.BlockSpec((r, w), lambda b, hh, i: (b * nblk + i, hh))
    st_spec = pl.BlockSpec((1, 1, dk, dv), lambda b, hh, i: (b, hh, 0, 0))
    in_specs = [tok(dk), tok(dk), tok(dk), tok(dv), st_spec]
    args = [q, k, g, v, s0]
    out_specs = [tok(dv), st_spec]
    out_shape = [jax.ShapeDtypeStruct((n_seq * seq_len, h * dv), F32),
                 jax.ShapeDtypeStruct(s0.shape, F32)]
    if meta is not None:
        qm, km, gm, vm, n_meta, meta_blk = meta
        mspec = lambda w: pl.BlockSpec((n_meta, w), lambda b, hh, i: (meta_blk, hh))
        in_specs += [mspec(dk), mspec(dk), mspec(dk), mspec(dv)]
        args += [qm, km, gm, vm]
        out_specs.append(pl.BlockSpec((n_meta, dv), lambda b, hh, i: (0, hh)))
        out_shape.append(jax.ShapeDtypeStruct((n_meta, h * dv), F32))
    return pl.pallas_call(
        functools.partial(_hgrn_rec_body, c=c, r=r, has_meta=meta is not None),
        grid=(n_seq, h, nblk),
        in_specs=in_specs,
        out_specs=out_specs,
        out_shape=out_shape,
        scratch_shapes=[pltpu.VMEM((dv, dk), F32)],
        compiler_params=_params(3),
        name="hgrn_rec",
    )(*args)


def _mix_out_body(*refs, head_dim):
    if head_dim:
        o_ref, gate_ref, x_ref, gn_ref, w_ref, pw_ref, y_ref = refs
        o = o_ref[...]
        parts = []
        for h in range(o.shape[1] // head_dim):
            oh = o[:, h * head_dim:(h + 1) * head_dim]
            parts.append(oh * lax.rsqrt(jnp.mean(oh * oh, axis=-1, keepdims=True) + EPS))
        o = (jnp.concatenate(parts, axis=-1) * gn_ref[...] * gate_ref[...]).astype(BF16)
    else:
        o_ref, x_ref, w_ref, pw_ref, y_ref = refs
        o = o_ref[...]
    m = _mm(o, w_ref[...])
    y_ref[...] = x_ref[...] + _rmsnorm(m, pw_ref[...])


def _mix_out(o, gate, x, gn, w_bf, pw, tm, head_dim):
    n, d = x.shape
    row = pl.BlockSpec((tm, d), lambda i: (i, 0))
    if head_dim:
        in_specs = [row, row, row, _resident((1, d)), _resident(w_bf.shape), _resident((1, d))]
        args = (o, gate, x, gn, w_bf, pw)
    else:
        in_specs = [row, row, _resident(w_bf.shape), _resident((1, d))]
        args = (o, x, w_bf, pw)
    return pl.pallas_call(
        functools.partial(_mix_out_body, head_dim=head_dim),
        grid=(n // tm,),
        in_specs=in_specs,
        out_specs=row,
        out_shape=jax.ShapeDtypeStruct((n, d), F32),
        compiler_params=_params(1),
        name="mix_out",
    )(*args)


def _ffn_body(x_ref, nw_ref, wgu_ref, wd_ref, pw_ref, y_ref, *, dff, fc):
    x = x_ref[...]
    xn = _rmsnorm(x, nw_ref[...]).astype(BF16)
    acc = jnp.zeros(x.shape, F32)
    for c0 in range(0, dff, fc):
        a = _mm(xn, wgu_ref[:, c0:c0 + fc])
        u = _mm(xn, wgu_ref[:, dff + c0:dff + c0 + fc])
        hid = (a * _sigmoid(a) * u).astype(BF16)
        acc = acc + _mm(hid, wd_ref[c0:c0 + fc, :])
    y_ref[...] = x + _rmsnorm(acc, pw_ref[...])


def _ffn(x, nw, wgu_bf, wd_bf, pw, tm):
    n, d = x.shape
    dff = wd_bf.shape[0]
    fc = dff // 2 if (dff // 2) % LANES == 0 else dff
    row = pl.BlockSpec((tm, d), lambda i: (i, 0))
    return pl.pallas_call(
        functools.partial(_ffn_body, dff=dff, fc=fc),
        grid=(n // tm,),
        in_specs=[row, _resident((1, d)), _resident(wgu_bf.shape), _resident(wd_bf.shape),
                  _resident((1, d))],
        out_specs=row,
        out_shape=jax.ShapeDtypeStruct((n, d), F32),
        compiler_params=_params(1),
        name="ffn",
    )(x, nw, wgu_bf, wd_bf, pw)


def _fox_proj_body(x_ref, nw_ref, w_ref, wf_ref, bf_ref, k_ref, v_ref, lf_ref,
                   qb_ref, kb_ref, vb_ref, *, scale):
    d = k_ref.shape[1]
    nh = lf_ref.shape[1]
    xn = _rmsnorm(x_ref[...], nw_ref[...]).astype(BF16)
    qb_ref[...] = (_mm(xn, w_ref[:, 0:d]) * scale).astype(BF16)
    k = _mm(xn, w_ref[:, d:2 * d])
    k_ref[...] = k
    kb_ref[...] = k.astype(BF16)
    v = _mm(xn, w_ref[:, 2 * d:3 * d])
    v_ref[...] = v
    vb_ref[...] = v.astype(BF16)
    lf = _log_sigmoid(_mm(xn, wf_ref[...]) + bf_ref[...])
    lf_ref[...] = lf[:, 0:nh]


def _fox_proj(x, nw, w_bf, wf_bf, bf_row, nh, tm):
    n, d = x.shape
    row = lambda w: pl.BlockSpec((tm, w), lambda i: (i, 0))
    scale = float(d // nh) ** -0.5
    return pl.pallas_call(
        functools.partial(_fox_proj_body, scale=scale),
        grid=(n // tm,),
        in_specs=[row(d), _resident((1, d)), _resident(w_bf.shape), _resident(wf_bf.shape),
                  _resident(bf_row.shape)],
        out_specs=[row(d), row(d), row(nh), row(d), row(d), row(d)],
        out_shape=[jax.ShapeDtypeStruct((n, d), F32), jax.ShapeDtypeStruct((n, d), F32),
                   jax.ShapeDtypeStruct((n, nh), F32), jax.ShapeDtypeStruct((n, d), BF16),
                   jax.ShapeDtypeStruct((n, d), BF16), jax.ShapeDtypeStruct((n, d), BF16)],
        compiler_params=_params(1),
        name="fox_proj",
    )(x, nw, w_bf, wf_bf, bf_row)


def _head_sq_norms(xt, nh):
    dh = xt.shape[0] // nh
    x2 = xt * xt
    return jnp.concatenate([jnp.sum(x2[h * dh:(h + 1) * dh], axis=0, keepdims=True)
                            for h in range(nh)], axis=0)


def _fox_proj_t_body(x_ref, nw_ref, wt_ref, wf_ref, bf_ref, kt_ref, vt_ref, lf_ref,
                     qtb_ref, kb_ref, vtb_ref, qsq_ref, ksq_ref, *, qscale):
    d = kt_ref.shape[0]
    nh = lf_ref.shape[1]
    xn = _rmsnorm(x_ref[...], nw_ref[...]).astype(BF16)
    qtb = (_mm_nt(wt_ref[0:d, :], xn) * qscale).astype(BF16)
    qtb_ref[...] = qtb
    qsq_ref[...] = _head_sq_norms(qtb.astype(F32), nh)
    kt = _mm_nt(wt_ref[d:2 * d, :], xn)
    kt_ref[...] = kt
    ksq_ref[...] = _head_sq_norms(kt.astype(BF16).astype(F32), nh)
    kb_ref[...] = _mm_nt(xn, wt_ref[d:2 * d, :]).astype(BF16)
    vt = _mm_nt(wt_ref[2 * d:3 * d, :], xn)
    vt_ref[...] = vt
    vtb_ref[...] = vt.astype(BF16)
    lf = _log_sigmoid(_mm(xn, wf_ref[...]) + bf_ref[...])
    lf_ref[...] = lf[:, 0:nh]


def _fox_proj_t(x, nw, wt_bf, wf_bf, bf_row, nh, tm, qscale):
    n, d = x.shape
    row = lambda w: pl.BlockSpec((tm, w), lambda i: (i, 0))
    col = pl.BlockSpec((d, tm), lambda i: (0, i))
    sq = pl.BlockSpec((nh, tm), lambda i: (0, i))
    return pl.pallas_call(
        functools.partial(_fox_proj_t_body, qscale=qscale),
        grid=(n // tm,),
        in_specs=[row(d), _resident((1, d)), _resident(wt_bf.shape), _resident(wf_bf.shape),
                  _resident(bf_row.shape)],
        out_specs=[col, col, row(nh), col, row(d), col, sq, sq],
        out_shape=[jax.ShapeDtypeStruct((d, n), F32), jax.ShapeDtypeStruct((d, n), F32),
                   jax.ShapeDtypeStruct((n, nh), F32), jax.ShapeDtypeStruct((d, n), BF16),
                   jax.ShapeDtypeStruct((n, d), BF16), jax.ShapeDtypeStruct((d, n), BF16),
                   jax.ShapeDtypeStruct((nh, n), F32), jax.ShapeDtypeStruct((nh, n), F32)],
        compiler_params=_params(1),
        name="fox_proj_t",
    )(x, nw, wt_bf, wf_bf, bf_row)


BIAS_LANES_PER_HEAD = 6


def _split3(v):
    hi = v.astype(BF16).astype(F32)
    r1 = v - hi
    mid = r1.astype(BF16).astype(F32)
    lo = (r1 - mid).astype(BF16).astype(F32)
    return hi, mid, lo


def _fox_kbias_body(c_ref, aux_ref, *, cscale):
    tm = c_ref.shape[0]
    npair = aux_ref.shape[0]
    c = c_ref[...] * (-cscale)
    lane = lax.broadcasted_iota(jnp.int32, (tm, LANES), 1)
    slot = lane % BIAS_LANES_PER_HEAD
    piece = slot % 3
    used = lane < 2 * BIAS_LANES_PER_HEAD
    for p in range(npair):
        c0 = jnp.broadcast_to(c[:, 2 * p:2 * p + 1], (tm, LANES))
        c1 = jnp.broadcast_to(c[:, 2 * p + 1:2 * p + 2], (tm, LANES))
        hi, mid, lo = _split3(jnp.where(lane < BIAS_LANES_PER_HEAD, c0, c1))
        val = jnp.where(piece == 0, hi, jnp.where(piece == 1, mid, lo))
        val = jnp.where(slot < 3, val, 1.0)
        aux_ref[p] = jnp.where(used, val, 0.0).astype(BF16)


def _fox_kbias(c_rows, npair, tm, cscale):
    n, nh = c_rows.shape
    return pl.pallas_call(
        functools.partial(_fox_kbias_body, cscale=cscale),
        grid=(n // tm,),
        in_specs=[pl.BlockSpec((tm, nh), lambda i: (i, 0))],
        out_specs=pl.BlockSpec((npair, tm, LANES), lambda i: (0, i, 0)),
        out_shape=jax.ShapeDtypeStruct((npair, n, LANES), BF16),
        compiler_params=_params(1),
        name="fox_kbias",
    )(c_rows)


def _lane_cumsum_body(x_ref, y_ref, *, reverse_exclusive):
    rows, n = x_ref.shape
    nb = n // LANES
    t_i = lax.broadcasted_iota(jnp.int32, (LANES, LANES), 0)
    k_i = lax.broadcasted_iota(jnp.int32, (LANES, LANES), 1)
    sel = (t_i > k_i) if reverse_exclusive else (t_i <= k_i)
    tri = jnp.where(sel, 1.0, 0.0).astype(F32)

    def body(i, carry):
        b = (nb - 1 - i) if reverse_exclusive else i
        sl = pl.ds(pl.multiple_of(b * LANES, LANES), LANES)
        xb = x_ref[:, sl]
        y_ref[:, sl] = jnp.dot(xb, tri, precision=HIGHEST, preferred_element_type=F32) + carry
        return carry + jnp.sum(xb, axis=-1, keepdims=True)

    lax.fori_loop(0, nb, body, jnp.zeros((rows, 1), F32))


def _lane_cumsum(x, reverse_exclusive, tr):
    rows, n = x.shape
    spec = pl.BlockSpec((tr, n), lambda i: (i, 0))
    return pl.pallas_call(
        functools.partial(_lane_cumsum_body, reverse_exclusive=reverse_exclusive),
        grid=(rows // tr,),
        in_specs=[spec],
        out_specs=spec,
        out_shape=jax.ShapeDtypeStruct((rows, n), F32),
        compiler_params=_params(1),
        name="lane_cumsum",
    )(x)


ATTN_TILE = 512
ACC_PAD = 16


def _fox_attn_body(jlo_ref, c_ref, qt_ref, k_ref, ka_ref, vt_ref, kme_ref, vtm_ref,
                   qm_ref, km_ref, vm_ref, o_ref, om_ref, qx_s, m_s, acc_s,
                   *, tq, tk, dh, n_meta, cscale):
    i = pl.program_id(1)
    lane = lax.broadcasted_iota(jnp.int32, (1, 2 * dh), 1)
    head_sel = [lane < dh, lane >= dh]

    @pl.when(i == 0)
    def _():
        r_i = lax.broadcasted_iota(jnp.int32, (n_meta, n_meta), 0)
        c_i = lax.broadcasted_iota(jnp.int32, (n_meta, n_meta), 1)
        out = jnp.zeros((n_meta, 2 * dh), F32)
        for a in range(2):
            qa = jnp.where(head_sel[a], qm_ref[...], jnp.zeros_like(qm_ref[...]))
            x = _mm_nt(qa, km_ref[...]) - c_ref[a, :, 0:n_meta]
            x = jnp.where(c_i <= r_i, x, NEG)
            p = jnp.exp(x - jnp.max(x, axis=-1, keepdims=True))
            oa = _mm(p.astype(BF16), vm_ref[...]) / jnp.sum(p, axis=-1, keepdims=True)
            out = jnp.where(head_sel[a], oa, out)
        om_ref[...] = out.astype(BF16)

    q0 = pl.multiple_of(LANES + i * tq, LANES)
    row = lax.broadcasted_iota(jnp.int32, (2 * dh, tq), 0)
    qt = qt_ref[...]
    for a in range(2):
        hi, mid, lo = _split3(c_ref[a, :, pl.ds(q0, tq)] * cscale)
        slot = row - a * BIAS_LANES_PER_HEAD
        bias = jnp.where(slot == 3, hi, jnp.where(slot == 4, mid, jnp.where(slot == 5, lo, 1.0)))
        bias = jnp.where((slot >= 0) & (slot < BIAS_LANES_PER_HEAD), bias, 0.0)
        qx_s[a, 0:2 * dh, :] = jnp.where((row >= a * dh) & (row < (a + 1) * dh), qt,
                                         jnp.zeros_like(qt))
        qx_s[a, 2 * dh:4 * dh, :] = bias.astype(BF16)
        m_s[a] = jnp.full((1, tq), NEG, F32)
        acc_s[a] = jnp.zeros((dh + ACC_PAD, tq), F32)

    def tile(kx, vt, masked):
        tkb = kx.shape[0]
        ones = jnp.where(lax.broadcasted_iota(jnp.int32, (ACC_PAD, tkb), 0) == 0,
                         1.0, 0.0).astype(BF16)
        for a in range(2):
            st = _mm(kx, qx_s[a])
            if masked:
                k_i = lax.broadcasted_iota(jnp.int32, (tkb, tq), 0)
                q_i = lax.broadcasted_iota(jnp.int32, (tkb, tq), 1)
                st = jnp.where(k_i <= q_i, st, NEG)
            m_prev = m_s[a]
            m_new = jnp.maximum(m_prev, jnp.max(st, axis=0, keepdims=True))
            p = jnp.exp2(st - m_new).astype(BF16)
            vx = jnp.concatenate([vt[a * dh:(a + 1) * dh, :], ones], axis=0)
            acc_s[a] = jnp.exp2(m_prev - m_new) * acc_s[a] + _mm(vx, p)
            m_s[a] = m_new

    tile(kme_ref[0], vtm_ref[...], False)

    def key_block(j):
        k0 = pl.multiple_of(j * tk, tk)
        kx = jnp.concatenate([k_ref[pl.ds(k0, tk), :], ka_ref[0, pl.ds(k0, tk), :]], axis=1)
        return kx, vt_ref[:, pl.ds(k0, tk)]

    def body(j, carry):
        tile(*key_block(j), False)
        return carry

    lax.fori_loop(jlo_ref[pl.program_id(0), i], i, body, 0)
    tile(*key_block(i), True)

    halves = []
    for a in range(2):
        acc = acc_s[a]
        halves.append(acc[0:dh] / acc[dh:dh + 1])
    o_ref[...] = jnp.concatenate(halves, axis=0).T.astype(BF16)


def _fox_attn(jlo, c3, qt_b, kb, kaux, vt_b, kx_meta, vt_meta, qb_aux, kb_aux, vb_aux,
              n_meta, meta_blk, dh, tq, cscale):
    n, d = kb.shape
    npair = d // (2 * dh)
    nl = c3.shape[2]
    mspec = pl.BlockSpec((n_meta, 2 * dh), lambda p, i, _: (meta_blk, p))
    return pl.pallas_call(
        functools.partial(_fox_attn_body, tq=tq, tk=tq, dh=dh, n_meta=n_meta, cscale=cscale),
        grid_spec=pltpu.PrefetchScalarGridSpec(
            num_scalar_prefetch=1,
            grid=(npair, n // tq),
            in_specs=[pl.BlockSpec((2, 1, nl), lambda p, i, _: (p, 0, 0)),
                      pl.BlockSpec((2 * dh, tq), lambda p, i, _: (p, i)),
                      pl.BlockSpec((n, 2 * dh), lambda p, i, _: (0, p)),
                      pl.BlockSpec((1, n, LANES), lambda p, i, _: (p, 0, 0)),
                      pl.BlockSpec((2 * dh, n), lambda p, i, _: (p, 0)),
                      pl.BlockSpec((1, n_meta, 4 * dh), lambda p, i, _: (p, 0, 0)),
                      pl.BlockSpec((2 * dh, n_meta), lambda p, i, _: (p, 0)),
                      mspec, mspec, mspec],
            out_specs=[pl.BlockSpec((tq, 2 * dh), lambda p, i, _: (i, p)),
                       pl.BlockSpec((n_meta, 2 * dh), lambda p, i, _: (0, p))],
            scratch_shapes=[pltpu.VMEM((2, 4 * dh, tq), BF16), pltpu.VMEM((2, 1, tq), F32),
                            pltpu.VMEM((2, dh + ACC_PAD, tq), F32)]),
        out_shape=[jax.ShapeDtypeStruct((n, d), BF16), jax.ShapeDtypeStruct((n_meta, d), BF16)],
        compiler_params=_params(2),
        name="fox_attn",
    )(jlo, c3, qt_b, kb, kaux, vt_b, kx_meta, vt_meta, qb_aux, kb_aux, vb_aux)


PRUNE_NATS = 50.0
NORM_SLACK = 1.01


def _fox_first_key_block(c_main, qsq, ksq, tq):
    nh, n = c_main.shape
    nq = n // tq
    u = jnp.sqrt(jnp.max(qsq, axis=1) * jnp.max(ksq, axis=1)) * (NORM_SLACK / LOG2E)
    c_first = c_main[:, 0::tq]
    c_last = c_main[:, tq - 1::tq]
    bound = 2.0 * u[:, None, None] + c_first[:, :, None] - c_last[:, None, :]
    j_lt_i = jnp.arange(nq)[None, :] < jnp.arange(nq)[:, None]
    skip = (bound <= -PRUNE_NATS) & j_lt_i[None]
    jlo = jnp.sum(skip, axis=2).astype(jnp.int32)
    return jnp.min(jlo.reshape(nh // 2, 2, nq), axis=1)


def _fox_sample_body(q_ref, kn_ref, vn_ref, cn_ref, kc_ref, vc_ref, suf_ref, o_ref,
                     m_s, l_s, acc_s, *, nh, dh, s_len):
    t = pl.program_id(1)

    @pl.when(t == 0)
    def _():
        r_i = lax.broadcasted_iota(jnp.int32, (s_len, s_len), 0)
        c_i = lax.broadcasted_iota(jnp.int32, (s_len, s_len), 1)
        for h in range(nh):
            cols = slice(h * dh, (h + 1) * dh)
            x = _mm_nt(q_ref[:, cols], kn_ref[:, cols]) - cn_ref[0, h:h + 1, 0:s_len]
            x = jnp.where(c_i <= r_i, x, NEG)
            m = jnp.max(x, axis=-1, keepdims=True)
            p = jnp.exp(x - m)
            m_s[h] = m
            l_s[h] = jnp.sum(p, axis=-1, keepdims=True)
            acc_s[h] = _mm(p.astype(BF16), vn_ref[:, cols])

    for h in range(nh):
        x = _mm(q_ref[:, h * dh:(h + 1) * dh], kc_ref[0, h].astype(BF16)) + suf_ref[0, h:h + 1, :]
        m_prev = m_s[h]
        m_new = jnp.maximum(m_prev, jnp.max(x, axis=-1, keepdims=True))
        alpha = jnp.exp(m_prev - m_new)
        p = jnp.exp(x - m_new)
        l_s[h] = alpha * l_s[h] + jnp.sum(p, axis=-1, keepdims=True)
        acc_s[h] = alpha * acc_s[h] + _mm_nt(p.astype(BF16), vc_ref[0, h].astype(BF16))
        m_s[h] = m_new

    @pl.when(t == pl.num_programs(1) - 1)
    def _():
        o_ref[...] = jnp.concatenate([acc_s[h] / l_s[h] for h in range(nh)], axis=-1).astype(BF16)


def _fox_sample(qb, kb, vb, cn, kc_t, vc_t, layer, suf, s_len, tk):
    _, nb, nh, dh, past = kc_t.shape
    d = nh * dh
    tok = pl.BlockSpec((s_len, d), lambda b, t: (b, 0))
    cache = pl.BlockSpec((None, 1, nh, dh, tk), lambda b, t: (layer, b, 0, 0, t))
    return pl.pallas_call(
        functools.partial(_fox_sample_body, nh=nh, dh=dh, s_len=s_len),
        grid=(nb, past // tk),
        in_specs=[tok, tok, tok, pl.BlockSpec((1, nh, LANES), lambda b, t: (b, 0, 0)),
                  cache, cache, pl.BlockSpec((1, nh, tk), lambda b, t: (b, 0, t))],
        out_specs=tok,
        out_shape=jax.ShapeDtypeStruct((nb * s_len, d), BF16),
        scratch_shapes=[pltpu.VMEM((nh, s_len, 1), F32), pltpu.VMEM((nh, s_len, 1), F32),
                        pltpu.VMEM((nh, s_len, dh), F32)],
        compiler_params=_params(2),
        name="fox_sample",
    )(qb, kb, vb, cn, kc_t, vc_t, suf)


def _largest_tile(n, cap):
    t = min(n, cap)
    while n % t:
        t //= 2
    return t


def kernel(x_prompt, x_sample, state_hgrn, cache_k, cache_v, cache_logf, meta_tokens,
           norm_pre_mix, norm_post_mix, norm_pre_ffn, norm_post_ffn,
           a_w_in, a_lb_logits, a_g_norm, a_w_out, b_w_in, b_f, b_w_out,
           ffn_w_gu, ffn_w_down):
    batch, seq, d = x_prompt.shape
    assert batch == 1, "one prompt stream per step"
    n_dec, s_len, _ = x_sample.shape
    n_meta = meta_tokens.shape[0]
    depth = norm_pre_mix.shape[0]
    _, _, a_heads, a_dk, a_dv = state_hgrn.shape
    _, _, past, b_heads, b_dh = cache_k.shape
    ns = n_dec * s_len
    assert s_len % HGRN_SUB == 0 and n_meta % HGRN_SUB == 0 and ns % n_meta == 0
    assert 2 * b_dh == LANES and a_dk == LANES and a_dv == LANES

    tm = _largest_tile(seq, 512)
    n_aux = ns + n_meta
    meta_blk = ns // n_meta
    rec_r = _largest_tile(seq, 512)
    rec_c = _largest_tile(rec_r, 64)
    tq = _largest_tile(seq, ATTN_TILE)
    tk_cache = _largest_tile(past, 1024)
    npair = b_heads // 2

    sm = jax.nn.softmax(a_lb_logits.astype(F32), axis=0)
    lb_all = jnp.cumsum(sm, axis=0) - sm[0]

    xm = x_prompt.reshape(seq, d)
    xa = jnp.concatenate([x_sample.reshape(ns, d), meta_tokens.astype(F32)], axis=0)
    row = lambda w: w.reshape(1, -1).astype(F32)
    cache_kt = jnp.transpose(cache_k, (0, 1, 3, 4, 2))
    cache_vt = jnp.transpose(cache_v, (0, 1, 3, 4, 2))
    cache_lt = jnp.swapaxes(cache_logf.astype(F32), 2, 3)

    st_p, st_s, k_p, v_p, lf_p, k_s, v_s, lf_s = [], [], [], [], [], [], [], []
    for i in range(depth):
        j = i // 2
        if i % 2 == 0:
            lb = lb_all[j]
            lbp = jnp.stack([jnp.log(jnp.maximum(lb, LB_FLOOR)), jnp.log1p(-lb), 1.0 - lb])
            w_in = a_w_in[j].astype(BF16)
            w_out = a_w_out[j].astype(BF16)
            qm, km, gm, vm, gtm = _hgrn_proj(xm, row(norm_pre_mix[i]), w_in, lbp, tm)
            qa, ka, ga, va, gta = _hgrn_proj(xa, row(norm_pre_mix[i]), w_in, lbp, n_aux)
            o_s, s_s = _hgrn_rec(qa, ka, ga, va, state_hgrn[j].astype(F32),
                                 n_seq=n_dec, seq_len=s_len, r=s_len, c=s_len)
            o_m, s_p, o_meta = _hgrn_rec(qm, km, gm, vm,
                                         jnp.zeros((1, a_heads, a_dk, a_dv), F32),
                                         n_seq=1, seq_len=seq, r=rec_r, c=rec_c,
                                         meta=(qa, ka, ga, va, n_meta, meta_blk))
            st_p.append(s_p.astype(state_hgrn.dtype))
            st_s.append(s_s.astype(state_hgrn.dtype))
            o_a = jnp.concatenate([o_s, o_meta], axis=0)
            xm = _mix_out(o_m, gtm, xm, row(a_g_norm[j]), w_out, row(norm_post_mix[i]), tm, a_dv)
            xa = _mix_out(o_a, gta, xa, row(a_g_norm[j]), w_out, row(norm_post_mix[i]), n_aux, a_dv)
        else:
            w_t = jnp.swapaxes(b_w_in[j], 0, 1)
            wt_qkv = w_t[:3 * d].astype(BF16)
            w_in = jnp.swapaxes(wt_qkv, 0, 1)
            w_f = jnp.pad(b_w_in[j][:, 3 * d:], ((0, 0), (0, LANES - b_heads))).astype(BF16)
            bf_row = jnp.pad(b_f[j].astype(F32), (0, LANES - b_heads)).reshape(1, LANES)
            w_out = b_w_out[j].astype(BF16)
            scale = float(b_dh) ** -0.5
            kt_m, vt_m, lf_m, qtb_m, kb_m, vtb_m, qsq, ksq = _fox_proj_t(
                xm, row(norm_pre_mix[i]), wt_qkv, w_f, bf_row, b_heads, tm, scale * LOG2E)
            kf_a, vf_a, lf_a, qb_a, kb_a, vb_a = _fox_proj(
                xa, row(norm_pre_mix[i]), w_in, w_f, bf_row, b_heads, n_aux)
            lf_t = jnp.concatenate(
                [jnp.pad(lf_a[ns:].T, ((0, 0), (0, LANES - n_meta))), lf_m.T], axis=1)
            c_all = _lane_cumsum(lf_t, False, b_heads)
            kaux = _fox_kbias(c_all[:, LANES:].T, npair, tm, LOG2E)
            kaux_meta = _fox_kbias(c_all[:, :n_meta].T, npair, n_meta, LOG2E)
            k_meta = jnp.swapaxes(kb_a[ns:].reshape(n_meta, npair, LANES), 0, 1)
            jlo = _fox_first_key_block(c_all[:, LANES:], qsq, ksq, tq)
            o_m, o_meta = _fox_attn(jlo, c_all.reshape(b_heads, 1, -1), qtb_m, kb_m, kaux, vtb_m,
                                    jnp.concatenate([k_meta, kaux_meta], axis=-1), vb_a[ns:].T,
                                    qb_a, kb_a, vb_a, n_meta, meta_blk, b_dh, tq, LOG2E)
            cl_t = cache_lt[j].reshape(n_dec * b_heads, past)
            tr = _largest_tile(n_dec * b_heads, LANES)
            suf = _lane_cumsum(cl_t, True, tr).reshape(n_dec, b_heads, past)
            ln_t = jnp.swapaxes(lf_a[:ns].reshape(n_dec, s_len, b_heads), 1, 2)
            ln_t = jnp.pad(ln_t, ((0, 0), (0, 0), (0, LANES - s_len)))
            cn = _lane_cumsum(ln_t.reshape(n_dec * b_heads, LANES), False, tr)
            o_s = _fox_sample(qb_a, kb_a, vb_a, cn.reshape(n_dec, b_heads, LANES),
                              cache_kt, cache_vt, j, suf, s_len, tk_cache)
            o_a = jnp.concatenate([o_s, o_meta], axis=0)
            xm = _mix_out(o_m, None, xm, None, w_out, row(norm_post_mix[i]), tm, 0)
            xa = _mix_out(o_a, None, xa, None, w_out, row(norm_post_mix[i]), n_aux, 0)
            kv_out = lambda t_main, f_aux: jnp.transpose(
                jnp.concatenate([f_aux[ns:].T, t_main], axis=1).reshape(b_heads, b_dh, n_meta + seq),
                (2, 0, 1))[None]
            shp = lambda t, n: t.reshape(-1, n, b_heads, b_dh)
            k_p.append(kv_out(kt_m, kf_a))
            v_p.append(kv_out(vt_m, vf_a))
            lf_p.append(jnp.concatenate([lf_a[ns:], lf_m], axis=0).reshape(1, n_meta + seq, b_heads))
            k_s.append(shp(kf_a[:ns], s_len))
            v_s.append(shp(vf_a[:ns], s_len))
            lf_s.append(lf_a[:ns].reshape(n_dec, s_len, b_heads))
        wgu = ffn_w_gu[i].astype(BF16)
        wd = ffn_w_down[i].astype(BF16)
        xm = _ffn(xm, row(norm_pre_ffn[i]), wgu, wd, row(norm_post_ffn[i]), tm)
        xa = _ffn(xa, row(norm_pre_ffn[i]), wgu, wd, row(norm_post_ffn[i]), n_aux)

    return (xm.reshape(1, seq, d), xa[:ns].reshape(n_dec, s_len, d),
            jnp.stack(st_p), jnp.stack(k_p), jnp.stack(v_p), jnp.stack(lf_p),
            jnp.stack(st_s), jnp.stack(k_s), jnp.stack(v_s), jnp.stack(lf_s))
```

```python
import functools

import jax
import jax.numpy as jnp
from jax import lax
from jax.experimental import pallas as pl
from jax.experimental.pallas import tpu as pltpu

F32 = jnp.float32
BF16 = jnp.bfloat16
EPS = 1e-6
NEG = -1e30
LB_FLOOR = 1e-30

LANES = 128
VMEM_LIMIT = 56 * 1024 * 1024
HGRN_SUB = 16
HIGHEST = lax.Precision.HIGHEST
LOG2E = 1.4426950408889634


def _params(n_grid):
    return pltpu.CompilerParams(dimension_semantics=("arbitrary",) * n_grid,
                                vmem_limit_bytes=VMEM_LIMIT)


def _resident(shape):
    nd = len(shape)
    return pl.BlockSpec(shape, lambda *_: (0,) * nd, pipeline_mode=pl.Buffered(1))


def _rmsnorm(x, w):
    return x * lax.rsqrt(jnp.mean(x * x, axis=-1, keepdims=True) + EPS) * w


def _sigmoid(x):
    return 1.0 / (1.0 + jnp.exp(-x))


def _log_sigmoid(x):
    return jnp.minimum(x, 0.0) - jnp.log1p(jnp.exp(-jnp.abs(x)))


def _mm(a, b):
    return jnp.dot(a, b, preferred_element_type=F32)


def _mm_nt(a, b):
    return lax.dot_general(a, b, (((1,), (1,)), ((), ())), preferred_element_type=F32)


def _mm_tn(a, b):
    return lax.dot_general(a, b, (((0,), (0,)), ((), ())), preferred_element_type=F32)


def _hgrn_proj_body(x_ref, nw_ref, w_ref, lb_ref, q_ref, k_ref, g_ref, v_ref, gate_ref):
    f = q_ref.shape[1]
    d = v_ref.shape[1]
    xn = _rmsnorm(x_ref[...], nw_ref[...]).astype(BF16)
    q = _mm(xn, w_ref[:, 0:f])
    q_ref[...] = q * _sigmoid(q)
    fl = _mm(xn, w_ref[:, f:2 * f])
    e = jnp.exp(-jnp.abs(fl))
    r = 1.0 / (1.0 + e)
    a = lb_ref[0:1, :]
    b = lb_ref[1:2, :] + (jnp.minimum(fl, 0.0) - jnp.log1p(e))
    g_ref[...] = jnp.maximum(a, b) + jnp.log1p(jnp.exp(-jnp.abs(a - b)))
    k_ref[...] = lb_ref[2:3, :] * jnp.where(fl >= 0.0, e * r, r)
    v_ref[...] = _mm(xn, w_ref[:, 2 * f:2 * f + d])
    gt = _mm(xn, w_ref[:, 2 * f + d:])
    gate_ref[...] = gt * _sigmoid(gt)


def _hgrn_proj(x, nw, w_bf, lbp, tm):
    n, d = x.shape
    f = lbp.shape[1]
    row = lambda w: pl.BlockSpec((tm, w), lambda i: (i, 0))
    return pl.pallas_call(
        _hgrn_proj_body,
        grid=(n // tm,),
        in_specs=[row(d), _resident((1, d)), _resident(w_bf.shape), _resident(lbp.shape)],
        out_specs=[row(f), row(f), row(f), row(d), row(d)],
        out_shape=[jax.ShapeDtypeStruct((n, w), F32) for w in (f, f, f, d, d)],
        compiler_params=_params(1),
        name="hgrn_proj",
    )(x, nw, w_bf, lbp)


def _hgrn_chunk(q, k, g, v, sts, c):
    nh = len(sts)
    sub, half = HGRN_SUB, HGRN_SUB // 2
    hs = lambda x, h: x[:, h * LANES:(h + 1) * LANES]
    r_i = lax.broadcasted_iota(jnp.int32, (c, c), 0)
    c_i = lax.broadcasted_iota(jnp.int32, (c, c), 1)
    tril = jnp.where(r_i >= c_i, 1.0, 0.0).astype(F32)
    gc = jnp.dot(tril, g, precision=HIGHEST, preferred_element_type=F32) * LOG2E
    g_last = gc[c - 1:c, :]
    qe = (q * jnp.exp2(gc)).astype(BF16)
    o_inter = [_mm_nt(hs(qe, h), sts[h].astype(BF16)) for h in range(nh)]
    kd = (k * jnp.exp2(g_last - gc)).astype(BF16)
    vb = v.astype(BF16)
    st_dec = jnp.exp2(g_last)
    st_new = [sts[h] * hs(st_dec, h) + _mm_tn(hs(vb, h), hs(kd, h)) for h in range(nh)]

    half_row = lax.broadcasted_iota(jnp.int32, (half, 1), 0)
    out_rows = []
    for i in range(c // sub):
        r0 = i * sub
        gi, qi, ki, vi = gc[r0:r0 + sub], q[r0:r0 + sub], k[r0:r0 + sub], v[r0:r0 + sub]
        o_lo = [o_inter[h][r0:r0 + half] for h in range(nh)]
        o_hi = [o_inter[h][r0 + half:r0 + sub] for h in range(nh)]
        if i > 0:
            g_first = gc[r0:r0 + 1]
            qs = (qi * jnp.exp2(gi - g_first)).astype(BF16)
            ks = (k[:r0] * jnp.exp2(g_first - gc[:r0])).astype(BF16)
            att = [_mm_nt(hs(qs, h), hs(ks, h)).astype(BF16) for h in range(nh)]
            off = [_mm(att[h], hs(vb[:r0], h)) for h in range(nh)]
            o_lo = [o_lo[h] + off[h][:half] for h in range(nh)]
            o_hi = [o_hi[h] + off[h][half:] for h in range(nh)]
        for rows, acc, s_range in ((slice(0, half), o_lo, range(half)),
                                   (slice(half, sub), o_hi, range(sub))):
            g_rows, q_rows = gi[rows], qi[rows]
            for s in s_range:
                diff = g_rows - gi[s:s + 1]
                if s >= rows.start:
                    diff = jnp.where(half_row >= s - rows.start, diff, NEG)
                prod = q_rows * jnp.exp2(diff) * ki[s:s + 1]
                for h in range(nh):
                    col = jnp.sum(hs(prod, h), axis=-1, keepdims=True)
                    acc[h] = acc[h] + col * hs(vi, h)[s:s + 1]
        out_rows.append(jnp.concatenate(
            [jnp.concatenate([o_lo[h], o_hi[h]], axis=0) for h in range(nh)], axis=1))
    o = jnp.concatenate(out_rows, axis=0) if len(out_rows) > 1 else out_rows[0]
    return o, st_new


def _hgrn_rec_body(*refs, c, r, has_meta):
    if has_meta:
        (q_ref, k_ref, g_ref, v_ref, s0_ref, qm_ref, km_ref, gm_ref, vm_ref,
         o_ref, s_ref, om_ref, st_ref) = refs
    else:
        q_ref, k_ref, g_ref, v_ref, s0_ref, o_ref, s_ref, st_ref = refs
    blk = pl.program_id(1)
    nh = st_ref.shape[0]

    @pl.when(blk == 0)
    def _():
        sts = [s0_ref[0, h].T for h in range(nh)]
        if has_meta:
            o_m, sts = _hgrn_chunk(qm_ref[...], km_ref[...], gm_ref[...], vm_ref[...],
                                   sts, qm_ref.shape[0])
            om_ref[...] = o_m
        for h in range(nh):
            st_ref[h] = sts[h]

    def body(j, carry):
        sl = pl.ds(pl.multiple_of(j * c, c), c)
        o, sts = _hgrn_chunk(q_ref[sl, :], k_ref[sl, :], g_ref[sl, :], v_ref[sl, :],
                             [st_ref[h] for h in range(nh)], c)
        o_ref[sl, :] = o
        for h in range(nh):
            st_ref[h] = sts[h]
        return carry

    lax.fori_loop(0, r // c, body, 0)

    @pl.when(blk == pl.num_programs(1) - 1)
    def _():
        for h in range(nh):
            s_ref[0, h] = st_ref[h].T


def _hgrn_rec(q, k, g, v, s0, *, n_seq, seq_len, r, c, meta=None):
    _, h, dk, dv = s0.shape
    nblk = seq_len // r
    tok = lambda w: pl.BlockSpec((r, h * w), lambda b, i: (b * nblk + i, 0))
    st_spec = pl.BlockSpec((1, h, dk, dv), lambda b, i: (b, 0, 0, 0))
    in_specs = [tok(dk), tok(dk), tok(dk), tok(dv), st_spec]
    args = [q, k, g, v, s0]
    out_specs = [tok(dv), st_spec]
    out_shape = [jax.ShapeDtypeStruct((n_seq * seq_len, h * dv), F32),
                 jax.ShapeDtypeStruct(s0.shape, F32)]
    if meta is not None:
        qm, km, gm, vm, n_meta, meta_blk = meta
        mspec = lambda w: pl.BlockSpec((n_meta, h * w), lambda b, i: (meta_blk, 0))
        in_specs += [mspec(dk), mspec(dk), mspec(dk), mspec(dv)]
        args += [qm, km, gm, vm]
        out_specs.append(pl.BlockSpec((n_meta, h * dv), lambda b, i: (0, 0)))
        out_shape.append(jax.ShapeDtypeStruct((n_meta, h * dv), F32))
    return pl.pallas_call(
        functools.partial(_hgrn_rec_body, c=c, r=r, has_meta=meta is not None),
        grid=(n_seq, nblk),
        in_specs=in_specs,
        out_specs=out_specs,
        out_shape=out_shape,
        scratch_shapes=[pltpu.VMEM((h, dv, dk), F32)],
        compiler_params=_params(2),
        name="hgrn_rec",
    )(*args)


def _mix_out_body(*refs, head_dim):
    if head_dim:
        o_ref, gate_ref, x_ref, gn_ref, w_ref, pw_ref, y_ref = refs
        o = o_ref[...]
        parts = []
        for h in range(o.shape[1] // head_dim):
            oh = o[:, h * head_dim:(h + 1) * head_dim]
            parts.append(oh * lax.rsqrt(jnp.mean(oh * oh, axis=-1, keepdims=True) + EPS))
        o = (jnp.concatenate(parts, axis=-1) * gn_ref[...] * gate_ref[...]).astype(BF16)
    else:
        o_ref, x_ref, w_ref, pw_ref, y_ref = refs
        o = o_ref[...]
    m = _mm(o, w_ref[...])
    y_ref[...] = x_ref[...] + _rmsnorm(m, pw_ref[...])


def _mix_out(o, gate, x, gn, w_bf, pw, tm, head_dim):
    n, d = x.shape
    row = pl.BlockSpec((tm, d), lambda i: (i, 0))
    if head_dim:
        in_specs = [row, row, row, _resident((1, d)), _resident(w_bf.shape), _resident((1, d))]
        args = (o, gate, x, gn, w_bf, pw)
    else:
        in_specs = [row, row, _resident(w_bf.shape), _resident((1, d))]
        args = (o, x, w_bf, pw)
    return pl.pallas_call(
        functools.partial(_mix_out_body, head_dim=head_dim),
        grid=(n // tm,),
        in_specs=in_specs,
        out_specs=row,
        out_shape=jax.ShapeDtypeStruct((n, d), F32),
        compiler_params=_params(1),
        name="mix_out",
    )(*args)


def _ffn_body(x_ref, nw_ref, wgu_ref, wd_ref, pw_ref, y_ref, *, dff, fc):
    x = x_ref[...]
    xn = _rmsnorm(x, nw_ref[...]).astype(BF16)
    acc = jnp.zeros(x.shape, F32)
    for c0 in range(0, dff, fc):
        a = _mm(xn, wgu_ref[:, c0:c0 + fc])
        u = _mm(xn, wgu_ref[:, dff + c0:dff + c0 + fc])
        hid = (a * _sigmoid(a) * u).astype(BF16)
        acc = acc + _mm(hid, wd_ref[c0:c0 + fc, :])
    y_ref[...] = x + _rmsnorm(acc, pw_ref[...])


def _ffn(x, nw, wgu_bf, wd_bf, pw, tm):
    n, d = x.shape
    dff = wd_bf.shape[0]
    fc = dff // 2 if (dff // 2) % LANES == 0 else dff
    row = pl.BlockSpec((tm, d), lambda i: (i, 0))
    return pl.pallas_call(
        functools.partial(_ffn_body, dff=dff, fc=fc),
        grid=(n // tm,),
        in_specs=[row, _resident((1, d)), _resident(wgu_bf.shape), _resident(wd_bf.shape),
                  _resident((1, d))],
        out_specs=row,
        out_shape=jax.ShapeDtypeStruct((n, d), F32),
        compiler_params=_params(1),
        name="ffn",
    )(x, nw, wgu_bf, wd_bf, pw)


def _fox_proj_body(x_ref, nw_ref, w_ref, wf_ref, bf_ref, k_ref, v_ref, lf_ref,
                   qb_ref, kb_ref, vb_ref, *, scale):
    d = k_ref.shape[1]
    nh = lf_ref.shape[1]
    xn = _rmsnorm(x_ref[...], nw_ref[...]).astype(BF16)
    qb_ref[...] = (_mm(xn, w_ref[:, 0:d]) * scale).astype(BF16)
    k = _mm(xn, w_ref[:, d:2 * d])
    k_ref[...] = k
    kb_ref[...] = k.astype(BF16)
    v = _mm(xn, w_ref[:, 2 * d:3 * d])
    v_ref[...] = v
    vb_ref[...] = v.astype(BF16)
    lf = _log_sigmoid(_mm(xn, wf_ref[...]) + bf_ref[...])
    lf_ref[...] = lf[:, 0:nh]


def _fox_proj(x, nw, w_bf, wf_bf, bf_row, nh, tm):
    n, d = x.shape
    row = lambda w: pl.BlockSpec((tm, w), lambda i: (i, 0))
    scale = float(d // nh) ** -0.5
    return pl.pallas_call(
        functools.partial(_fox_proj_body, scale=scale),
        grid=(n // tm,),
        in_specs=[row(d), _resident((1, d)), _resident(w_bf.shape), _resident(wf_bf.shape),
                  _resident(bf_row.shape)],
        out_specs=[row(d), row(d), row(nh), row(d), row(d), row(d)],
        out_shape=[jax.ShapeDtypeStruct((n, d), F32), jax.ShapeDtypeStruct((n, d), F32),
                   jax.ShapeDtypeStruct((n, nh), F32), jax.ShapeDtypeStruct((n, d), BF16),
                   jax.ShapeDtypeStruct((n, d), BF16), jax.ShapeDtypeStruct((n, d), BF16)],
        compiler_params=_params(1),
        name="fox_proj",
    )(x, nw, w_bf, wf_bf, bf_row)


def _head_sq_norms(xt, nh):
    dh = xt.shape[0] // nh
    x2 = xt * xt
    return jnp.concatenate([jnp.sum(x2[h * dh:(h + 1) * dh], axis=0, keepdims=True)
                            for h in range(nh)], axis=0)


def _fox_proj_t_body(x_ref, nw_ref, wt_ref, wf_ref, bf_ref, kt_ref, vt_ref, lf_ref,
                     qtb_ref, kb_ref, vtb_ref, qsq_ref, ksq_ref, *, qscale):
    d = kt_ref.shape[0]
    nh = lf_ref.shape[1]
    xn = _rmsnorm(x_ref[...], nw_ref[...]).astype(BF16)
    qtb = (_mm_nt(wt_ref[0:d, :], xn) * qscale).astype(BF16)
    qtb_ref[...] = qtb
    qsq_ref[...] = _head_sq_norms(qtb.astype(F32), nh)
    kt = _mm_nt(wt_ref[d:2 * d, :], xn)
    kt_ref[...] = kt
    ksq_ref[...] = _head_sq_norms(kt.astype(BF16).astype(F32), nh)
    kb_ref[...] = _mm_nt(xn, wt_ref[d:2 * d, :]).astype(BF16)
    vt = _mm_nt(wt_ref[2 * d:3 * d, :], xn)
    vt_ref[...] = vt
    vtb_ref[...] = vt.astype(BF16)
    lf = _log_sigmoid(_mm(xn, wf_ref[...]) + bf_ref[...])
    lf_ref[...] = lf[:, 0:nh]


def _fox_proj_t(x, nw, wt_bf, wf_bf, bf_row, nh, tm, qscale):
    n, d = x.shape
    row = lambda w: pl.BlockSpec((tm, w), lambda i: (i, 0))
    col = pl.BlockSpec((d, tm), lambda i: (0, i))
    sq = pl.BlockSpec((nh, tm), lambda i: (0, i))
    return pl.pallas_call(
        functools.partial(_fox_proj_t_body, qscale=qscale),
        grid=(n // tm,),
        in_specs=[row(d), _resident((1, d)), _resident(wt_bf.shape), _resident(wf_bf.shape),
                  _resident(bf_row.shape)],
        out_specs=[col, col, row(nh), col, row(d), col, sq, sq],
        out_shape=[jax.ShapeDtypeStruct((d, n), F32), jax.ShapeDtypeStruct((d, n), F32),
                   jax.ShapeDtypeStruct((n, nh), F32), jax.ShapeDtypeStruct((d, n), BF16),
                   jax.ShapeDtypeStruct((n, d), BF16), jax.ShapeDtypeStruct((d, n), BF16),
                   jax.ShapeDtypeStruct((nh, n), F32), jax.ShapeDtypeStruct((nh, n), F32)],
        compiler_params=_params(1),
        name="fox_proj_t",
    )(x, nw, wt_bf, wf_bf, bf_row)


BIAS_LANES_PER_HEAD = 6


def _split3(v):
    hi = v.astype(BF16).astype(F32)
    r1 = v - hi
    mid = r1.astype(BF16).astype(F32)
    lo = (r1 - mid).astype(BF16).astype(F32)
    return hi, mid, lo


def _fox_kbias_body(c_ref, aux_ref, *, cscale):
    tm = c_ref.shape[0]
    npair = aux_ref.shape[0]
    c = c_ref[...] * (-cscale)
    lane = lax.broadcasted_iota(jnp.int32, (tm, LANES), 1)
    slot = lane % BIAS_LANES_PER_HEAD
    piece = slot % 3
    used = lane < 2 * BIAS_LANES_PER_HEAD
    for p in range(npair):
        c0 = jnp.broadcast_to(c[:, 2 * p:2 * p + 1], (tm, LANES))
        c1 = jnp.broadcast_to(c[:, 2 * p + 1:2 * p + 2], (tm, LANES))
        hi, mid, lo = _split3(jnp.where(lane < BIAS_LANES_PER_HEAD, c0, c1))
        val = jnp.where(piece == 0, hi, jnp.where(piece == 1, mid, lo))
        val = jnp.where(slot < 3, val, 1.0)
        aux_ref[p] = jnp.where(used, val, 0.0).astype(BF16)


def _fox_kbias(c_rows, npair, tm, cscale):
    n, nh = c_rows.shape
    return pl.pallas_call(
        functools.partial(_fox_kbias_body, cscale=cscale),
        grid=(n // tm,),
        in_specs=[pl.BlockSpec((tm, nh), lambda i: (i, 0))],
        out_specs=pl.BlockSpec((npair, tm, LANES), lambda i: (0, i, 0)),
        out_shape=jax.ShapeDtypeStruct((npair, n, LANES), BF16),
        compiler_params=_params(1),
        name="fox_kbias",
    )(c_rows)


def _lane_cumsum_body(x_ref, y_ref, *, reverse_exclusive):
    rows, n = x_ref.shape
    nb = n // LANES
    t_i = lax.broadcasted_iota(jnp.int32, (LANES, LANES), 0)
    k_i = lax.broadcasted_iota(jnp.int32, (LANES, LANES), 1)
    sel = (t_i > k_i) if reverse_exclusive else (t_i <= k_i)
    tri = jnp.where(sel, 1.0, 0.0).astype(F32)

    def body(i, carry):
        b = (nb - 1 - i) if reverse_exclusive else i
        sl = pl.ds(pl.multiple_of(b * LANES, LANES), LANES)
        xb = x_ref[:, sl]
        y_ref[:, sl] = jnp.dot(xb, tri, precision=HIGHEST, preferred_element_type=F32) + carry
        return carry + jnp.sum(xb, axis=-1, keepdims=True)

    lax.fori_loop(0, nb, body, jnp.zeros((rows, 1), F32))


def _lane_cumsum(x, reverse_exclusive, tr):
    rows, n = x.shape
    spec = pl.BlockSpec((tr, n), lambda i: (i, 0))
    return pl.pallas_call(
        functools.partial(_lane_cumsum_body, reverse_exclusive=reverse_exclusive),
        grid=(rows // tr,),
        in_specs=[spec],
        out_specs=spec,
        out_shape=jax.ShapeDtypeStruct((rows, n), F32),
        compiler_params=_params(1),
        name="lane_cumsum",
    )(x)


ATTN_TILE = 512
ACC_PAD = 16


def _fox_attn_body(jlo_ref, c_ref, qt_ref, k_ref, ka_ref, vt_ref, kme_ref, vtm_ref,
                   qm_ref, km_ref, vm_ref, o_ref, om_ref, qx_s, m_s, acc_s,
                   *, tq, tk, dh, n_meta, cscale):
    i = pl.program_id(1)
    lane = lax.broadcasted_iota(jnp.int32, (1, 2 * dh), 1)
    head_sel = [lane < dh, lane >= dh]

    @pl.when(i == 0)
    def _():
        r_i = lax.broadcasted_iota(jnp.int32, (n_meta, n_meta), 0)
        c_i = lax.broadcasted_iota(jnp.int32, (n_meta, n_meta), 1)
        out = jnp.zeros((n_meta, 2 * dh), F32)
        for a in range(2):
            qa = jnp.where(head_sel[a], qm_ref[...], jnp.zeros_like(qm_ref[...]))
            x = _mm_nt(qa, km_ref[...]) - c_ref[a, :, 0:n_meta]
            x = jnp.where(c_i <= r_i, x, NEG)
            p = jnp.exp(x - jnp.max(x, axis=-1, keepdims=True))
            oa = _mm(p.astype(BF16), vm_ref[...]) / jnp.sum(p, axis=-1, keepdims=True)
            out = jnp.where(head_sel[a], oa, out)
        om_ref[...] = out.astype(BF16)

    q0 = pl.multiple_of(LANES + i * tq, LANES)
    row = lax.broadcasted_iota(jnp.int32, (2 * dh, tq), 0)
    qt = qt_ref[...]
    for a in range(2):
        hi, mid, lo = _split3(c_ref[a, :, pl.ds(q0, tq)] * cscale)
        slot = row - a * BIAS_LANES_PER_HEAD
        bias = jnp.where(slot == 3, hi, jnp.where(slot == 4, mid, jnp.where(slot == 5, lo, 1.0)))
        bias = jnp.where((slot >= 0) & (slot < BIAS_LANES_PER_HEAD), bias, 0.0)
        qx_s[a, 0:2 * dh, :] = jnp.where((row >= a * dh) & (row < (a + 1) * dh), qt,
                                         jnp.zeros_like(qt))
        qx_s[a, 2 * dh:4 * dh, :] = bias.astype(BF16)
        m_s[a] = jnp.full((1, tq), NEG, F32)
        acc_s[a] = jnp.zeros((dh + ACC_PAD, tq), F32)

    def tile(kx, vt, masked):
        tkb = kx.shape[0]
        ones = jnp.where(lax.broadcasted_iota(jnp.int32, (ACC_PAD, tkb), 0) == 0,
                         1.0, 0.0).astype(BF16)
        sts = [_mm(kx, qx_s[a]) for a in range(2)]
        for a in range(2):
            st = sts[a]
            if masked:
                k_i = lax.broadcasted_iota(jnp.int32, (tkb, tq), 0)
                q_i = lax.broadcasted_iota(jnp.int32, (tkb, tq), 1)
                st = jnp.where(k_i <= q_i, st, NEG)
            m_prev = m_s[a]
            m_new = jnp.maximum(m_prev, jnp.max(st, axis=0, keepdims=True))
            p = jnp.exp2(st - m_new).astype(BF16)
            vx = jnp.concatenate([vt[a * dh:(a + 1) * dh, :], ones], axis=0)
            acc_s[a] = jnp.exp2(m_prev - m_new) * acc_s[a] + _mm(vx, p)
            m_s[a] = m_new

    tile(kme_ref[0], vtm_ref[...], False)

    def key_block(j):
        k0 = pl.multiple_of(j * tk, tk)
        kx = jnp.concatenate([k_ref[pl.ds(k0, tk), :], ka_ref[0, pl.ds(k0, tk), :]], axis=1)
        return kx, vt_ref[:, pl.ds(k0, tk)]

    def body(j, carry):
        tile(*key_block(j), False)
        return carry

    lax.fori_loop(jlo_ref[pl.program_id(0), i], i, body, 0)
    tile(*key_block(i), True)

    halves = []
    for a in range(2):
        acc = acc_s[a]
        halves.append(acc[0:dh] / acc[dh:dh + 1])
    o_ref[...] = jnp.concatenate(halves, axis=0).T.astype(BF16)


def _fox_attn(jlo, c3, qt_b, kb, kaux, vt_b, kx_meta, vt_meta, qb_aux, kb_aux, vb_aux,
              n_meta, meta_blk, dh, tq, cscale):
    n, d = kb.shape
    npair = d // (2 * dh)
    nl = c3.shape[2]
    mspec = pl.BlockSpec((n_meta, 2 * dh), lambda p, i, _: (meta_blk, p))
    return pl.pallas_call(
        functools.partial(_fox_attn_body, tq=tq, tk=tq, dh=dh, n_meta=n_meta, cscale=cscale),
        grid_spec=pltpu.PrefetchScalarGridSpec(
            num_scalar_prefetch=1,
            grid=(npair, n // tq),
            in_specs=[pl.BlockSpec((2, 1, nl), lambda p, i, _: (p, 0, 0)),
                      pl.BlockSpec((2 * dh, tq), lambda p, i, _: (p, i)),
                      pl.BlockSpec((n, 2 * dh), lambda p, i, _: (0, p)),
                      pl.BlockSpec((1, n, LANES), lambda p, i, _: (p, 0, 0)),
                      pl.BlockSpec((2 * dh, n), lambda p, i, _: (p, 0)),
                      pl.BlockSpec((1, n_meta, 4 * dh), lambda p, i, _: (p, 0, 0)),
                      pl.BlockSpec((2 * dh, n_meta), lambda p, i, _: (p, 0)),
                      mspec, mspec, mspec],
            out_specs=[pl.BlockSpec((tq, 2 * dh), lambda p, i, _: (i, p)),
                       pl.BlockSpec((n_meta, 2 * dh), lambda p, i, _: (0, p))],
            scratch_shapes=[pltpu.VMEM((2, 4 * dh, tq), BF16), pltpu.VMEM((2, 1, tq), F32),
                            pltpu.VMEM((2, dh + ACC_PAD, tq), F32)]),
        out_shape=[jax.ShapeDtypeStruct((n, d), BF16), jax.ShapeDtypeStruct((n_meta, d), BF16)],
        compiler_params=_params(2),
        name="fox_attn",
    )(jlo, c3, qt_b, kb, kaux, vt_b, kx_meta, vt_meta, qb_aux, kb_aux, vb_aux)


PRUNE_NATS = 50.0
NORM_SLACK = 1.01


def _fox_first_key_block(c_main, qsq, ksq, tq):
    nh, n = c_main.shape
    nq = n // tq
    u = jnp.sqrt(jnp.max(qsq, axis=1) * jnp.max(ksq, axis=1)) * (NORM_SLACK / LOG2E)
    c_first = c_main[:, 0::tq]
    c_last = c_main[:, tq - 1::tq]
    bound = 2.0 * u[:, None, None] + c_first[:, :, None] - c_last[:, None, :]
    j_lt_i = jnp.arange(nq)[None, :] < jnp.arange(nq)[:, None]
    skip = (bound <= -PRUNE_NATS) & j_lt_i[None]
    jlo = jnp.sum(skip, axis=2).astype(jnp.int32)
    return jnp.min(jlo.reshape(nh // 2, 2, nq), axis=1)


def _fox_sample_body(q_ref, kn_ref, vn_ref, cn_ref, kc_ref, vc_ref, suf_ref, o_ref,
                     m_s, l_s, acc_s, *, nh, dh, s_len):
    t = pl.program_id(1)

    @pl.when(t == 0)
    def _():
        r_i = lax.broadcasted_iota(jnp.int32, (s_len, s_len), 0)
        c_i = lax.broadcasted_iota(jnp.int32, (s_len, s_len), 1)
        for h in range(nh):
            cols = slice(h * dh, (h + 1) * dh)
            x = _mm_nt(q_ref[:, cols], kn_ref[:, cols]) - cn_ref[0, h:h + 1, 0:s_len]
            x = jnp.where(c_i <= r_i, x, NEG)
            m = jnp.max(x, axis=-1, keepdims=True)
            p = jnp.exp(x - m)
            m_s[h] = m
            l_s[h] = jnp.sum(p, axis=-1, keepdims=True)
            acc_s[h] = _mm(p.astype(BF16), vn_ref[:, cols])

    heads = range(nh)
    x = [_mm(q_ref[:, h * dh:(h + 1) * dh], kc_ref[0, h].astype(BF16)) + suf_ref[0, h:h + 1, :]
         for h in heads]
    m_prev = [m_s[h] for h in heads]
    m_new = [jnp.maximum(m_prev[h], jnp.max(x[h], axis=-1, keepdims=True)) for h in heads]
    p = [jnp.exp(x[h] - m_new[h]) for h in heads]
    pv = [_mm_nt(p[h].astype(BF16), vc_ref[0, h].astype(BF16)) for h in heads]
    for h in heads:
        alpha = jnp.exp(m_prev[h] - m_new[h])
        l_s[h] = alpha * l_s[h] + jnp.sum(p[h], axis=-1, keepdims=True)
        acc_s[h] = alpha * acc_s[h] + pv[h]
        m_s[h] = m_new[h]

    @pl.when(t == pl.num_programs(1) - 1)
    def _():
        o_ref[...] = jnp.concatenate([acc_s[h] / l_s[h] for h in range(nh)], axis=-1).astype(BF16)


def _fox_sample(qb, kb, vb, cn, kc_t, vc_t, layer, suf, s_len, tk):
    _, nb, nh, dh, past = kc_t.shape
    d = nh * dh
    tok = pl.BlockSpec((s_len, d), lambda b, t: (b, 0))
    cache = pl.BlockSpec((None, 1, nh, dh, tk), lambda b, t: (layer, b, 0, 0, t))
    return pl.pallas_call(
        functools.partial(_fox_sample_body, nh=nh, dh=dh, s_len=s_len),
        grid=(nb, past // tk),
        in_specs=[tok, tok, tok, pl.BlockSpec((1, nh, LANES), lambda b, t: (b, 0, 0)),
                  cache, cache, pl.BlockSpec((1, nh, tk), lambda b, t: (b, 0, t))],
        out_specs=tok,
        out_shape=jax.ShapeDtypeStruct((nb * s_len, d), BF16),
        scratch_shapes=[pltpu.VMEM((nh, s_len, 1), F32), pltpu.VMEM((nh, s_len, 1), F32),
                        pltpu.VMEM((nh, s_len, dh), F32)],
        compiler_params=_params(2),
        name="fox_sample",
    )(qb, kb, vb, cn, kc_t, vc_t, suf)


def _largest_tile(n, cap):
    t = min(n, cap)
    while n % t:
        t //= 2
    return t


def kernel(x_prompt, x_sample, state_hgrn, cache_k, cache_v, cache_logf, meta_tokens,
           norm_pre_mix, norm_post_mix, norm_pre_ffn, norm_post_ffn,
           a_w_in, a_lb_logits, a_g_norm, a_w_out, b_w_in, b_f, b_w_out,
           ffn_w_gu, ffn_w_down):
    batch, seq, d = x_prompt.shape
    assert batch == 1, "one prompt stream per step"
    n_dec, s_len, _ = x_sample.shape
    n_meta = meta_tokens.shape[0]
    depth = norm_pre_mix.shape[0]
    _, _, a_heads, a_dk, a_dv = state_hgrn.shape
    _, _, past, b_heads, b_dh = cache_k.shape
    ns = n_dec * s_len
    assert s_len % HGRN_SUB == 0 and n_meta % HGRN_SUB == 0 and ns % n_meta == 0
    assert 2 * b_dh == LANES and a_dk == LANES and a_dv == LANES

    tm = _largest_tile(seq, 512)
    n_aux = ns + n_meta
    meta_blk = ns // n_meta
    rec_r = _largest_tile(seq, 512)
    rec_c = _largest_tile(rec_r, 64)
    tq = _largest_tile(seq, ATTN_TILE)
    tk_cache = _largest_tile(past, 1024)
    npair = b_heads // 2

    sm = jax.nn.softmax(a_lb_logits.astype(F32), axis=0)
    lb_all = jnp.cumsum(sm, axis=0) - sm[0]

    xm = x_prompt.reshape(seq, d)
    xa = jnp.concatenate([x_sample.reshape(ns, d), meta_tokens.astype(F32)], axis=0)
    row = lambda w: w.reshape(1, -1).astype(F32)
    cache_kt = jnp.transpose(cache_k, (0, 1, 3, 4, 2))
    cache_vt = jnp.transpose(cache_v, (0, 1, 3, 4, 2))
    cache_lt = jnp.swapaxes(cache_logf.astype(F32), 2, 3)

    st_p, st_s, k_p, v_p, lf_p, k_s, v_s, lf_s = [], [], [], [], [], [], [], []
    for i in range(depth):
        j = i // 2
        if i % 2 == 0:
            lb = lb_all[j]
            lbp = jnp.stack([jnp.log(jnp.maximum(lb, LB_FLOOR)), jnp.log1p(-lb), 1.0 - lb])
            w_in = a_w_in[j].astype(BF16)
            w_out = a_w_out[j].astype(BF16)
            qm, km, gm, vm, gtm = _hgrn_proj(xm, row(norm_pre_mix[i]), w_in, lbp, tm)
            qa, ka, ga, va, gta = _hgrn_proj(xa, row(norm_pre_mix[i]), w_in, lbp, n_aux)
            o_s, s_s = _hgrn_rec(qa, ka, ga, va, state_hgrn[j].astype(F32),
                                 n_seq=n_dec, seq_len=s_len, r=s_len, c=s_len)
            o_m, s_p, o_meta = _hgrn_rec(qm, km, gm, vm,
                                         jnp.zeros((1, a_heads, a_dk, a_dv), F32),
                                         n_seq=1, seq_len=seq, r=rec_r, c=rec_c,
                                         meta=(qa, ka, ga, va, n_meta, meta_blk))
            st_p.append(s_p.astype(state_hgrn.dtype))
            st_s.append(s_s.astype(state_hgrn.dtype))
            o_a = jnp.concatenate([o_s, o_meta], axis=0)
            xm = _mix_out(o_m, gtm, xm, row(a_g_norm[j]), w_out, row(norm_post_mix[i]), tm, a_dv)
            xa = _mix_out(o_a, gta, xa, row(a_g_norm[j]), w_out, row(norm_post_mix[i]), n_aux, a_dv)
        else:
            w_t = jnp.swapaxes(b_w_in[j], 0, 1)
            wt_qkv = w_t[:3 * d].astype(BF16)
            w_in = jnp.swapaxes(wt_qkv, 0, 1)
            w_f = jnp.pad(b_w_in[j][:, 3 * d:], ((0, 0), (0, LANES - b_heads))).astype(BF16)
            bf_row = jnp.pad(b_f[j].astype(F32), (0, LANES - b_heads)).reshape(1, LANES)
            w_out = b_w_out[j].astype(BF16)
            scale = float(b_dh) ** -0.5
            kt_m, vt_m, lf_m, qtb_m, kb_m, vtb_m, qsq, ksq = _fox_proj_t(
                xm, row(norm_pre_mix[i]), wt_qkv, w_f, bf_row, b_heads, tm, scale * LOG2E)
            kf_a, vf_a, lf_a, qb_a, kb_a, vb_a = _fox_proj(
                xa, row(norm_pre_mix[i]), w_in, w_f, bf_row, b_heads, n_aux)
            lf_t = jnp.concatenate(
                [jnp.pad(lf_a[ns:].T, ((0, 0), (0, LANES - n_meta))), lf_m.T], axis=1)
            c_all = _lane_cumsum(lf_t, False, b_heads)
            kaux = _fox_kbias(c_all[:, LANES:].T, npair, tm, LOG2E)
            kaux_meta = _fox_kbias(c_all[:, :n_meta].T, npair, n_meta, LOG2E)
            k_meta = jnp.swapaxes(kb_a[ns:].reshape(n_meta, npair, LANES), 0, 1)
            jlo = _fox_first_key_block(c_all[:, LANES:], qsq, ksq, tq)
            o_m, o_meta = _fox_attn(jlo, c_all.reshape(b_heads, 1, -1), qtb_m, kb_m, kaux, vtb_m,
                                    jnp.concatenate([k_meta, kaux_meta], axis=-1), vb_a[ns:].T,
                                    qb_a, kb_a, vb_a, n_meta, meta_blk, b_dh, tq, LOG2E)
            cl_t = cache_lt[j].reshape(n_dec * b_heads, past)
            tr = _largest_tile(n_dec * b_heads, LANES)
            suf = _lane_cumsum(cl_t, True, tr).reshape(n_dec, b_heads, past)
            ln_t = jnp.swapaxes(lf_a[:ns].reshape(n_dec, s_len, b_heads), 1, 2)
            ln_t = jnp.pad(ln_t, ((0, 0), (0, 0), (0, LANES - s_len)))
            cn = _lane_cumsum(ln_t.reshape(n_dec * b_heads, LANES), False, tr)
            o_s = _fox_sample(qb_a, kb_a, vb_a, cn.reshape(n_dec, b_heads, LANES),
                              cache_kt, cache_vt, j, suf, s_len, tk_cache)
            o_a = jnp.concatenate([o_s, o_meta], axis=0)
            xm = _mix_out(o_m, None, xm, None, w_out, row(norm_post_mix[i]), tm, 0)
            xa = _mix_out(o_a, None, xa, None, w_out, row(norm_post_mix[i]), n_aux, 0)
            kv_out = lambda t_main, f_aux: jnp.transpose(
                jnp.concatenate([f_aux[ns:].T, t_main], axis=1).reshape(b_heads, b_dh, n_meta + seq),
                (2, 0, 1))[None]
            shp = lambda t, n: t.reshape(-1, n, b_heads, b_dh)
            k_p.append(kv_out(kt_m, kf_a))
            v_p.append(kv_out(vt_m, vf_a))
            lf_p.append(jnp.concatenate([lf_a[ns:], lf_m], axis=0).reshape(1, n_meta + seq, b_heads))
            k_s.append(shp(kf_a[:ns], s_len))
            v_s.append(shp(vf_a[:ns], s_len))
            lf_s.append(lf_a[:ns].reshape(n_dec, s_len, b_heads))
        wgu = ffn_w_gu[i].astype(BF16)
        wd = ffn_w_down[i].astype(BF16)
        xm = _ffn(xm, row(norm_pre_ffn[i]), wgu, wd, row(norm_post_ffn[i]), tm)
        xa = _ffn(xa, row(norm_pre_ffn[i]), wgu, wd, row(norm_post_ffn[i]), n_aux)

    return (xm.reshape(1, seq, d), xa[:ns].reshape(n_dec, s_len, d),
            jnp.stack(st_p), jnp.stack(k_p), jnp.stack(v_p), jnp.stack(lf_p),
            jnp.stack(st_s), jnp.stack(k_s), jnp.stack(v_s), jnp.stack(lf_s))
```

```python
import functools

import jax
import jax.numpy as jnp
from jax import lax
from jax.experimental import pallas as pl
from jax.experimental.pallas import tpu as pltpu

F32 = jnp.float32
BF16 = jnp.bfloat16
EPS = 1e-6
NEG = -1e30
LB_FLOOR = 1e-30

LANES = 128
VMEM_LIMIT = 56 * 1024 * 1024
HGRN_SUB = 16
HIGHEST = lax.Precision.HIGHEST
LOG2E = 1.4426950408889634


def _params(n_grid):
    return pltpu.CompilerParams(dimension_semantics=("arbitrary",) * n_grid,
                                vmem_limit_bytes=VMEM_LIMIT)


def _resident(shape):
    nd = len(shape)
    return pl.BlockSpec(shape, lambda *_: (0,) * nd, pipeline_mode=pl.Buffered(1))


def _rmsnorm(x, w):
    return x * lax.rsqrt(jnp.mean(x * x, axis=-1, keepdims=True) + EPS) * w


def _sigmoid(x):
    return 1.0 / (1.0 + jnp.exp(-x))


def _log1p_exp(z):
    return jnp.log(1.0 + jnp.exp(z))


def _log_sigmoid(x):
    return jnp.minimum(x, 0.0) - _log1p_exp(-jnp.abs(x))


def _mm(a, b):
    return jnp.dot(a, b, preferred_element_type=F32)


def _mm_nt(a, b):
    return lax.dot_general(a, b, (((1,), (1,)), ((), ())), preferred_element_type=F32)


def _mm_tn(a, b):
    return lax.dot_general(a, b, (((0,), (0,)), ((), ())), preferred_element_type=F32)


def _hgrn_proj_body(x_ref, nw_ref, w_ref, lb_ref, q_ref, k_ref, g_ref, v_ref, gate_ref):
    f = q_ref.shape[1]
    d = v_ref.shape[1]
    xn = _rmsnorm(x_ref[...], nw_ref[...]).astype(BF16)
    q = _mm(xn, w_ref[:, 0:f])
    q_ref[...] = q * _sigmoid(q)
    fl = _mm(xn, w_ref[:, f:2 * f])
    e = jnp.exp(-jnp.abs(fl))
    r = 1.0 / (1.0 + e)
    a = lb_ref[0:1, :]
    b = lb_ref[1:2, :] + (jnp.minimum(fl, 0.0) - jnp.log(1.0 + e))
    g_ref[...] = jnp.maximum(a, b) + _log1p_exp(-jnp.abs(a - b))
    k_ref[...] = lb_ref[2:3, :] * jnp.where(fl >= 0.0, e * r, r)
    v_ref[...] = _mm(xn, w_ref[:, 2 * f:2 * f + d])
    gt = _mm(xn, w_ref[:, 2 * f + d:])
    gate_ref[...] = gt * _sigmoid(gt)


def _hgrn_proj(x, nw, w_bf, lbp, tm):
    n, d = x.shape
    f = lbp.shape[1]
    row = lambda w: pl.BlockSpec((tm, w), lambda i: (i, 0))
    return pl.pallas_call(
        _hgrn_proj_body,
        grid=(n // tm,),
        in_specs=[row(d), _resident((1, d)), _resident(w_bf.shape), _resident(lbp.shape)],
        out_specs=[row(f), row(f), row(f), row(d), row(d)],
        out_shape=[jax.ShapeDtypeStruct((n, w), F32) for w in (f, f, f, d, d)],
        compiler_params=_params(1),
        name="hgrn_proj",
    )(x, nw, w_bf, lbp)


def _hgrn_chunk(q, k, g, v, sts, c):
    nh = len(sts)
    sub, half = HGRN_SUB, HGRN_SUB // 2
    hs = lambda x, h: x[:, h * LANES:(h + 1) * LANES]
    r_i = lax.broadcasted_iota(jnp.int32, (c, c), 0)
    c_i = lax.broadcasted_iota(jnp.int32, (c, c), 1)
    tril = jnp.where(r_i >= c_i, 1.0, 0.0).astype(F32)
    gc = jnp.dot(tril, g, precision=HIGHEST, preferred_element_type=F32) * LOG2E
    g_last = gc[c - 1:c, :]
    qe = (q * jnp.exp2(gc)).astype(BF16)
    o_inter = [_mm_nt(hs(qe, h), sts[h].astype(BF16)) for h in range(nh)]
    kd = (k * jnp.exp2(g_last - gc)).astype(BF16)
    vb = v.astype(BF16)
    st_dec = jnp.exp2(g_last)
    st_new = [sts[h] * hs(st_dec, h) + _mm_tn(hs(vb, h), hs(kd, h)) for h in range(nh)]

    half_row = lax.broadcasted_iota(jnp.int32, (half, 1), 0)
    out_rows = []
    for i in range(c // sub):
        r0 = i * sub
        gi, qi, ki, vi = gc[r0:r0 + sub], q[r0:r0 + sub], k[r0:r0 + sub], v[r0:r0 + sub]
        o_lo = [o_inter[h][r0:r0 + half] for h in range(nh)]
        o_hi = [o_inter[h][r0 + half:r0 + sub] for h in range(nh)]
        if i > 0:
            g_first = gc[r0:r0 + 1]
            qs = (qi * jnp.exp2(gi - g_first)).astype(BF16)
            ks = (k[:r0] * jnp.exp2(g_first - gc[:r0])).astype(BF16)
            att = [_mm_nt(hs(qs, h), hs(ks, h)).astype(BF16) for h in range(nh)]
            off = [_mm(att[h], hs(vb[:r0], h)) for h in range(nh)]
            o_lo = [o_lo[h] + off[h][:half] for h in range(nh)]
            o_hi = [o_hi[h] + off[h][half:] for h in range(nh)]
        for rows, acc, s_range in ((slice(0, half), o_lo, range(half)),
                                   (slice(half, sub), o_hi, range(sub))):
            g_rows, q_rows = gi[rows], qi[rows]
            for s in s_range:
                diff = g_rows - gi[s:s + 1]
                if s >= rows.start:
                    diff = jnp.where(half_row >= s - rows.start, diff, NEG)
                prod = q_rows * jnp.exp2(diff) * ki[s:s + 1]
                for h in range(nh):
                    col = jnp.sum(hs(prod, h), axis=-1, keepdims=True)
                    acc[h] = acc[h] + col * hs(vi, h)[s:s + 1]
        out_rows.append(jnp.concatenate(
            [jnp.concatenate([o_lo[h], o_hi[h]], axis=0) for h in range(nh)], axis=1))
    o = jnp.concatenate(out_rows, axis=0) if len(out_rows) > 1 else out_rows[0]
    return o, st_new


def _hgrn_rec_body(*refs, c, r, has_meta):
    if has_meta:
        (q_ref, k_ref, g_ref, v_ref, s0_ref, qm_ref, km_ref, gm_ref, vm_ref,
         o_ref, s_ref, om_ref, st_ref) = refs
    else:
        q_ref, k_ref, g_ref, v_ref, s0_ref, o_ref, s_ref, st_ref = refs
    blk = pl.program_id(1)
    nh = st_ref.shape[0]

    @pl.when(blk == 0)
    def _():
        sts = [s0_ref[0, h].T for h in range(nh)]
        if has_meta:
            o_m, sts = _hgrn_chunk(qm_ref[...], km_ref[...], gm_ref[...], vm_ref[...],
                                   sts, qm_ref.shape[0])
            om_ref[...] = o_m
        for h in range(nh):
            st_ref[h] = sts[h]

    def body(j, carry):
        sl = pl.ds(pl.multiple_of(j * c, c), c)
        o, sts = _hgrn_chunk(q_ref[sl, :], k_ref[sl, :], g_ref[sl, :], v_ref[sl, :],
                             [st_ref[h] for h in range(nh)], c)
        o_ref[sl, :] = o
        for h in range(nh):
            st_ref[h] = sts[h]
        return carry

    lax.fori_loop(0, r // c, body, 0)

    @pl.when(blk == pl.num_programs(1) - 1)
    def _():
        for h in range(nh):
            s_ref[0, h] = st_ref[h].T


def _hgrn_rec(q, k, g, v, s0, *, n_seq, seq_len, r, c, meta=None):
    _, h, dk, dv = s0.shape
    nblk = seq_len // r
    tok = lambda w: pl.BlockSpec((r, h * w), lambda b, i: (b * nblk + i, 0))
    st_spec = pl.BlockSpec((1, h, dk, dv), lambda b, i: (b, 0, 0, 0))
    in_specs = [tok(dk), tok(dk), tok(dk), tok(dv), st_spec]
    args = [q, k, g, v, s0]
    out_specs = [tok(dv), st_spec]
    out_shape = [jax.ShapeDtypeStruct((n_seq * seq_len, h * dv), F32),
                 jax.ShapeDtypeStruct(s0.shape, F32)]
    if meta is not None:
        qm, km, gm, vm, n_meta, meta_blk = meta
        mspec = lambda w: pl.BlockSpec((n_meta, h * w), lambda b, i: (meta_blk, 0))
        in_specs += [mspec(dk), mspec(dk), mspec(dk), mspec(dv)]
        args += [qm, km, gm, vm]
        out_specs.append(pl.BlockSpec((n_meta, h * dv), lambda b, i: (0, 0)))
        out_shape.append(jax.ShapeDtypeStruct((n_meta, h * dv), F32))
    return pl.pallas_call(
        functools.partial(_hgrn_rec_body, c=c, r=r, has_meta=meta is not None),
        grid=(n_seq, nblk),
        in_specs=in_specs,
        out_specs=out_specs,
        out_shape=out_shape,
        scratch_shapes=[pltpu.VMEM((h, dv, dk), F32)],
        compiler_params=_params(2),
        name="hgrn_rec",
    )(*args)


def _mix_out_body(*refs, head_dim):
    if head_dim:
        o_ref, gate_ref, x_ref, gn_ref, w_ref, pw_ref, y_ref = refs
        o = o_ref[...]
        parts = []
        for h in range(o.shape[1] // head_dim):
            oh = o[:, h * head_dim:(h + 1) * head_dim]
            parts.append(oh * lax.rsqrt(jnp.mean(oh * oh, axis=-1, keepdims=True) + EPS))
        o = (jnp.concatenate(parts, axis=-1) * gn_ref[...] * gate_ref[...]).astype(BF16)
    else:
        o_ref, x_ref, w_ref, pw_ref, y_ref = refs
        o = o_ref[...]
    m = _mm(o, w_ref[...])
    y_ref[...] = x_ref[...] + _rmsnorm(m, pw_ref[...])


def _mix_out(o, gate, x, gn, w_bf, pw, tm, head_dim):
    n, d = x.shape
    row = pl.BlockSpec((tm, d), lambda i: (i, 0))
    if head_dim:
        in_specs = [row, row, row, _resident((1, d)), _resident(w_bf.shape), _resident((1, d))]
        args = (o, gate, x, gn, w_bf, pw)
    else:
        in_specs = [row, row, _resident(w_bf.shape), _resident((1, d))]
        args = (o, x, w_bf, pw)
    return pl.pallas_call(
        functools.partial(_mix_out_body, head_dim=head_dim),
        grid=(n // tm,),
        in_specs=in_specs,
        out_specs=row,
        out_shape=jax.ShapeDtypeStruct((n, d), F32),
        compiler_params=_params(1),
        name="mix_out",
    )(*args)


def _ffn_body(x_ref, nw_ref, wgu_ref, wd_ref, pw_ref, y_ref, *, dff, fc):
    x = x_ref[...]
    xn = _rmsnorm(x, nw_ref[...]).astype(BF16)
    acc = jnp.zeros(x.shape, F32)
    for c0 in range(0, dff, fc):
        a = _mm(xn, wgu_ref[:, c0:c0 + fc])
        u = _mm(xn, wgu_ref[:, dff + c0:dff + c0 + fc])
        hid = (a * _sigmoid(a) * u).astype(BF16)
        acc = acc + _mm(hid, wd_ref[c0:c0 + fc, :])
    y_ref[...] = x + _rmsnorm(acc, pw_ref[...])


def _ffn(x, nw, wgu_bf, wd_bf, pw, tm):
    n, d = x.shape
    dff = wd_bf.shape[0]
    fc = dff // 2 if (dff // 2) % LANES == 0 else dff
    row = pl.BlockSpec((tm, d), lambda i: (i, 0))
    return pl.pallas_call(
        functools.partial(_ffn_body, dff=dff, fc=fc),
        grid=(n // tm,),
        in_specs=[row, _resident((1, d)), _resident(wgu_bf.shape), _resident(wd_bf.shape),
                  _resident((1, d))],
        out_specs=row,
        out_shape=jax.ShapeDtypeStruct((n, d), F32),
        compiler_params=_params(1),
        name="ffn",
    )(x, nw, wgu_bf, wd_bf, pw)


def _fox_proj_body(x_ref, nw_ref, w_ref, wf_ref, bf_ref, k_ref, v_ref, lf_ref,
                   qb_ref, kb_ref, vb_ref, *, scale):
    d = k_ref.shape[1]
    nh = lf_ref.shape[1]
    xn = _rmsnorm(x_ref[...], nw_ref[...]).astype(BF16)
    qb_ref[...] = (_mm(xn, w_ref[:, 0:d]) * scale).astype(BF16)
    k = _mm(xn, w_ref[:, d:2 * d])
    k_ref[...] = k
    kb_ref[...] = k.astype(BF16)
    v = _mm(xn, w_ref[:, 2 * d:3 * d])
    v_ref[...] = v
    vb_ref[...] = v.astype(BF16)
    lf = _log_sigmoid(_mm(xn, wf_ref[...]) + bf_ref[...])
    lf_ref[...] = lf[:, 0:nh]


def _fox_proj(x, nw, w_bf, wf_bf, bf_row, nh, tm):
    n, d = x.shape
    row = lambda w: pl.BlockSpec((tm, w), lambda i: (i, 0))
    scale = float(d // nh) ** -0.5
    return pl.pallas_call(
        functools.partial(_fox_proj_body, scale=scale),
        grid=(n // tm,),
        in_specs=[row(d), _resident((1, d)), _resident(w_bf.shape), _resident(wf_bf.shape),
                  _resident(bf_row.shape)],
        out_specs=[row(d), row(d), row(nh), row(d), row(d), row(d)],
        out_shape=[jax.ShapeDtypeStruct((n, d), F32), jax.ShapeDtypeStruct((n, d), F32),
                   jax.ShapeDtypeStruct((n, nh), F32), jax.ShapeDtypeStruct((n, d), BF16),
                   jax.ShapeDtypeStruct((n, d), BF16), jax.ShapeDtypeStruct((n, d), BF16)],
        compiler_params=_params(1),
        name="fox_proj",
    )(x, nw, w_bf, wf_bf, bf_row)


def _head_sq_norms(xt, nh):
    dh = xt.shape[0] // nh
    x2 = xt * xt
    return jnp.concatenate([jnp.sum(x2[h * dh:(h + 1) * dh], axis=0, keepdims=True)
                            for h in range(nh)], axis=0)


def _fox_proj_t_body(x_ref, nw_ref, wt_ref, wf_ref, bf_ref, kt_ref, vt_ref, lf_ref,
                     qtb_ref, kb_ref, vtb_ref, qsq_ref, ksq_ref, *, qscale):
    d = kt_ref.shape[0]
    nh = lf_ref.shape[1]
    xn = _rmsnorm(x_ref[...], nw_ref[...]).astype(BF16)
    qtb = (_mm_nt(wt_ref[0:d, :], xn) * qscale).astype(BF16)
    qtb_ref[...] = qtb
    qsq_ref[...] = _head_sq_norms(qtb.astype(F32), nh)
    kt = _mm_nt(wt_ref[d:2 * d, :], xn)
    kt_ref[...] = kt
    ksq_ref[...] = _head_sq_norms(kt.astype(BF16).astype(F32), nh)
    kb_ref[...] = _mm_nt(xn, wt_ref[d:2 * d, :]).astype(BF16)
    vt = _mm_nt(wt_ref[2 * d:3 * d, :], xn)
    vt_ref[...] = vt
    vtb_ref[...] = vt.astype(BF16)
    lf = _log_sigmoid(_mm(xn, wf_ref[...]) + bf_ref[...])
    lf_ref[...] = lf[:, 0:nh]


def _fox_proj_t(x, nw, wt_bf, wf_bf, bf_row, nh, tm, qscale):
    n, d = x.shape
    row = lambda w: pl.BlockSpec((tm, w), lambda i: (i, 0))
    col = pl.BlockSpec((d, tm), lambda i: (0, i))
    sq = pl.BlockSpec((nh, tm), lambda i: (0, i))
    return pl.pallas_call(
        functools.partial(_fox_proj_t_body, qscale=qscale),
        grid=(n // tm,),
        in_specs=[row(d), _resident((1, d)), _resident(wt_bf.shape), _resident(wf_bf.shape),
                  _resident(bf_row.shape)],
        out_specs=[col, col, row(nh), col, row(d), col, sq, sq],
        out_shape=[jax.ShapeDtypeStruct((d, n), F32), jax.ShapeDtypeStruct((d, n), F32),
                   jax.ShapeDtypeStruct((n, nh), F32), jax.ShapeDtypeStruct((d, n), BF16),
                   jax.ShapeDtypeStruct((n, d), BF16), jax.ShapeDtypeStruct((d, n), BF16),
                   jax.ShapeDtypeStruct((nh, n), F32), jax.ShapeDtypeStruct((nh, n), F32)],
        compiler_params=_params(1),
        name="fox_proj_t",
    )(x, nw, wt_bf, wf_bf, bf_row)


BIAS_LANES_PER_HEAD = 6


def _split3(v):
    hi = v.astype(BF16).astype(F32)
    r1 = v - hi
    mid = r1.astype(BF16).astype(F32)
    lo = (r1 - mid).astype(BF16).astype(F32)
    return hi, mid, lo


def _fox_kbias_body(c_ref, aux_ref, *, cscale):
    tm = c_ref.shape[0]
    npair = aux_ref.shape[0]
    c = c_ref[...] * (-cscale)
    lane = lax.broadcasted_iota(jnp.int32, (tm, LANES), 1)
    slot = lane % BIAS_LANES_PER_HEAD
    piece = slot % 3
    used = lane < 2 * BIAS_LANES_PER_HEAD
    for p in range(npair):
        c0 = jnp.broadcast_to(c[:, 2 * p:2 * p + 1], (tm, LANES))
        c1 = jnp.broadcast_to(c[:, 2 * p + 1:2 * p + 2], (tm, LANES))
        hi, mid, lo = _split3(jnp.where(lane < BIAS_LANES_PER_HEAD, c0, c1))
        val = jnp.where(piece == 0, hi, jnp.where(piece == 1, mid, lo))
        val = jnp.where(slot < 3, val, 1.0)
        aux_ref[p] = jnp.where(used, val, 0.0).astype(BF16)


def _fox_kbias(c_rows, npair, tm, cscale):
    n, nh = c_rows.shape
    return pl.pallas_call(
        functools.partial(_fox_kbias_body, cscale=cscale),
        grid=(n // tm,),
        in_specs=[pl.BlockSpec((tm, nh), lambda i: (i, 0))],
        out_specs=pl.BlockSpec((npair, tm, LANES), lambda i: (0, i, 0)),
        out_shape=jax.ShapeDtypeStruct((npair, n, LANES), BF16),
        compiler_params=_params(1),
        name="fox_kbias",
    )(c_rows)


def _lane_cumsum_body(x_ref, y_ref, *, reverse_exclusive):
    rows, n = x_ref.shape
    nb = n // LANES
    t_i = lax.broadcasted_iota(jnp.int32, (LANES, LANES), 0)
    k_i = lax.broadcasted_iota(jnp.int32, (LANES, LANES), 1)
    sel = (t_i > k_i) if reverse_exclusive else (t_i <= k_i)
    tri = jnp.where(sel, 1.0, 0.0).astype(F32)

    def body(i, carry):
        b = (nb - 1 - i) if reverse_exclusive else i
        sl = pl.ds(pl.multiple_of(b * LANES, LANES), LANES)
        xb = x_ref[:, sl]
        y_ref[:, sl] = jnp.dot(xb, tri, precision=HIGHEST, preferred_element_type=F32) + carry
        return carry + jnp.sum(xb, axis=-1, keepdims=True)

    lax.fori_loop(0, nb, body, jnp.zeros((rows, 1), F32))


def _lane_cumsum(x, reverse_exclusive, tr):
    rows, n = x.shape
    spec = pl.BlockSpec((tr, n), lambda i: (i, 0))
    return pl.pallas_call(
        functools.partial(_lane_cumsum_body, reverse_exclusive=reverse_exclusive),
        grid=(rows // tr,),
        in_specs=[spec],
        out_specs=spec,
        out_shape=jax.ShapeDtypeStruct((rows, n), F32),
        compiler_params=_params(1),
        name="lane_cumsum",
    )(x)


ATTN_TILE = 512
ACC_PAD = 16


def _fox_attn_body(jlo_ref, c_ref, qt_ref, k_ref, ka_ref, vt_ref, kme_ref, vtm_ref,
                   qm_ref, km_ref, vm_ref, o_ref, om_ref, qx_s, m_s, acc_s,
                   *, tq, tk, dh, n_meta, cscale):
    i = pl.program_id(1)
    lane = lax.broadcasted_iota(jnp.int32, (1, 2 * dh), 1)
    head_sel = [lane < dh, lane >= dh]

    @pl.when(i == 0)
    def _():
        r_i = lax.broadcasted_iota(jnp.int32, (n_meta, n_meta), 0)
        c_i = lax.broadcasted_iota(jnp.int32, (n_meta, n_meta), 1)
        out = jnp.zeros((n_meta, 2 * dh), F32)
        for a in range(2):
            qa = jnp.where(head_sel[a], qm_ref[...], jnp.zeros_like(qm_ref[...]))
            x = _mm_nt(qa, km_ref[...]) - c_ref[a, :, 0:n_meta]
            x = jnp.where(c_i <= r_i, x, NEG)
            p = jnp.exp(x - jnp.max(x, axis=-1, keepdims=True))
            oa = _mm(p.astype(BF16), vm_ref[...]) / jnp.sum(p, axis=-1, keepdims=True)
            out = jnp.where(head_sel[a], oa, out)
        om_ref[...] = out.astype(BF16)

    q0 = pl.multiple_of(LANES + i * tq, LANES)
    row = lax.broadcasted_iota(jnp.int32, (2 * dh, tq), 0)
    qt = qt_ref[...]
    for a in range(2):
        hi, mid, lo = _split3(c_ref[a, :, pl.ds(q0, tq)] * cscale)
        slot = row - a * BIAS_LANES_PER_HEAD
        bias = jnp.where(slot == 3, hi, jnp.where(slot == 4, mid, jnp.where(slot == 5, lo, 1.0)))
        bias = jnp.where((slot >= 0) & (slot < BIAS_LANES_PER_HEAD), bias, 0.0)
        qx_s[a, 0:2 * dh, :] = jnp.where((row >= a * dh) & (row < (a + 1) * dh), qt,
                                         jnp.zeros_like(qt))
        qx_s[a, 2 * dh:4 * dh, :] = bias.astype(BF16)
        m_s[a] = jnp.full((1, tq), NEG, F32)
        acc_s[a] = jnp.zeros((dh + ACC_PAD, tq), F32)

    def tile(kx, vt, masked):
        tkb = kx.shape[0]
        ones = jnp.where(lax.broadcasted_iota(jnp.int32, (ACC_PAD, tkb), 0) == 0,
                         1.0, 0.0).astype(BF16)
        sts = [_mm(kx, qx_s[a]) for a in range(2)]
        for a in range(2):
            st = sts[a]
            if masked:
                k_i = lax.broadcasted_iota(jnp.int32, (tkb, tq), 0)
                q_i = lax.broadcasted_iota(jnp.int32, (tkb, tq), 1)
                st = jnp.where(k_i <= q_i, st, NEG)
            m_prev = m_s[a]
            m_new = jnp.maximum(m_prev, jnp.max(st, axis=0, keepdims=True))
            p = jnp.exp2(st - m_new).astype(BF16)
            vx = jnp.concatenate([vt[a * dh:(a + 1) * dh, :], ones], axis=0)
            acc_s[a] = jnp.exp2(m_prev - m_new) * acc_s[a] + _mm(vx, p)
            m_s[a] = m_new

    tile(kme_ref[0], vtm_ref[...], False)

    def key_block(j):
        k0 = pl.multiple_of(j * tk, tk)
        kx = jnp.concatenate([k_ref[pl.ds(k0, tk), :], ka_ref[0, pl.ds(k0, tk), :]], axis=1)
        return kx, vt_ref[:, pl.ds(k0, tk)]

    def body(j, carry):
        tile(*key_block(j), False)
        return carry

    lax.fori_loop(jlo_ref[pl.program_id(0), i], i, body, 0)
    tile(*key_block(i), True)

    halves = []
    for a in range(2):
        acc = acc_s[a]
        halves.append(acc[0:dh] / acc[dh:dh + 1])
    o_ref[...] = jnp.concatenate(halves, axis=0).T.astype(BF16)


def _fox_attn(jlo, c3, qt_b, kb, kaux, vt_b, kx_meta, vt_meta, qb_aux, kb_aux, vb_aux,
              n_meta, meta_blk, dh, tq, cscale):
    n, d = kb.shape
    npair = d // (2 * dh)
    nl = c3.shape[2]
    mspec = pl.BlockSpec((n_meta, 2 * dh), lambda p, i, _: (meta_blk, p))
    return pl.pallas_call(
        functools.partial(_fox_attn_body, tq=tq, tk=tq, dh=dh, n_meta=n_meta, cscale=cscale),
        grid_spec=pltpu.PrefetchScalarGridSpec(
            num_scalar_prefetch=1,
            grid=(npair, n // tq),
            in_specs=[pl.BlockSpec((2, 1, nl), lambda p, i, _: (p, 0, 0)),
                      pl.BlockSpec((2 * dh, tq), lambda p, i, _: (p, i)),
                      pl.BlockSpec((n, 2 * dh), lambda p, i, _: (0, p)),
                      pl.BlockSpec((1, n, LANES), lambda p, i, _: (p, 0, 0)),
                      pl.BlockSpec((2 * dh, n), lambda p, i, _: (p, 0)),
                      pl.BlockSpec((1, n_meta, 4 * dh), lambda p, i, _: (p, 0, 0)),
                      pl.BlockSpec((2 * dh, n_meta), lambda p, i, _: (p, 0)),
                      mspec, mspec, mspec],
            out_specs=[pl.BlockSpec((tq, 2 * dh), lambda p, i, _: (i, p)),
                       pl.BlockSpec((n_meta, 2 * dh), lambda p, i, _: (0, p))],
            scratch_shapes=[pltpu.VMEM((2, 4 * dh, tq), BF16), pltpu.VMEM((2, 1, tq), F32),
                            pltpu.VMEM((2, dh + ACC_PAD, tq), F32)]),
        out_shape=[jax.ShapeDtypeStruct((n, d), BF16), jax.ShapeDtypeStruct((n_meta, d), BF16)],
        compiler_params=_params(2),
        name="fox_attn",
    )(jlo, c3, qt_b, kb, kaux, vt_b, kx_meta, vt_meta, qb_aux, kb_aux, vb_aux)


PRUNE_NATS = 50.0
NORM_SLACK = 1.01


def _fox_first_key_block(c_main, qsq, ksq, tq):
    nh, n = c_main.shape
    nq = n // tq
    u = jnp.sqrt(jnp.max(qsq, axis=1) * jnp.max(ksq, axis=1)) * (NORM_SLACK / LOG2E)
    c_first = c_main[:, 0::tq]
    c_last = c_main[:, tq - 1::tq]
    bound = 2.0 * u[:, None, None] + c_first[:, :, None] - c_last[:, None, :]
    j_lt_i = jnp.arange(nq)[None, :] < jnp.arange(nq)[:, None]
    skip = (bound <= -PRUNE_NATS) & j_lt_i[None]
    jlo = jnp.sum(skip, axis=2).astype(jnp.int32)
    return jnp.min(jlo.reshape(nh // 2, 2, nq), axis=1)


def _fox_sample_body(q_ref, kn_ref, vn_ref, cn_ref, kc_ref, vc_ref, suf_ref, o_ref,
                     m_s, l_s, acc_s, *, nh, dh, s_len):
    t = pl.program_id(1)

    @pl.when(t == 0)
    def _():
        r_i = lax.broadcasted_iota(jnp.int32, (s_len, s_len), 0)
        c_i = lax.broadcasted_iota(jnp.int32, (s_len, s_len), 1)
        cols = [slice(h * dh, (h + 1) * dh) for h in range(nh)]
        x = [_mm_nt(q_ref[:, cols[h]], kn_ref[:, cols[h]]) - cn_ref[0, h:h + 1, 0:s_len]
             for h in range(nh)]
        x = [jnp.where(c_i <= r_i, x[h], NEG) for h in range(nh)]
        m = [jnp.max(x[h], axis=-1, keepdims=True) for h in range(nh)]
        p = [jnp.exp(x[h] - m[h]) for h in range(nh)]
        pv = [_mm(p[h].astype(BF16), vn_ref[:, cols[h]]) for h in range(nh)]
        for h in range(nh):
            m_s[h] = m[h]
            l_s[h] = jnp.sum(p[h], axis=-1, keepdims=True)
            acc_s[h] = pv[h]

    heads = range(nh)
    x = [_mm(q_ref[:, h * dh:(h + 1) * dh], kc_ref[0, h].astype(BF16)) + suf_ref[0, h:h + 1, :]
         for h in heads]
    m_prev = [m_s[h] for h in heads]
    m_new = [jnp.maximum(m_prev[h], jnp.max(x[h], axis=-1, keepdims=True)) for h in heads]
    p = [jnp.exp(x[h] - m_new[h]) for h in heads]
    pv = [_mm_nt(p[h].astype(BF16), vc_ref[0, h].astype(BF16)) for h in heads]
    for h in heads:
        alpha = jnp.exp(m_prev[h] - m_new[h])
        l_s[h] = alpha * l_s[h] + jnp.sum(p[h], axis=-1, keepdims=True)
        acc_s[h] = alpha * acc_s[h] + pv[h]
        m_s[h] = m_new[h]

    @pl.when(t == pl.num_programs(1) - 1)
    def _():
        o_ref[...] = jnp.concatenate([acc_s[h] / l_s[h] for h in range(nh)], axis=-1).astype(BF16)


def _fox_sample(qb, kb, vb, cn, kc_t, vc_t, layer, suf, s_len, tk):
    _, nb, nh, dh, past = kc_t.shape
    d = nh * dh
    tok = pl.BlockSpec((s_len, d), lambda b, t: (b, 0))
    cache = pl.BlockSpec((None, 1, nh, dh, tk), lambda b, t: (layer, b, 0, 0, t))
    return pl.pallas_call(
        functools.partial(_fox_sample_body, nh=nh, dh=dh, s_len=s_len),
        grid=(nb, past // tk),
        in_specs=[tok, tok, tok, pl.BlockSpec((1, nh, LANES), lambda b, t: (b, 0, 0)),
                  cache, cache, pl.BlockSpec((1, nh, tk), lambda b, t: (b, 0, t))],
        out_specs=tok,
        out_shape=jax.ShapeDtypeStruct((nb * s_len, d), BF16),
        scratch_shapes=[pltpu.VMEM((nh, s_len, 1), F32), pltpu.VMEM((nh, s_len, 1), F32),
                        pltpu.VMEM((nh, s_len, dh), F32)],
        compiler_params=_params(2),
        name="fox_sample",
    )(qb, kb, vb, cn, kc_t, vc_t, suf)


def _largest_tile(n, cap):
    t = min(n, cap)
    while n % t:
        t //= 2
    return t


def kernel(x_prompt, x_sample, state_hgrn, cache_k, cache_v, cache_logf, meta_tokens,
           norm_pre_mix, norm_post_mix, norm_pre_ffn, norm_post_ffn,
           a_w_in, a_lb_logits, a_g_norm, a_w_out, b_w_in, b_f, b_w_out,
           ffn_w_gu, ffn_w_down):
    batch, seq, d = x_prompt.shape
    assert batch == 1, "one prompt stream per step"
    n_dec, s_len, _ = x_sample.shape
    n_meta = meta_tokens.shape[0]
    depth = norm_pre_mix.shape[0]
    _, _, a_heads, a_dk, a_dv = state_hgrn.shape
    _, _, past, b_heads, b_dh = cache_k.shape
    ns = n_dec * s_len
    assert s_len % HGRN_SUB == 0 and n_meta % HGRN_SUB == 0 and ns % n_meta == 0
    assert 2 * b_dh == LANES and a_dk == LANES and a_dv == LANES

    tm = _largest_tile(seq, 512)
    n_aux = ns + n_meta
    meta_blk = ns // n_meta
    rec_r = _largest_tile(seq, 512)
    rec_c = _largest_tile(rec_r, 64)
    tq = _largest_tile(seq, ATTN_TILE)
    tk_cache = _largest_tile(past, 1024)
    npair = b_heads // 2

    sm = jax.nn.softmax(a_lb_logits.astype(F32), axis=0)
    lb_all = jnp.cumsum(sm, axis=0) - sm[0]

    xm = x_prompt.reshape(seq, d)
    xa = jnp.concatenate([x_sample.reshape(ns, d), meta_tokens.astype(F32)], axis=0)
    row = lambda w: w.reshape(1, -1).astype(F32)
    cache_kt = jnp.transpose(cache_k, (0, 1, 3, 4, 2))
    cache_vt = jnp.transpose(cache_v, (0, 1, 3, 4, 2))
    cache_lt = jnp.swapaxes(cache_logf.astype(F32), 2, 3)

    st_p, st_s, k_p, v_p, lf_p, k_s, v_s, lf_s = [], [], [], [], [], [], [], []
    for i in range(depth):
        j = i // 2
        if i % 2 == 0:
            lb = lb_all[j]
            lbp = jnp.stack([jnp.log(jnp.maximum(lb, LB_FLOOR)), jnp.log1p(-lb), 1.0 - lb])
            w_in = a_w_in[j].astype(BF16)
            w_out = a_w_out[j].astype(BF16)
            qm, km, gm, vm, gtm = _hgrn_proj(xm, row(norm_pre_mix[i]), w_in, lbp, tm)
            qa, ka, ga, va, gta = _hgrn_proj(xa, row(norm_pre_mix[i]), w_in, lbp, n_aux)
            o_s, s_s = _hgrn_rec(qa, ka, ga, va, state_hgrn[j].astype(F32),
                                 n_seq=n_dec, seq_len=s_len, r=s_len, c=s_len)
            o_m, s_p, o_meta = _hgrn_rec(qm, km, gm, vm,
                                         jnp.zeros((1, a_heads, a_dk, a_dv), F32),
                                         n_seq=1, seq_len=seq, r=rec_r, c=rec_c,
                                         meta=(qa, ka, ga, va, n_meta, meta_blk))
            st_p.append(s_p.astype(state_hgrn.dtype))
            st_s.append(s_s.astype(state_hgrn.dtype))
            o_a = jnp.concatenate([o_s, o_meta], axis=0)
            xm = _mix_out(o_m, gtm, xm, row(a_g_norm[j]), w_out, row(norm_post_mix[i]), tm, a_dv)
            xa = _mix_out(o_a, gta, xa, row(a_g_norm[j]), w_out, row(norm_post_mix[i]), n_aux, a_dv)
        else:
            w_t = jnp.swapaxes(b_w_in[j], 0, 1)
            wt_qkv = w_t[:3 * d].astype(BF16)
            w_in = jnp.swapaxes(wt_qkv, 0, 1)
            w_f = jnp.pad(b_w_in[j][:, 3 * d:], ((0, 0), (0, LANES - b_heads))).astype(BF16)
            bf_row = jnp.pad(b_f[j].astype(F32), (0, LANES - b_heads)).reshape(1, LANES)
            w_out = b_w_out[j].astype(BF16)
            scale = float(b_dh) ** -0.5
            kt_m, vt_m, lf_m, qtb_m, kb_m, vtb_m, qsq, ksq = _fox_proj_t(
                xm, row(norm_pre_mix[i]), wt_qkv, w_f, bf_row, b_heads, tm, scale * LOG2E)
            kf_a, vf_a, lf_a, qb_a, kb_a, vb_a = _fox_proj(
                xa, row(norm_pre_mix[i]), w_in, w_f, bf_row, b_heads, n_aux)
            lf_t = jnp.concatenate(
                [jnp.pad(lf_a[ns:].T, ((0, 0), (0, LANES - n_meta))), lf_m.T], axis=1)
            c_all = _lane_cumsum(lf_t, False, b_heads)
            kaux = _fox_kbias(c_all[:, LANES:].T, npair, tm, LOG2E)
            kaux_meta = _fox_kbias(c_all[:, :n_meta].T, npair, n_meta, LOG2E)
            k_meta = jnp.swapaxes(kb_a[ns:].reshape(n_meta, npair, LANES), 0, 1)
            jlo = _fox_first_key_block(c_all[:, LANES:], qsq, ksq, tq)
            o_m, o_meta = _fox_attn(jlo, c_all.reshape(b_heads, 1, -1), qtb_m, kb_m, kaux, vtb_m,
                                    jnp.concatenate([k_meta, kaux_meta], axis=-1), vb_a[ns:].T,
                                    qb_a, kb_a, vb_a, n_meta, meta_blk, b_dh, tq, LOG2E)
            cl_t = cache_lt[j].reshape(n_dec * b_heads, past)
            tr = _largest_tile(n_dec * b_heads, LANES)
            suf = _lane_cumsum(cl_t, True, tr).reshape(n_dec, b_heads, past)
            ln_t = jnp.swapaxes(lf_a[:ns].reshape(n_dec, s_len, b_heads), 1, 2)
            ln_t = jnp.pad(ln_t, ((0, 0), (0, 0), (0, LANES - s_len)))
            cn = _lane_cumsum(ln_t.reshape(n_dec * b_heads, LANES), False, tr)
            o_s = _fox_sample(qb_a, kb_a, vb_a, cn.reshape(n_dec, b_heads, LANES),
                              cache_kt, cache_vt, j, suf, s_len, tk_cache)
            o_a = jnp.concatenate([o_s, o_meta], axis=0)
            xm = _mix_out(o_m, None, xm, None, w_out, row(norm_post_mix[i]), tm, 0)
            xa = _mix_out(o_a, None, xa, None, w_out, row(norm_post_mix[i]), n_aux, 0)
            shp = lambda t, n: t.reshape(-1, n, b_heads, b_dh)
            k_p.append((kf_a[ns:].T, kt_m))
            v_p.append((vf_a[ns:].T, vt_m))
            lf_p.append(jnp.concatenate([lf_a[ns:], lf_m], axis=0).reshape(1, n_meta + seq, b_heads))
            k_s.append(shp(kf_a[:ns], s_len))
            v_s.append(shp(vf_a[:ns], s_len))
            lf_s.append(lf_a[:ns].reshape(n_dec, s_len, b_heads))
        wgu = ffn_w_gu[i].astype(BF16)
        wd = ffn_w_down[i].astype(BF16)
        xm = _ffn(xm, row(norm_pre_ffn[i]), wgu, wd, row(norm_post_ffn[i]), tm)
        xa = _ffn(xa, row(norm_pre_ffn[i]), wgu, wd, row(norm_post_ffn[i]), n_aux)

    def prompt_kv(parts):
        t = jnp.concatenate([jnp.stack([m for m, _ in parts]), jnp.stack([x for _, x in parts])],
                            axis=2)
        t = t.reshape(len(parts), b_heads, b_dh, n_meta + seq)
        return jnp.transpose(t, (0, 3, 1, 2))[:, None]

    return (xm.reshape(1, seq, d), xa[:ns].reshape(n_dec, s_len, d),
            jnp.stack(st_p), prompt_kv(k_p), prompt_kv(v_p), jnp.stack(lf_p),
            jnp.stack(st_s), jnp.stack(k_s), jnp.stack(v_s), jnp.stack(lf_s))
```

```python
import functools

import jax
import jax.numpy as jnp
from jax import lax
from jax.experimental import pallas as pl
from jax.experimental.pallas import tpu as pltpu

F32 = jnp.float32
BF16 = jnp.bfloat16
EPS = 1e-6
NEG = -1e30
LB_FLOOR = 1e-30

LANES = 128
VMEM_LIMIT = 56 * 1024 * 1024
HGRN_SUB = 16
HIGHEST = lax.Precision.HIGHEST
LOG2E = 1.4426950408889634


def _params(n_grid):
    return pltpu.CompilerParams(dimension_semantics=("arbitrary",) * n_grid,
                                vmem_limit_bytes=VMEM_LIMIT)


def _resident(shape):
    nd = len(shape)
    return pl.BlockSpec(shape, lambda *_: (0,) * nd, pipeline_mode=pl.Buffered(1))


def _rmsnorm(x, w):
    return x * lax.rsqrt(jnp.mean(x * x, axis=-1, keepdims=True) + EPS) * w


def _sigmoid(x):
    return 1.0 / (1.0 + jnp.exp(-x))


def _log1p_exp(z):
    return jnp.log(1.0 + jnp.exp(z))


def _log_sigmoid(x):
    return jnp.minimum(x, 0.0) - _log1p_exp(-jnp.abs(x))


def _mm(a, b):
    return jnp.dot(a, b, preferred_element_type=F32)


def _mm_nt(a, b):
    return lax.dot_general(a, b, (((1,), (1,)), ((), ())), preferred_element_type=F32)


def _mm_tn(a, b):
    return lax.dot_general(a, b, (((0,), (0,)), ((), ())), preferred_element_type=F32)


def _hgrn_proj_body(x_ref, nw_ref, w_ref, lb_ref, q_ref, k_ref, g_ref, v_ref, gate_ref):
    f = q_ref.shape[1]
    d = v_ref.shape[1]
    xn = _rmsnorm(x_ref[...], nw_ref[...]).astype(BF16)
    q = _mm(xn, w_ref[:, 0:f])
    q_ref[...] = q * _sigmoid(q)
    fl = _mm(xn, w_ref[:, f:2 * f])
    e = jnp.exp(-jnp.abs(fl))
    r = 1.0 / (1.0 + e)
    a = lb_ref[0:1, :]
    b = lb_ref[1:2, :] + (jnp.minimum(fl, 0.0) - jnp.log(1.0 + e))
    g_ref[...] = jnp.maximum(a, b) + _log1p_exp(-jnp.abs(a - b))
    k_ref[...] = lb_ref[2:3, :] * jnp.where(fl >= 0.0, e * r, r)
    v_ref[...] = _mm(xn, w_ref[:, 2 * f:2 * f + d])
    gt = _mm(xn, w_ref[:, 2 * f + d:])
    gate_ref[...] = gt * _sigmoid(gt)


def _hgrn_proj(x, nw, w_bf, lbp, tm):
    n, d = x.shape
    f = lbp.shape[1]
    row = lambda w: pl.BlockSpec((tm, w), lambda i: (i, 0))
    return pl.pallas_call(
        _hgrn_proj_body,
        grid=(n // tm,),
        in_specs=[row(d), _resident((1, d)), _resident(w_bf.shape), _resident(lbp.shape)],
        out_specs=[row(f), row(f), row(f), row(d), row(d)],
        out_shape=[jax.ShapeDtypeStruct((n, w), F32) for w in (f, f, f, d, d)],
        compiler_params=_params(1),
        name="hgrn_proj",
    )(x, nw, w_bf, lbp)


def _hgrn_chunk(q, k, g, v, sts, c):
    nh = len(sts)
    sub, half = HGRN_SUB, HGRN_SUB // 2
    hs = lambda x, h: x[:, h * LANES:(h + 1) * LANES]
    r_i = lax.broadcasted_iota(jnp.int32, (c, c), 0)
    c_i = lax.broadcasted_iota(jnp.int32, (c, c), 1)
    tril = jnp.where(r_i >= c_i, 1.0, 0.0).astype(F32)
    gc = jnp.dot(tril, g, precision=HIGHEST, preferred_element_type=F32) * LOG2E
    g_last = gc[c - 1:c, :]
    qe = (q * jnp.exp2(gc)).astype(BF16)
    o_inter = [_mm_nt(hs(qe, h), sts[h].astype(BF16)) for h in range(nh)]
    kd = (k * jnp.exp2(g_last - gc)).astype(BF16)
    vb = v.astype(BF16)
    st_dec = jnp.exp2(g_last)
    st_new = [sts[h] * hs(st_dec, h) + _mm_tn(hs(vb, h), hs(kd, h)) for h in range(nh)]

    half_row = lax.broadcasted_iota(jnp.int32, (half, 1), 0)
    out_rows = []
    for i in range(c // sub):
        r0 = i * sub
        gi, qi, ki, vi = gc[r0:r0 + sub], q[r0:r0 + sub], k[r0:r0 + sub], v[r0:r0 + sub]
        o_lo = [o_inter[h][r0:r0 + half] for h in range(nh)]
        o_hi = [o_inter[h][r0 + half:r0 + sub] for h in range(nh)]
        if i > 0:
            g_first = gc[r0:r0 + 1]
            qs = (qi * jnp.exp2(gi - g_first)).astype(BF16)
            ks = (k[:r0] * jnp.exp2(g_first - gc[:r0])).astype(BF16)
            att = [_mm_nt(hs(qs, h), hs(ks, h)).astype(BF16) for h in range(nh)]
            off = [_mm(att[h], hs(vb[:r0], h)) for h in range(nh)]
            o_lo = [o_lo[h] + off[h][:half] for h in range(nh)]
            o_hi = [o_hi[h] + off[h][half:] for h in range(nh)]
        for rows, acc, s_range in ((slice(0, half), o_lo, range(half)),
                                   (slice(half, sub), o_hi, range(sub))):
            g_rows, q_rows = gi[rows], qi[rows]
            for s in s_range:
                diff = g_rows - gi[s:s + 1]
                if s >= rows.start:
                    diff = jnp.where(half_row >= s - rows.start, diff, NEG)
                prod = q_rows * jnp.exp2(diff) * ki[s:s + 1]
                for h in range(nh):
                    col = jnp.sum(hs(prod, h), axis=-1, keepdims=True)
                    acc[h] = acc[h] + col * hs(vi, h)[s:s + 1]
        out_rows.append(jnp.concatenate(
            [jnp.concatenate([o_lo[h], o_hi[h]], axis=0) for h in range(nh)], axis=1))
    o = jnp.concatenate(out_rows, axis=0) if len(out_rows) > 1 else out_rows[0]
    return o, st_new


def _hgrn_rec_body(*refs, c, r, has_meta):
    if has_meta:
        (q_ref, k_ref, g_ref, v_ref, s0_ref, qm_ref, km_ref, gm_ref, vm_ref,
         o_ref, s_ref, om_ref, st_ref) = refs
    else:
        q_ref, k_ref, g_ref, v_ref, s0_ref, o_ref, s_ref, st_ref = refs
    blk = pl.program_id(1)
    nh = st_ref.shape[0]

    @pl.when(blk == 0)
    def _():
        sts = [s0_ref[0, h].T for h in range(nh)]
        if has_meta:
            o_m, sts = _hgrn_chunk(qm_ref[...], km_ref[...], gm_ref[...], vm_ref[...],
                                   sts, qm_ref.shape[0])
            om_ref[...] = o_m
        for h in range(nh):
            st_ref[h] = sts[h]

    def body(j, carry):
        sl = pl.ds(pl.multiple_of(j * c, c), c)
        o, sts = _hgrn_chunk(q_ref[sl, :], k_ref[sl, :], g_ref[sl, :], v_ref[sl, :],
                             [st_ref[h] for h in range(nh)], c)
        o_ref[sl, :] = o
        for h in range(nh):
            st_ref[h] = sts[h]
        return carry

    lax.fori_loop(0, r // c, body, 0)

    @pl.when(blk == pl.num_programs(1) - 1)
    def _():
        for h in range(nh):
            s_ref[0, h] = st_ref[h].T


def _hgrn_rec(q, k, g, v, s0, *, n_seq, seq_len, r, c, meta=None):
    _, h, dk, dv = s0.shape
    nblk = seq_len // r
    tok = lambda w: pl.BlockSpec((r, h * w), lambda b, i: (b * nblk + i, 0))
    st_spec = pl.BlockSpec((1, h, dk, dv), lambda b, i: (b, 0, 0, 0))
    in_specs = [tok(dk), tok(dk), tok(dk), tok(dv), st_spec]
    args = [q, k, g, v, s0]
    out_specs = [tok(dv), st_spec]
    out_shape = [jax.ShapeDtypeStruct((n_seq * seq_len, h * dv), F32),
                 jax.ShapeDtypeStruct(s0.shape, F32)]
    if meta is not None:
        qm, km, gm, vm, n_meta, meta_blk = meta
        mspec = lambda w: pl.BlockSpec((n_meta, h * w), lambda b, i: (meta_blk, 0))
        in_specs += [mspec(dk), mspec(dk), mspec(dk), mspec(dv)]
        args += [qm, km, gm, vm]
        out_specs.append(pl.BlockSpec((n_meta, h * dv), lambda b, i: (0, 0)))
        out_shape.append(jax.ShapeDtypeStruct((n_meta, h * dv), F32))
    return pl.pallas_call(
        functools.partial(_hgrn_rec_body, c=c, r=r, has_meta=meta is not None),
        grid=(n_seq, nblk),
        in_specs=in_specs,
        out_specs=out_specs,
        out_shape=out_shape,
        scratch_shapes=[pltpu.VMEM((h, dv, dk), F32)],
        compiler_params=_params(2),
        name="hgrn_rec",
    )(*args)


def _mix_ffn_body(*refs, head_dim, dff, fc):
    if head_dim:
        (o_ref, gate_ref, x_ref, gn_ref, wo_ref, pm_ref,
         nf_ref, wgu_ref, wd_ref, pf_ref, y_ref) = refs
        o = o_ref[...]
        parts = []
        for h in range(o.shape[1] // head_dim):
            oh = o[:, h * head_dim:(h + 1) * head_dim]
            parts.append(oh * lax.rsqrt(jnp.mean(oh * oh, axis=-1, keepdims=True) + EPS))
        o = (jnp.concatenate(parts, axis=-1) * gn_ref[...] * gate_ref[...]).astype(BF16)
    else:
        o_ref, x_ref, wo_ref, pm_ref, nf_ref, wgu_ref, wd_ref, pf_ref, y_ref = refs
        o = o_ref[...]
    x = x_ref[...] + _rmsnorm(_mm(o, wo_ref[...]), pm_ref[...])
    xn = _rmsnorm(x, nf_ref[...]).astype(BF16)
    acc = jnp.zeros(x.shape, F32)
    for c0 in range(0, dff, fc):
        a = _mm(xn, wgu_ref[:, c0:c0 + fc])
        u = _mm(xn, wgu_ref[:, dff + c0:dff + c0 + fc])
        hid = (a * _sigmoid(a) * u).astype(BF16)
        acc = acc + _mm(hid, wd_ref[c0:c0 + fc, :])
    y_ref[...] = x + _rmsnorm(acc, pf_ref[...])


def _mix_ffn(o, gate, x, gn, wo_bf, pm, nf, wgu_bf, wd_bf, pf, tm, head_dim):
    n, d = x.shape
    dff = wd_bf.shape[0]
    fc = dff // 2 if (dff // 2) % LANES == 0 else dff
    row = pl.BlockSpec((tm, d), lambda i: (i, 0))
    vec = _resident((1, d))
    ffn_specs = [vec, _resident(wgu_bf.shape), _resident(wd_bf.shape), vec]
    if head_dim:
        in_specs = [row, row, row, vec, _resident(wo_bf.shape), vec] + ffn_specs
        args = (o, gate, x, gn, wo_bf, pm, nf, wgu_bf, wd_bf, pf)
    else:
        in_specs = [row, row, _resident(wo_bf.shape), vec] + ffn_specs
        args = (o, x, wo_bf, pm, nf, wgu_bf, wd_bf, pf)
    return pl.pallas_call(
        functools.partial(_mix_ffn_body, head_dim=head_dim, dff=dff, fc=fc),
        grid=(n // tm,),
        in_specs=in_specs,
        out_specs=row,
        out_shape=jax.ShapeDtypeStruct((n, d), F32),
        compiler_params=_params(1),
        name="mix_ffn",
    )(*args)


def _fox_proj_body(x_ref, nw_ref, w_ref, wf_ref, bf_ref, k_ref, v_ref, lf_ref,
                   qb_ref, kb_ref, vb_ref, *, scale):
    d = k_ref.shape[1]
    nh = lf_ref.shape[1]
    xn = _rmsnorm(x_ref[...], nw_ref[...]).astype(BF16)
    qb_ref[...] = (_mm(xn, w_ref[:, 0:d]) * scale).astype(BF16)
    k = _mm(xn, w_ref[:, d:2 * d])
    k_ref[...] = k
    kb_ref[...] = k.astype(BF16)
    v = _mm(xn, w_ref[:, 2 * d:3 * d])
    v_ref[...] = v
    vb_ref[...] = v.astype(BF16)
    lf = _log_sigmoid(_mm(xn, wf_ref[...]) + bf_ref[...])
    lf_ref[...] = lf[:, 0:nh]


def _fox_proj(x, nw, w_bf, wf_bf, bf_row, nh, tm):
    n, d = x.shape
    row = lambda w: pl.BlockSpec((tm, w), lambda i: (i, 0))
    scale = float(d // nh) ** -0.5
    return pl.pallas_call(
        functools.partial(_fox_proj_body, scale=scale),
        grid=(n // tm,),
        in_specs=[row(d), _resident((1, d)), _resident(w_bf.shape), _resident(wf_bf.shape),
                  _resident(bf_row.shape)],
        out_specs=[row(d), row(d), row(nh), row(d), row(d), row(d)],
        out_shape=[jax.ShapeDtypeStruct((n, d), F32), jax.ShapeDtypeStruct((n, d), F32),
                   jax.ShapeDtypeStruct((n, nh), F32), jax.ShapeDtypeStruct((n, d), BF16),
                   jax.ShapeDtypeStruct((n, d), BF16), jax.ShapeDtypeStruct((n, d), BF16)],
        compiler_params=_params(1),
        name="fox_proj",
    )(x, nw, w_bf, wf_bf, bf_row)


def _head_sq_norms(xt, nh):
    dh = xt.shape[0] // nh
    x2 = xt * xt
    return jnp.concatenate([jnp.sum(x2[h * dh:(h + 1) * dh], axis=0, keepdims=True)
                            for h in range(nh)], axis=0)


def _fox_proj_t_body(x_ref, nw_ref, wt_ref, wf_ref, bf_ref, kt_ref, vt_ref, lf_ref,
                     qtb_ref, kb_ref, vtb_ref, qsq_ref, ksq_ref, *, qscale):
    d = kt_ref.shape[0]
    nh = lf_ref.shape[1]
    xn = _rmsnorm(x_ref[...], nw_ref[...]).astype(BF16)
    qtb = (_mm_nt(wt_ref[0:d, :], xn) * qscale).astype(BF16)
    qtb_ref[...] = qtb
    qsq_ref[...] = _head_sq_norms(qtb.astype(F32), nh)
    kt = _mm_nt(wt_ref[d:2 * d, :], xn)
    kt_ref[...] = kt
    ksq_ref[...] = _head_sq_norms(kt.astype(BF16).astype(F32), nh)
    kb_ref[...] = _mm_nt(xn, wt_ref[d:2 * d, :]).astype(BF16)
    vt = _mm_nt(wt_ref[2 * d:3 * d, :], xn)
    vt_ref[...] = vt
    vtb_ref[...] = vt.astype(BF16)
    lf = _log_sigmoid(_mm(xn, wf_ref[...]) + bf_ref[...])
    lf_ref[...] = lf[:, 0:nh]


def _fox_proj_t(x, nw, wt_bf, wf_bf, bf_row, nh, tm, qscale):
    n, d = x.shape
    row = lambda w: pl.BlockSpec((tm, w), lambda i: (i, 0))
    col = pl.BlockSpec((d, tm), lambda i: (0, i))
    sq = pl.BlockSpec((nh, tm), lambda i: (0, i))
    return pl.pallas_call(
        functools.partial(_fox_proj_t_body, qscale=qscale),
        grid=(n // tm,),
        in_specs=[row(d), _resident((1, d)), _resident(wt_bf.shape), _resident(wf_bf.shape),
                  _resident(bf_row.shape)],
        out_specs=[col, col, row(nh), col, row(d), col, sq, sq],
        out_shape=[jax.ShapeDtypeStruct((d, n), F32), jax.ShapeDtypeStruct((d, n), F32),
                   jax.ShapeDtypeStruct((n, nh), F32), jax.ShapeDtypeStruct((d, n), BF16),
                   jax.ShapeDtypeStruct((n, d), BF16), jax.ShapeDtypeStruct((d, n), BF16),
                   jax.ShapeDtypeStruct((nh, n), F32), jax.ShapeDtypeStruct((nh, n), F32)],
        compiler_params=_params(1),
        name="fox_proj_t",
    )(x, nw, wt_bf, wf_bf, bf_row)


BIAS_LANES_PER_HEAD = 6


def _split3(v):
    hi = v.astype(BF16).astype(F32)
    r1 = v - hi
    mid = r1.astype(BF16).astype(F32)
    lo = (r1 - mid).astype(BF16).astype(F32)
    return hi, mid, lo


def _fox_kbias_body(c_ref, aux_ref, *, cscale):
    tm = c_ref.shape[0]
    npair = aux_ref.shape[0]
    c = c_ref[...] * (-cscale)
    lane = lax.broadcasted_iota(jnp.int32, (tm, LANES), 1)
    slot = lane % BIAS_LANES_PER_HEAD
    piece = slot % 3
    used = lane < 2 * BIAS_LANES_PER_HEAD
    for p in range(npair):
        c0 = jnp.broadcast_to(c[:, 2 * p:2 * p + 1], (tm, LANES))
        c1 = jnp.broadcast_to(c[:, 2 * p + 1:2 * p + 2], (tm, LANES))
        hi, mid, lo = _split3(jnp.where(lane < BIAS_LANES_PER_HEAD, c0, c1))
        val = jnp.where(piece == 0, hi, jnp.where(piece == 1, mid, lo))
        val = jnp.where(slot < 3, val, 1.0)
        aux_ref[p] = jnp.where(used, val, 0.0).astype(BF16)


def _fox_kbias(c_rows, npair, tm, cscale):
    n, nh = c_rows.shape
    return pl.pallas_call(
        functools.partial(_fox_kbias_body, cscale=cscale),
        grid=(n // tm,),
        in_specs=[pl.BlockSpec((tm, nh), lambda i: (i, 0))],
        out_specs=pl.BlockSpec((npair, tm, LANES), lambda i: (0, i, 0)),
        out_shape=jax.ShapeDtypeStruct((npair, n, LANES), BF16),
        compiler_params=_params(1),
        name="fox_kbias",
    )(c_rows)


def _lane_cumsum_body(x_ref, y_ref, *, reverse_exclusive):
    rows, n = x_ref.shape
    nb = n // LANES
    t_i = lax.broadcasted_iota(jnp.int32, (LANES, LANES), 0)
    k_i = lax.broadcasted_iota(jnp.int32, (LANES, LANES), 1)
    sel = (t_i > k_i) if reverse_exclusive else (t_i <= k_i)
    tri = jnp.where(sel, 1.0, 0.0).astype(F32)

    def body(i, carry):
        b = (nb - 1 - i) if reverse_exclusive else i
        sl = pl.ds(pl.multiple_of(b * LANES, LANES), LANES)
        xb = x_ref[:, sl]
        y_ref[:, sl] = jnp.dot(xb, tri, precision=HIGHEST, preferred_element_type=F32) + carry
        return carry + jnp.sum(xb, axis=-1, keepdims=True)

    lax.fori_loop(0, nb, body, jnp.zeros((rows, 1), F32))


def _lane_cumsum(x, reverse_exclusive, tr):
    rows, n = x.shape
    spec = pl.BlockSpec((tr, n), lambda i: (i, 0))
    return pl.pallas_call(
        functools.partial(_lane_cumsum_body, reverse_exclusive=reverse_exclusive),
        grid=(rows // tr,),
        in_specs=[spec],
        out_specs=spec,
        out_shape=jax.ShapeDtypeStruct((rows, n), F32),
        compiler_params=_params(1),
        name="lane_cumsum",
    )(x)


ATTN_TILE = 512
ATTN_PAIRS_PER_STEP = 2
ACC_PAD = 16


def _fox_attn_body(jlo_ref, c_ref, qt_ref, k_ref, ka_ref, vt_ref, kme_ref, vtm_ref,
                   qm_ref, km_ref, vm_ref, o_ref, om_ref, qx_s, m_s, acc_s,
                   *, tq, tk, dh, n_meta, cscale, npg):
    i = pl.program_id(1)
    heads = range(2 * npg)
    pair_lanes = lambda x, pp: x[:, pp * 2 * dh:(pp + 1) * 2 * dh]
    pair_rows = lambda x, pp: x[pp * 2 * dh:(pp + 1) * 2 * dh, :]
    lane = lax.broadcasted_iota(jnp.int32, (1, 2 * dh), 1)
    head_sel = [lane < dh, lane >= dh]

    @pl.when(i == 0)
    def _():
        r_i = lax.broadcasted_iota(jnp.int32, (n_meta, n_meta), 0)
        c_i = lax.broadcasted_iota(jnp.int32, (n_meta, n_meta), 1)
        outs = []
        for pp in range(npg):
            qm, km, vm = (pair_lanes(r[...], pp) for r in (qm_ref, km_ref, vm_ref))
            out = jnp.zeros((n_meta, 2 * dh), F32)
            for a in range(2):
                qa = jnp.where(head_sel[a], qm, jnp.zeros_like(qm))
                x = _mm_nt(qa, km) - c_ref[2 * pp + a, :, 0:n_meta]
                x = jnp.where(c_i <= r_i, x, NEG)
                p = jnp.exp(x - jnp.max(x, axis=-1, keepdims=True))
                oa = _mm(p.astype(BF16), vm) / jnp.sum(p, axis=-1, keepdims=True)
                out = jnp.where(head_sel[a], oa, out)
            outs.append(out)
        om_ref[...] = jnp.concatenate(outs, axis=1).astype(BF16)

    q0 = pl.multiple_of(LANES + i * tq, LANES)
    row = lax.broadcasted_iota(jnp.int32, (2 * dh, tq), 0)
    for hh in heads:
        pp, a = divmod(hh, 2)
        qt = pair_rows(qt_ref[...], pp)
        hi, mid, lo = _split3(c_ref[hh, :, pl.ds(q0, tq)] * cscale)
        slot = row - a * BIAS_LANES_PER_HEAD
        bias = jnp.where(slot == 3, hi, jnp.where(slot == 4, mid, jnp.where(slot == 5, lo, 1.0)))
        bias = jnp.where((slot >= 0) & (slot < BIAS_LANES_PER_HEAD), bias, 0.0)
        qx_s[hh, 0:2 * dh, :] = jnp.where((row >= a * dh) & (row < (a + 1) * dh), qt,
                                          jnp.zeros_like(qt))
        qx_s[hh, 2 * dh:4 * dh, :] = bias.astype(BF16)
        m_s[hh] = jnp.full((1, tq), NEG, F32)
        acc_s[hh] = jnp.zeros((dh + ACC_PAD, tq), F32)

    def scores(kxs):
        return [_mm(kxs[hh // 2], qx_s[hh]) for hh in heads]

    def update(sts, vts, masked):
        tkb = vts[0].shape[1]
        ones = jnp.where(lax.broadcasted_iota(jnp.int32, (ACC_PAD, tkb), 0) == 0,
                         1.0, 0.0).astype(BF16)
        for hh in heads:
            pp, a = divmod(hh, 2)
            st = sts[hh]
            if masked:
                k_i = lax.broadcasted_iota(jnp.int32, (tkb, tq), 0)
                q_i = lax.broadcasted_iota(jnp.int32, (tkb, tq), 1)
                st = jnp.where(k_i <= q_i, st, NEG)
            m_prev = m_s[hh]
            m_new = jnp.maximum(m_prev, jnp.max(st, axis=0, keepdims=True))
            p = jnp.exp2(st - m_new).astype(BF16)
            vx = jnp.concatenate([vts[pp][a * dh:(a + 1) * dh, :], ones], axis=0)
            acc_s[hh] = jnp.exp2(m_prev - m_new) * acc_s[hh] + _mm(vx, p)
            m_s[hh] = m_new

    def key_block(j):
        k0 = pl.multiple_of(j * tk, tk)
        kxs = [jnp.concatenate([pair_lanes(k_ref[pl.ds(k0, tk), :], pp),
                                ka_ref[pp, pl.ds(k0, tk), :]], axis=1) for pp in range(npg)]
        vts = [pair_rows(vt_ref[:, pl.ds(k0, tk)], pp) for pp in range(npg)]
        return kxs, vts

    def body(j, carry):
        kxs, vts = key_block(j)
        update(scores(kxs), vts, False)
        return carry

    lax.fori_loop(jlo_ref[pl.program_id(0), i], i, body, 0)

    kxs, vts = key_block(i)
    st_diag = scores(kxs)
    st_meta = scores([kme_ref[pp] for pp in range(npg)])
    update(st_diag, vts, True)
    update(st_meta, [pair_rows(vtm_ref[...], pp) for pp in range(npg)], False)

    outs = []
    for pp in range(npg):
        halves = []
        for a in range(2):
            acc = acc_s[2 * pp + a]
            halves.append(acc[0:dh] / acc[dh:dh + 1])
        outs.append(jnp.concatenate(halves, axis=0).T)
    o_ref[...] = jnp.concatenate(outs, axis=1).astype(BF16)


def _fox_attn(jlo, c3, qt_b, kb, kaux, vt_b, kx_meta, vt_meta, qb_aux, kb_aux, vb_aux,
              n_meta, meta_blk, dh, tq, cscale, npg):
    n, d = kb.shape
    gw = 2 * dh * npg
    ngroups = d // gw
    nl = c3.shape[2]
    mspec = pl.BlockSpec((n_meta, gw), lambda p, i, _: (meta_blk, p))
    once = dict(pipeline_mode=pl.Buffered(1))
    return pl.pallas_call(
        functools.partial(_fox_attn_body, tq=tq, tk=tq, dh=dh, n_meta=n_meta, cscale=cscale,
                          npg=npg),
        grid_spec=pltpu.PrefetchScalarGridSpec(
            num_scalar_prefetch=1,
            grid=(ngroups, n // tq),
            in_specs=[pl.BlockSpec((2 * npg, 1, nl), lambda p, i, _: (p, 0, 0)),
                      pl.BlockSpec((gw, tq), lambda p, i, _: (p, i)),
                      pl.BlockSpec((n, gw), lambda p, i, _: (0, p), **once),
                      pl.BlockSpec((npg, n, LANES), lambda p, i, _: (p, 0, 0), **once),
                      pl.BlockSpec((gw, n), lambda p, i, _: (p, 0), **once),
                      pl.BlockSpec((npg, n_meta, 4 * dh), lambda p, i, _: (p, 0, 0)),
                      pl.BlockSpec((gw, n_meta), lambda p, i, _: (p, 0)),
                      mspec, mspec, mspec],
            out_specs=[pl.BlockSpec((tq, gw), lambda p, i, _: (i, p)),
                       pl.BlockSpec((n_meta, gw), lambda p, i, _: (0, p))],
            scratch_shapes=[pltpu.VMEM((2 * npg, 4 * dh, tq), BF16),
                            pltpu.VMEM((2 * npg, 1, tq), F32),
                            pltpu.VMEM((2 * npg, dh + ACC_PAD, tq), F32)]),
        out_shape=[jax.ShapeDtypeStruct((n, d), BF16), jax.ShapeDtypeStruct((n_meta, d), BF16)],
        compiler_params=_params(2),
        name="fox_attn",
    )(jlo, c3, qt_b, kb, kaux, vt_b, kx_meta, vt_meta, qb_aux, kb_aux, vb_aux)


PRUNE_NATS = 50.0
NORM_SLACK = 1.01


def _fox_first_key_block(c_main, qsq, ksq, tq, group):
    nh, n = c_main.shape
    nq = n // tq
    u = jnp.sqrt(jnp.max(qsq, axis=1) * jnp.max(ksq, axis=1)) * (NORM_SLACK / LOG2E)
    c_first = c_main[:, 0::tq]
    c_last = c_main[:, tq - 1::tq]
    bound = 2.0 * u[:, None, None] + c_first[:, :, None] - c_last[:, None, :]
    j_lt_i = jnp.arange(nq)[None, :] < jnp.arange(nq)[:, None]
    skip = (bound <= -PRUNE_NATS) & j_lt_i[None]
    jlo = jnp.sum(skip, axis=2).astype(jnp.int32)
    return jnp.min(jlo.reshape(nh // group, group, nq), axis=1)


def _fox_sample_body(q_ref, kn_ref, vn_ref, cn_ref, kc_ref, vc_ref, suf_ref, o_ref,
                     m_s, l_s, acc_s, *, nh, dh, s_len):
    t = pl.program_id(1)

    @pl.when(t == 0)
    def _():
        r_i = lax.broadcasted_iota(jnp.int32, (s_len, s_len), 0)
        c_i = lax.broadcasted_iota(jnp.int32, (s_len, s_len), 1)
        cols = [slice(h * dh, (h + 1) * dh) for h in range(nh)]
        x = [_mm_nt(q_ref[:, cols[h]], kn_ref[:, cols[h]]) - cn_ref[0, h:h + 1, 0:s_len]
             for h in range(nh)]
        x = [jnp.where(c_i <= r_i, x[h], NEG) for h in range(nh)]
        m = [jnp.max(x[h], axis=-1, keepdims=True) for h in range(nh)]
        p = [jnp.exp(x[h] - m[h]) for h in range(nh)]
        pv = [_mm(p[h].astype(BF16), vn_ref[:, cols[h]]) for h in range(nh)]
        for h in range(nh):
            m_s[h] = m[h]
            l_s[h] = jnp.sum(p[h], axis=-1, keepdims=True)
            acc_s[h] = pv[h]

    heads = range(nh)
    x = [_mm(q_ref[:, h * dh:(h + 1) * dh], kc_ref[0, h].astype(BF16)) + suf_ref[0, h:h + 1, :]
         for h in heads]
    m_prev = [m_s[h] for h in heads]
    m_new = [jnp.maximum(m_prev[h], jnp.max(x[h], axis=-1, keepdims=True)) for h in heads]
    p = [jnp.exp(x[h] - m_new[h]) for h in heads]
    pv = [_mm_nt(p[h].astype(BF16), vc_ref[0, h].astype(BF16)) for h in heads]
    for h in heads:
        alpha = jnp.exp(m_prev[h] - m_new[h])
        l_s[h] = alpha * l_s[h] + jnp.sum(p[h], axis=-1, keepdims=True)
        acc_s[h] = alpha * acc_s[h] + pv[h]
        m_s[h] = m_new[h]

    @pl.when(t == pl.num_programs(1) - 1)
    def _():
        o_ref[...] = jnp.concatenate([acc_s[h] / l_s[h] for h in range(nh)], axis=-1).astype(BF16)


def _fox_sample(qb, kb, vb, cn, kc_t, vc_t, layer, suf, s_len, tk):
    _, nb, nh, dh, past = kc_t.shape
    d = nh * dh
    tok = pl.BlockSpec((s_len, d), lambda b, t: (b, 0))
    cache = pl.BlockSpec((None, 1, nh, dh, tk), lambda b, t: (layer, b, 0, 0, t))
    return pl.pallas_call(
        functools.partial(_fox_sample_body, nh=nh, dh=dh, s_len=s_len),
        grid=(nb, past // tk),
        in_specs=[tok, tok, tok, pl.BlockSpec((1, nh, LANES), lambda b, t: (b, 0, 0)),
                  cache, cache, pl.BlockSpec((1, nh, tk), lambda b, t: (b, 0, t))],
        out_specs=tok,
        out_shape=jax.ShapeDtypeStruct((nb * s_len, d), BF16),
        scratch_shapes=[pltpu.VMEM((nh, s_len, 1), F32), pltpu.VMEM((nh, s_len, 1), F32),
                        pltpu.VMEM((nh, s_len, dh), F32)],
        compiler_params=_params(2),
        name="fox_sample",
    )(qb, kb, vb, cn, kc_t, vc_t, suf)


def _largest_tile(n, cap):
    t = min(n, cap)
    while n % t:
        t //= 2
    return t


def kernel(x_prompt, x_sample, state_hgrn, cache_k, cache_v, cache_logf, meta_tokens,
           norm_pre_mix, norm_post_mix, norm_pre_ffn, norm_post_ffn,
           a_w_in, a_lb_logits, a_g_norm, a_w_out, b_w_in, b_f, b_w_out,
           ffn_w_gu, ffn_w_down):
    batch, seq, d = x_prompt.shape
    assert batch == 1, "one prompt stream per step"
    n_dec, s_len, _ = x_sample.shape
    n_meta = meta_tokens.shape[0]
    depth = norm_pre_mix.shape[0]
    _, _, a_heads, a_dk, a_dv = state_hgrn.shape
    _, _, past, b_heads, b_dh = cache_k.shape
    ns = n_dec * s_len
    assert s_len % HGRN_SUB == 0 and n_meta % HGRN_SUB == 0 and ns % n_meta == 0
    assert 2 * b_dh == LANES and a_dk == LANES and a_dv == LANES

    tm = _largest_tile(seq, 512)
    n_aux = ns + n_meta
    meta_blk = ns // n_meta
    rec_r = _largest_tile(seq, 512)
    rec_c = _largest_tile(rec_r, 64)
    tq = _largest_tile(seq, ATTN_TILE)
    tk_cache = _largest_tile(past, 1024)
    npair = b_heads // 2
    npg = ATTN_PAIRS_PER_STEP if npair % ATTN_PAIRS_PER_STEP == 0 else 1

    sm = jax.nn.softmax(a_lb_logits.astype(F32), axis=0)
    lb_all = jnp.cumsum(sm, axis=0) - sm[0]

    xm = x_prompt.reshape(seq, d)
    xa = jnp.concatenate([x_sample.reshape(ns, d), meta_tokens.astype(F32)], axis=0)
    row = lambda w: w.reshape(1, -1).astype(F32)
    cache_kt = jnp.transpose(cache_k, (0, 1, 3, 4, 2))
    cache_vt = jnp.transpose(cache_v, (0, 1, 3, 4, 2))
    cache_lt = jnp.swapaxes(cache_logf.astype(F32), 2, 3)

    st_p, st_s, k_p, v_p, lf_p, k_s, v_s, lf_s = [], [], [], [], [], [], [], []
    for i in range(depth):
        j = i // 2
        if i % 2 == 0:
            lb = lb_all[j]
            lbp = jnp.stack([jnp.log(jnp.maximum(lb, LB_FLOOR)), jnp.log1p(-lb), 1.0 - lb])
            w_in = a_w_in[j].astype(BF16)
            w_out = a_w_out[j].astype(BF16)
            qm, km, gm, vm, gtm = _hgrn_proj(xm, row(norm_pre_mix[i]), w_in, lbp, tm)
            qa, ka, ga, va, gta = _hgrn_proj(xa, row(norm_pre_mix[i]), w_in, lbp, n_aux)
            o_s, s_s = _hgrn_rec(qa, ka, ga, va, state_hgrn[j].astype(F32),
                                 n_seq=n_dec, seq_len=s_len, r=s_len, c=s_len)
            o_m, s_p, o_meta = _hgrn_rec(qm, km, gm, vm,
                                         jnp.zeros((1, a_heads, a_dk, a_dv), F32),
                                         n_seq=1, seq_len=seq, r=rec_r, c=rec_c,
                                         meta=(qa, ka, ga, va, n_meta, meta_blk))
            st_p.append(s_p.astype(state_hgrn.dtype))
            st_s.append(s_s.astype(state_hgrn.dtype))
            o_a = jnp.concatenate([o_s, o_meta], axis=0)
            mix_m = (o_m, gtm, xm, row(a_g_norm[j]))
            mix_a = (o_a, gta, xa, row(a_g_norm[j]))
            head_dim = a_dv
        else:
            w_t = jnp.swapaxes(b_w_in[j], 0, 1)
            wt_qkv = w_t[:3 * d].astype(BF16)
            w_in = jnp.swapaxes(wt_qkv, 0, 1)
            w_f = jnp.pad(b_w_in[j][:, 3 * d:], ((0, 0), (0, LANES - b_heads))).astype(BF16)
            bf_row = jnp.pad(b_f[j].astype(F32), (0, LANES - b_heads)).reshape(1, LANES)
            w_out = b_w_out[j].astype(BF16)
            scale = float(b_dh) ** -0.5
            kt_m, vt_m, lf_m, qtb_m, kb_m, vtb_m, qsq, ksq = _fox_proj_t(
                xm, row(norm_pre_mix[i]), wt_qkv, w_f, bf_row, b_heads, tm, scale * LOG2E)
            kf_a, vf_a, lf_a, qb_a, kb_a, vb_a = _fox_proj(
                xa, row(norm_pre_mix[i]), w_in, w_f, bf_row, b_heads, n_aux)
            lf_t = jnp.concatenate(
                [jnp.pad(lf_a[ns:].T, ((0, 0), (0, LANES - n_meta))), lf_m.T], axis=1)
            c_all = _lane_cumsum(lf_t, False, b_heads)
            kaux = _fox_kbias(c_all[:, LANES:].T, npair, tm, LOG2E)
            kaux_meta = _fox_kbias(c_all[:, :n_meta].T, npair, n_meta, LOG2E)
            k_meta = jnp.swapaxes(kb_a[ns:].reshape(n_meta, npair, LANES), 0, 1)
            jlo = _fox_first_key_block(c_all[:, LANES:], qsq, ksq, tq, 2 * npg)
            o_m, o_meta = _fox_attn(jlo, c_all.reshape(b_heads, 1, -1), qtb_m, kb_m, kaux, vtb_m,
                                    jnp.concatenate([k_meta, kaux_meta], axis=-1), vb_a[ns:].T,
                                    qb_a, kb_a, vb_a, n_meta, meta_blk, b_dh, tq, LOG2E, npg)
            cl_t = cache_lt[j].reshape(n_dec * b_heads, past)
            tr = _largest_tile(n_dec * b_heads, LANES)
            suf = _lane_cumsum(cl_t, True, tr).reshape(n_dec, b_heads, past)
            ln_t = jnp.swapaxes(lf_a[:ns].reshape(n_dec, s_len, b_heads), 1, 2)
            ln_t = jnp.pad(ln_t, ((0, 0), (0, 0), (0, LANES - s_len)))
            cn = _lane_cumsum(ln_t.reshape(n_dec * b_heads, LANES), False, tr)
            o_s = _fox_sample(qb_a, kb_a, vb_a, cn.reshape(n_dec, b_heads, LANES),
                              cache_kt, cache_vt, j, suf, s_len, tk_cache)
            o_a = jnp.concatenate([o_s, o_meta], axis=0)
            mix_m = (o_m, None, xm, None)
            mix_a = (o_a, None, xa, None)
            head_dim = 0
            shp = lambda t, n: t.reshape(-1, n, b_heads, b_dh)
            k_p.append((kf_a[ns:].T, kt_m))
            v_p.append((vf_a[ns:].T, vt_m))
            lf_p.append(jnp.concatenate([lf_a[ns:], lf_m], axis=0).reshape(1, n_meta + seq, b_heads))
            k_s.append(shp(kf_a[:ns], s_len))
            v_s.append(shp(vf_a[:ns], s_len))
            lf_s.append(lf_a[:ns].reshape(n_dec, s_len, b_heads))
        ffn_w = (w_out, row(norm_post_mix[i]), row(norm_pre_ffn[i]), ffn_w_gu[i].astype(BF16),
                 ffn_w_down[i].astype(BF16), row(norm_post_ffn[i]))
        xm = _mix_ffn(*mix_m, *ffn_w, tm, head_dim)
        xa = _mix_ffn(*mix_a, *ffn_w, n_aux, head_dim)

    def prompt_kv(parts):
        t = jnp.concatenate([jnp.stack([m for m, _ in parts]), jnp.stack([x for _, x in parts])],
                            axis=2)
        t = t.reshape(len(parts), b_heads, b_dh, n_meta + seq)
        return jnp.transpose(t, (0, 3, 1, 2))[:, None]

    return (xm.reshape(1, seq, d), xa[:ns].reshape(n_dec, s_len, d),
            jnp.stack(st_p), prompt_kv(k_p), prompt_kv(v_p), jnp.stack(lf_p),
            jnp.stack(st_s), jnp.stack(k_s), jnp.stack(v_s), jnp.stack(lf_s))
```

```python
import functools

import jax
import jax.numpy as jnp
from jax import lax
from jax.experimental import pallas as pl
from jax.experimental.pallas import tpu as pltpu

F32 = jnp.float32
BF16 = jnp.bfloat16
EPS = 1e-6
NEG = -1e30
LB_FLOOR = 1e-30

LANES = 128
VMEM_LIMIT = 56 * 1024 * 1024
HGRN_SUB = 16
HIGHEST = lax.Precision.HIGHEST
LOG2E = 1.4426950408889634


def _params(n_grid):
    return pltpu.CompilerParams(dimension_semantics=("arbitrary",) * n_grid,
                                vmem_limit_bytes=VMEM_LIMIT)


def _resident(shape):
    nd = len(shape)
    return pl.BlockSpec(shape, lambda *_: (0,) * nd, pipeline_mode=pl.Buffered(1))


def _rmsnorm(x, w):
    return x * lax.rsqrt(jnp.mean(x * x, axis=-1, keepdims=True) + EPS) * w


def _sigmoid(x):
    return 1.0 / (1.0 + jnp.exp(-x))


def _log1p_exp(z):
    return jnp.log(1.0 + jnp.exp(z))


def _log_sigmoid(x):
    return jnp.minimum(x, 0.0) - _log1p_exp(-jnp.abs(x))


def _mm(a, b):
    return jnp.dot(a, b, preferred_element_type=F32)


def _mm_nt(a, b):
    return lax.dot_general(a, b, (((1,), (1,)), ((), ())), preferred_element_type=F32)


def _mm_tn(a, b):
    return lax.dot_general(a, b, (((0,), (0,)), ((), ())), preferred_element_type=F32)


def _hgrn_proj_body(x_ref, nw_ref, w_ref, lb_ref, q_ref, k_ref, g_ref, v_ref, gate_ref):
    f = q_ref.shape[1]
    d = v_ref.shape[1]
    xn = _rmsnorm(x_ref[...], nw_ref[...]).astype(BF16)
    q = _mm(xn, w_ref[:, 0:f])
    q_ref[...] = q * _sigmoid(q)
    fl = _mm(xn, w_ref[:, f:2 * f])
    e = jnp.exp(-jnp.abs(fl))
    r = 1.0 / (1.0 + e)
    a = lb_ref[0:1, :]
    b = lb_ref[1:2, :] + (jnp.minimum(fl, 0.0) - jnp.log(1.0 + e))
    g_ref[...] = jnp.maximum(a, b) + _log1p_exp(-jnp.abs(a - b))
    k_ref[...] = lb_ref[2:3, :] * jnp.where(fl >= 0.0, e * r, r)
    v_ref[...] = _mm(xn, w_ref[:, 2 * f:2 * f + d])
    gt = _mm(xn, w_ref[:, 2 * f + d:])
    gate_ref[...] = gt * _sigmoid(gt)


def _hgrn_proj(x, nw, w_bf, lbp, tm):
    n, d = x.shape
    f = lbp.shape[1]
    row = lambda w: pl.BlockSpec((tm, w), lambda i: (i, 0))
    return pl.pallas_call(
        _hgrn_proj_body,
        grid=(n // tm,),
        in_specs=[row(d), _resident((1, d)), _resident(w_bf.shape), _resident(lbp.shape)],
        out_specs=[row(f), row(f), row(f), row(d), row(d)],
        out_shape=[jax.ShapeDtypeStruct((n, w), F32) for w in (f, f, f, d, d)],
        compiler_params=_params(1),
        name="hgrn_proj",
    )(x, nw, w_bf, lbp)


def _hgrn_chunk(q, k, g, v, sts, c):
    nh = len(sts)
    sub, half = HGRN_SUB, HGRN_SUB // 2
    hs = lambda x, h: x[:, h * LANES:(h + 1) * LANES]
    r_i = lax.broadcasted_iota(jnp.int32, (c, c), 0)
    c_i = lax.broadcasted_iota(jnp.int32, (c, c), 1)
    tril = jnp.where(r_i >= c_i, 1.0, 0.0).astype(F32)
    gc = jnp.dot(tril, g, precision=HIGHEST, preferred_element_type=F32) * LOG2E
    g_last = gc[c - 1:c, :]
    qe = (q * jnp.exp2(gc)).astype(BF16)
    o_inter = [_mm_nt(hs(qe, h), sts[h].astype(BF16)) for h in range(nh)]
    kd = (k * jnp.exp2(g_last - gc)).astype(BF16)
    vb = v.astype(BF16)
    st_dec = jnp.exp2(g_last)
    st_new = [sts[h] * hs(st_dec, h) + _mm_tn(hs(vb, h), hs(kd, h)) for h in range(nh)]

    lk = jnp.log2(k) - gc
    half_row = lax.broadcasted_iota(jnp.int32, (half, 1), 0)
    out_rows = []
    for i in range(c // sub):
        r0 = i * sub
        gi, qi, vi, lki = gc[r0:r0 + sub], q[r0:r0 + sub], v[r0:r0 + sub], lk[r0:r0 + sub]
        o_lo = [o_inter[h][r0:r0 + half] for h in range(nh)]
        o_hi = [o_inter[h][r0 + half:r0 + sub] for h in range(nh)]
        if i > 0:
            g_first = gc[r0:r0 + 1]
            qs = (qi * jnp.exp2(gi - g_first)).astype(BF16)
            ks = (k[:r0] * jnp.exp2(g_first - gc[:r0])).astype(BF16)
            att = [_mm_nt(hs(qs, h), hs(ks, h)).astype(BF16) for h in range(nh)]
            off = [_mm(att[h], hs(vb[:r0], h)) for h in range(nh)]
            o_lo = [o_lo[h] + off[h][:half] for h in range(nh)]
            o_hi = [o_hi[h] + off[h][half:] for h in range(nh)]
        for rows, acc, s_range in ((slice(0, half), o_lo, range(half)),
                                   (slice(half, sub), o_hi, range(sub))):
            g_rows, q_rows = gi[rows], qi[rows]
            for s in s_range:
                diff = g_rows + lki[s:s + 1]
                if s >= rows.start:
                    diff = jnp.where(half_row >= s - rows.start, diff, NEG)
                prod = q_rows * jnp.exp2(diff)
                for h in range(nh):
                    col = jnp.sum(hs(prod, h), axis=-1, keepdims=True)
                    acc[h] = acc[h] + col * hs(vi, h)[s:s + 1]
        out_rows.append(jnp.concatenate(
            [jnp.concatenate([o_lo[h], o_hi[h]], axis=0) for h in range(nh)], axis=1))
    o = jnp.concatenate(out_rows, axis=0) if len(out_rows) > 1 else out_rows[0]
    return o, st_new


def _hgrn_rec_body(*refs, c, r, has_meta):
    if has_meta:
        (q_ref, k_ref, g_ref, v_ref, s0_ref, qm_ref, km_ref, gm_ref, vm_ref,
         o_ref, s_ref, om_ref, st_ref) = refs
    else:
        q_ref, k_ref, g_ref, v_ref, s0_ref, o_ref, s_ref, st_ref = refs
    blk = pl.program_id(1)
    nh = st_ref.shape[0]

    @pl.when(blk == 0)
    def _():
        sts = [s0_ref[0, h].T for h in range(nh)]
        if has_meta:
            o_m, sts = _hgrn_chunk(qm_ref[...], km_ref[...], gm_ref[...], vm_ref[...],
                                   sts, qm_ref.shape[0])
            om_ref[...] = o_m
        for h in range(nh):
            st_ref[h] = sts[h]

    def body(j, carry):
        sl = pl.ds(pl.multiple_of(j * c, c), c)
        o, sts = _hgrn_chunk(q_ref[sl, :], k_ref[sl, :], g_ref[sl, :], v_ref[sl, :],
                             [st_ref[h] for h in range(nh)], c)
        o_ref[sl, :] = o
        for h in range(nh):
            st_ref[h] = sts[h]
        return carry

    lax.fori_loop(0, r // c, body, 0)

    @pl.when(blk == pl.num_programs(1) - 1)
    def _():
        for h in range(nh):
            s_ref[0, h] = st_ref[h].T


def _hgrn_rec(q, k, g, v, s0, *, n_seq, seq_len, r, c, meta=None):
    _, h, dk, dv = s0.shape
    nblk = seq_len // r
    tok = lambda w: pl.BlockSpec((r, h * w), lambda b, i: (b * nblk + i, 0))
    st_spec = pl.BlockSpec((1, h, dk, dv), lambda b, i: (b, 0, 0, 0))
    in_specs = [tok(dk), tok(dk), tok(dk), tok(dv), st_spec]
    args = [q, k, g, v, s0]
    out_specs = [tok(dv), st_spec]
    out_shape = [jax.ShapeDtypeStruct((n_seq * seq_len, h * dv), F32),
                 jax.ShapeDtypeStruct(s0.shape, F32)]
    if meta is not None:
        qm, km, gm, vm, n_meta, meta_blk = meta
        mspec = lambda w: pl.BlockSpec((n_meta, h * w), lambda b, i: (meta_blk, 0))
        in_specs += [mspec(dk), mspec(dk), mspec(dk), mspec(dv)]
        args += [qm, km, gm, vm]
        out_specs.append(pl.BlockSpec((n_meta, h * dv), lambda b, i: (0, 0)))
        out_shape.append(jax.ShapeDtypeStruct((n_meta, h * dv), F32))
    return pl.pallas_call(
        functools.partial(_hgrn_rec_body, c=c, r=r, has_meta=meta is not None),
        grid=(n_seq, nblk),
        in_specs=in_specs,
        out_specs=out_specs,
        out_shape=out_shape,
        scratch_shapes=[pltpu.VMEM((h, dv, dk), F32)],
        compiler_params=_params(2),
        name="hgrn_rec",
    )(*args)


def _mix_ffn_body(*refs, head_dim, dff, fc):
    if head_dim:
        (o_ref, gate_ref, x_ref, gn_ref, wo_ref, pm_ref,
         nf_ref, wgu_ref, wd_ref, pf_ref, y_ref) = refs
        o = o_ref[...]
        parts = []
        for h in range(o.shape[1] // head_dim):
            oh = o[:, h * head_dim:(h + 1) * head_dim]
            parts.append(oh * lax.rsqrt(jnp.mean(oh * oh, axis=-1, keepdims=True) + EPS))
        o = (jnp.concatenate(parts, axis=-1) * gn_ref[...] * gate_ref[...]).astype(BF16)
    else:
        o_ref, x_ref, wo_ref, pm_ref, nf_ref, wgu_ref, wd_ref, pf_ref, y_ref = refs
        o = o_ref[...]
    x = x_ref[...] + _rmsnorm(_mm(o, wo_ref[...]), pm_ref[...])
    xn = _rmsnorm(x, nf_ref[...]).astype(BF16)
    acc = jnp.zeros(x.shape, F32)
    for c0 in range(0, dff, fc):
        a = _mm(xn, wgu_ref[:, c0:c0 + fc])
        u = _mm(xn, wgu_ref[:, dff + c0:dff + c0 + fc])
        hid = (a * _sigmoid(a) * u).astype(BF16)
        acc = acc + _mm(hid, wd_ref[c0:c0 + fc, :])
    y_ref[...] = x + _rmsnorm(acc, pf_ref[...])


def _mix_ffn(o, gate, x, gn, wo_bf, pm, nf, wgu_bf, wd_bf, pf, tm, head_dim):
    n, d = x.shape
    dff = wd_bf.shape[0]
    fc = dff // 2 if (dff // 2) % LANES == 0 else dff
    row = pl.BlockSpec((tm, d), lambda i: (i, 0))
    vec = _resident((1, d))
    ffn_specs = [vec, _resident(wgu_bf.shape), _resident(wd_bf.shape), vec]
    if head_dim:
        in_specs = [row, row, row, vec, _resident(wo_bf.shape), vec] + ffn_specs
        args = (o, gate, x, gn, wo_bf, pm, nf, wgu_bf, wd_bf, pf)
    else:
        in_specs = [row, row, _resident(wo_bf.shape), vec] + ffn_specs
        args = (o, x, wo_bf, pm, nf, wgu_bf, wd_bf, pf)
    return pl.pallas_call(
        functools.partial(_mix_ffn_body, head_dim=head_dim, dff=dff, fc=fc),
        grid=(n // tm,),
        in_specs=in_specs,
        out_specs=row,
        out_shape=jax.ShapeDtypeStruct((n, d), F32),
        compiler_params=_params(1),
        name="mix_ffn",
    )(*args)


def _fox_proj_body(x_ref, nw_ref, w_ref, wf_ref, bf_ref, k_ref, v_ref, lf_ref,
                   qb_ref, kb_ref, vb_ref, *, scale):
    d = k_ref.shape[1]
    nh = lf_ref.shape[1]
    xn = _rmsnorm(x_ref[...], nw_ref[...]).astype(BF16)
    qb_ref[...] = (_mm(xn, w_ref[:, 0:d]) * scale).astype(BF16)
    k = _mm(xn, w_ref[:, d:2 * d])
    k_ref[...] = k
    kb_ref[...] = k.astype(BF16)
    v = _mm(xn, w_ref[:, 2 * d:3 * d])
    v_ref[...] = v
    vb_ref[...] = v.astype(BF16)
    lf = _log_sigmoid(_mm(xn, wf_ref[...]) + bf_ref[...])
    lf_ref[...] = lf[:, 0:nh]


def _fox_proj(x, nw, w_bf, wf_bf, bf_row, nh, tm):
    n, d = x.shape
    row = lambda w: pl.BlockSpec((tm, w), lambda i: (i, 0))
    scale = float(d // nh) ** -0.5
    return pl.pallas_call(
        functools.partial(_fox_proj_body, scale=scale),
        grid=(n // tm,),
        in_specs=[row(d), _resident((1, d)), _resident(w_bf.shape), _resident(wf_bf.shape),
                  _resident(bf_row.shape)],
        out_specs=[row(d), row(d), row(nh), row(d), row(d), row(d)],
        out_shape=[jax.ShapeDtypeStruct((n, d), F32), jax.ShapeDtypeStruct((n, d), F32),
                   jax.ShapeDtypeStruct((n, nh), F32), jax.ShapeDtypeStruct((n, d), BF16),
                   jax.ShapeDtypeStruct((n, d), BF16), jax.ShapeDtypeStruct((n, d), BF16)],
        compiler_params=_params(1),
        name="fox_proj",
    )(x, nw, w_bf, wf_bf, bf_row)


def _head_sq_norms(xt, nh):
    dh = xt.shape[0] // nh
    x2 = xt * xt
    return jnp.concatenate([jnp.sum(x2[h * dh:(h + 1) * dh], axis=0, keepdims=True)
                            for h in range(nh)], axis=0)


def _fox_proj_t_body(x_ref, nw_ref, wt_ref, wf_ref, bf_ref, kt_ref, vt_ref, lf_ref,
                     qtb_ref, kb_ref, vtb_ref, qsq_ref, ksq_ref, *, qscale):
    d = kt_ref.shape[0]
    nh = lf_ref.shape[1]
    xn = _rmsnorm(x_ref[...], nw_ref[...]).astype(BF16)
    qtb = (_mm_nt(wt_ref[0:d, :], xn) * qscale).astype(BF16)
    qtb_ref[...] = qtb
    qsq_ref[...] = _head_sq_norms(qtb.astype(F32), nh)
    kt = _mm_nt(wt_ref[d:2 * d, :], xn)
    kt_ref[...] = kt
    ksq_ref[...] = _head_sq_norms(kt.astype(BF16).astype(F32), nh)
    kb_ref[...] = _mm_nt(xn, wt_ref[d:2 * d, :]).astype(BF16)
    vt = _mm_nt(wt_ref[2 * d:3 * d, :], xn)
    vt_ref[...] = vt
    vtb_ref[...] = vt.astype(BF16)
    lf = _log_sigmoid(_mm(xn, wf_ref[...]) + bf_ref[...])
    lf_ref[...] = lf[:, 0:nh]


def _fox_proj_t(x, nw, wt_bf, wf_bf, bf_row, nh, tm, qscale):
    n, d = x.shape
    row = lambda w: pl.BlockSpec((tm, w), lambda i: (i, 0))
    col = pl.BlockSpec((d, tm), lambda i: (0, i))
    sq = pl.BlockSpec((nh, tm), lambda i: (0, i))
    return pl.pallas_call(
        functools.partial(_fox_proj_t_body, qscale=qscale),
        grid=(n // tm,),
        in_specs=[row(d), _resident((1, d)), _resident(wt_bf.shape), _resident(wf_bf.shape),
                  _resident(bf_row.shape)],
        out_specs=[col, col, row(nh), col, row(d), col, sq, sq],
        out_shape=[jax.ShapeDtypeStruct((d, n), F32), jax.ShapeDtypeStruct((d, n), F32),
                   jax.ShapeDtypeStruct((n, nh), F32), jax.ShapeDtypeStruct((d, n), BF16),
                   jax.ShapeDtypeStruct((n, d), BF16), jax.ShapeDtypeStruct((d, n), BF16),
                   jax.ShapeDtypeStruct((nh, n), F32), jax.ShapeDtypeStruct((nh, n), F32)],
        compiler_params=_params(1),
        name="fox_proj_t",
    )(x, nw, wt_bf, wf_bf, bf_row)


BIAS_LANES_PER_HEAD = 6


def _split3(v):
    hi = v.astype(BF16).astype(F32)
    r1 = v - hi
    mid = r1.astype(BF16).astype(F32)
    lo = (r1 - mid).astype(BF16).astype(F32)
    return hi, mid, lo


def _fox_kbias_body(c4_ref, aux_ref, *, cscale, nh):
    tm = c4_ref.shape[0]
    npair = aux_ref.shape[0]
    hi, mid, lo = _split3(c4_ref[...] * (-cscale))
    group = lax.broadcasted_iota(jnp.int32, (tm, 4 * nh), 1) // nh
    src = jnp.where(group == 0, hi, jnp.where(group == 1, mid, jnp.where(group == 2, lo, 1.0)))
    row = lax.broadcasted_iota(jnp.int32, (4 * nh, npair * LANES), 0)
    col = lax.broadcasted_iota(jnp.int32, (4 * nh, npair * LANES), 1)
    lane = col % LANES
    slot = lane % BIAS_LANES_PER_HEAD
    head = 2 * (col // LANES) + lane // BIAS_LANES_PER_HEAD
    pick = (lane < 2 * BIAS_LANES_PER_HEAD) & (row == jnp.minimum(slot, 3) * nh + head)
    out = _mm(src.astype(BF16), jnp.where(pick, 1.0, 0.0).astype(BF16))
    for p in range(npair):
        aux_ref[p] = out[:, p * LANES:(p + 1) * LANES].astype(BF16)


def _fox_kbias(c_rows, npair, tm, cscale):
    n, nh = c_rows.shape
    return pl.pallas_call(
        functools.partial(_fox_kbias_body, cscale=cscale, nh=nh),
        grid=(n // tm,),
        in_specs=[pl.BlockSpec((tm, 4 * nh), lambda i: (i, 0))],
        out_specs=pl.BlockSpec((npair, tm, LANES), lambda i: (0, i, 0)),
        out_shape=jax.ShapeDtypeStruct((npair, n, LANES), BF16),
        compiler_params=_params(1),
        name="fox_kbias",
    )(jnp.tile(c_rows, (1, 4)))


def _lane_cumsum_body(x_ref, y_ref, *, reverse_exclusive):
    rows, n = x_ref.shape
    nb = n // LANES
    t_i = lax.broadcasted_iota(jnp.int32, (LANES, LANES), 0)
    k_i = lax.broadcasted_iota(jnp.int32, (LANES, LANES), 1)
    sel = (t_i > k_i) if reverse_exclusive else (t_i <= k_i)
    tri = jnp.where(sel, 1.0, 0.0).astype(F32)

    def body(i, carry):
        b = (nb - 1 - i) if reverse_exclusive else i
        sl = slice(b * LANES, (b + 1) * LANES)
        xb = x_ref[:, sl]
        y_ref[:, sl] = jnp.dot(xb, tri, precision=HIGHEST, preferred_element_type=F32) + carry
        return carry + jnp.sum(xb, axis=-1, keepdims=True)

    carry = jnp.zeros((rows, 1), F32)
    for i in range(nb):
        carry = body(i, carry)


def _lane_cumsum(x, reverse_exclusive, tr):
    rows, n = x.shape
    spec = pl.BlockSpec((tr, n), lambda i: (i, 0))
    return pl.pallas_call(
        functools.partial(_lane_cumsum_body, reverse_exclusive=reverse_exclusive),
        grid=(rows // tr,),
        in_specs=[spec],
        out_specs=spec,
        out_shape=jax.ShapeDtypeStruct((rows, n), F32),
        compiler_params=_params(1),
        name="lane_cumsum",
    )(x)


ATTN_TILE = 512
ATTN_PAIRS_PER_STEP = 2
ACC_PAD = 16


def _fox_attn_body(jlo_ref, c_ref, qt_ref, k_ref, ka_ref, vt_ref, kme_ref, vtm_ref,
                   qm_ref, km_ref, vm_ref, o_ref, om_ref, qx_s, m_s, acc_s,
                   *, tq, tk, dh, n_meta, cscale, npg):
    i = pl.program_id(1)
    heads = range(2 * npg)
    pair_lanes = lambda x, pp: x[:, pp * 2 * dh:(pp + 1) * 2 * dh]
    pair_rows = lambda x, pp: x[pp * 2 * dh:(pp + 1) * 2 * dh, :]
    lane = lax.broadcasted_iota(jnp.int32, (1, 2 * dh), 1)
    head_sel = [lane < dh, lane >= dh]

    @pl.when(i == 0)
    def _():
        r_i = lax.broadcasted_iota(jnp.int32, (n_meta, n_meta), 0)
        c_i = lax.broadcasted_iota(jnp.int32, (n_meta, n_meta), 1)
        outs = []
        for pp in range(npg):
            qm, km, vm = (pair_lanes(r[...], pp) for r in (qm_ref, km_ref, vm_ref))
            out = jnp.zeros((n_meta, 2 * dh), F32)
            for a in range(2):
                qa = jnp.where(head_sel[a], qm, jnp.zeros_like(qm))
                x = _mm_nt(qa, km) - c_ref[2 * pp + a, :, 0:n_meta]
                x = jnp.where(c_i <= r_i, x, NEG)
                p = jnp.exp(x - jnp.max(x, axis=-1, keepdims=True))
                oa = _mm(p.astype(BF16), vm) / jnp.sum(p, axis=-1, keepdims=True)
                out = jnp.where(head_sel[a], oa, out)
            outs.append(out)
        om_ref[...] = jnp.concatenate(outs, axis=1).astype(BF16)

    q0 = pl.multiple_of(LANES + i * tq, LANES)
    row = lax.broadcasted_iota(jnp.int32, (2 * dh, tq), 0)
    for hh in heads:
        pp, a = divmod(hh, 2)
        qt = pair_rows(qt_ref[...], pp)
        hi, mid, lo = _split3(c_ref[hh, :, pl.ds(q0, tq)] * cscale)
        slot = row - a * BIAS_LANES_PER_HEAD
        bias = jnp.where(slot == 3, hi, jnp.where(slot == 4, mid, jnp.where(slot == 5, lo, 1.0)))
        bias = jnp.where((slot >= 0) & (slot < BIAS_LANES_PER_HEAD), bias, 0.0)
        qx_s[hh, 0:2 * dh, :] = jnp.where((row >= a * dh) & (row < (a + 1) * dh), qt,
                                          jnp.zeros_like(qt))
        qx_s[hh, 2 * dh:4 * dh, :] = bias.astype(BF16)
        m_s[hh] = jnp.full((1, tq), NEG, F32)
        acc_s[hh] = jnp.zeros((dh + ACC_PAD, tq), F32)

    def scores(kxs):
        return [_mm(kxs[hh // 2], qx_s[hh]) for hh in heads]

    def update(sts, vts, masked):
        tkb = vts[0].shape[1]
        ones = jnp.where(lax.broadcasted_iota(jnp.int32, (ACC_PAD, tkb), 0) == 0,
                         1.0, 0.0).astype(BF16)
        for hh in heads:
            pp, a = divmod(hh, 2)
            st = sts[hh]
            if masked:
                k_i = lax.broadcasted_iota(jnp.int32, (tkb, tq), 0)
                q_i = lax.broadcasted_iota(jnp.int32, (tkb, tq), 1)
                st = jnp.where(k_i <= q_i, st, NEG)
            m_prev = m_s[hh]
            m_new = jnp.maximum(m_prev, jnp.max(st, axis=0, keepdims=True))
            p = jnp.exp2(st - m_new).astype(BF16)
            vx = jnp.concatenate([vts[pp][a * dh:(a + 1) * dh, :], ones], axis=0)
            acc_s[hh] = jnp.exp2(m_prev - m_new) * acc_s[hh] + _mm(vx, p)
            m_s[hh] = m_new

    def key_block(j):
        k0 = pl.multiple_of(j * tk, tk)
        kxs = [jnp.concatenate([pair_lanes(k_ref[pl.ds(k0, tk), :], pp),
                                ka_ref[pp, pl.ds(k0, tk), :]], axis=1) for pp in range(npg)]
        vts = [pair_rows(vt_ref[:, pl.ds(k0, tk)], pp) for pp in range(npg)]
        return kxs, vts

    def body(j, carry):
        kxs, vts = key_block(j)
        update(scores(kxs), vts, False)
        return carry

    lax.fori_loop(jlo_ref[pl.program_id(0), i], i, body, 0)

    kxs, vts = key_block(i)
    st_diag = scores(kxs)
    st_meta = scores([kme_ref[pp] for pp in range(npg)])
    update(st_diag, vts, True)
    update(st_meta, [pair_rows(vtm_ref[...], pp) for pp in range(npg)], False)

    outs = []
    for pp in range(npg):
        halves = []
        for a in range(2):
            acc = acc_s[2 * pp + a]
            halves.append(acc[0:dh] / acc[dh:dh + 1])
        outs.append(jnp.concatenate(halves, axis=0).T)
    o_ref[...] = jnp.concatenate(outs, axis=1).astype(BF16)


def _fox_attn(jlo, c3, qt_b, kb, kaux, vt_b, kx_meta, vt_meta, qb_aux, kb_aux, vb_aux,
              n_meta, meta_blk, dh, tq, cscale, npg):
    n, d = kb.shape
    gw = 2 * dh * npg
    ngroups = d // gw
    nl = c3.shape[2]
    mspec = pl.BlockSpec((n_meta, gw), lambda p, i, _: (meta_blk, p))
    once = dict(pipeline_mode=pl.Buffered(1))
    return pl.pallas_call(
        functools.partial(_fox_attn_body, tq=tq, tk=tq, dh=dh, n_meta=n_meta, cscale=cscale,
                          npg=npg),
        grid_spec=pltpu.PrefetchScalarGridSpec(
            num_scalar_prefetch=1,
            grid=(ngroups, n // tq),
            in_specs=[pl.BlockSpec((2 * npg, 1, nl), lambda p, i, _: (p, 0, 0)),
                      pl.BlockSpec((gw, tq), lambda p, i, _: (p, i)),
                      pl.BlockSpec((n, gw), lambda p, i, _: (0, p), **once),
                      pl.BlockSpec((npg, n, LANES), lambda p, i, _: (p, 0, 0), **once),
                      pl.BlockSpec((gw, n), lambda p, i, _: (p, 0), **once),
                      pl.BlockSpec((npg, n_meta, 4 * dh), lambda p, i, _: (p, 0, 0)),
                      pl.BlockSpec((gw, n_meta), lambda p, i, _: (p, 0)),
                      mspec, mspec, mspec],
            out_specs=[pl.BlockSpec((tq, gw), lambda p, i, _: (i, p)),
                       pl.BlockSpec((n_meta, gw), lambda p, i, _: (0, p))],
            scratch_shapes=[pltpu.VMEM((2 * npg, 4 * dh, tq), BF16),
                            pltpu.VMEM((2 * npg, 1, tq), F32),
                            pltpu.VMEM((2 * npg, dh + ACC_PAD, tq), F32)]),
        out_shape=[jax.ShapeDtypeStruct((n, d), BF16), jax.ShapeDtypeStruct((n_meta, d), BF16)],
        compiler_params=_params(2),
        name="fox_attn",
    )(jlo, c3, qt_b, kb, kaux, vt_b, kx_meta, vt_meta, qb_aux, kb_aux, vb_aux)


PRUNE_NATS = 37.0
NORM_SLACK = 1.01


def _fox_first_key_block(c_main, qsq, ksq, tq, group):
    nh, n = c_main.shape
    nq = n // tq
    u = jnp.sqrt(jnp.max(qsq, axis=1) * jnp.max(ksq, axis=1)) * (NORM_SLACK / LOG2E)
    c_first = c_main[:, 0::tq]
    c_last = c_main[:, tq - 1::tq]
    bound = 2.0 * u[:, None, None] + c_first[:, :, None] - c_last[:, None, :]
    j_lt_i = jnp.arange(nq)[None, :] < jnp.arange(nq)[:, None]
    skip = (bound <= -PRUNE_NATS) & j_lt_i[None]
    jlo = jnp.sum(skip, axis=2).astype(jnp.int32)
    return jnp.min(jlo.reshape(nh // group, group, nq), axis=1)


def _fox_sample_body(q_ref, kn_ref, vn_ref, cn_ref, kc_ref, vc_ref, suf_ref, o_ref,
                     m_s, l_s, acc_s, *, nh, dh, s_len):
    t = pl.program_id(1)

    @pl.when(t == 0)
    def _():
        r_i = lax.broadcasted_iota(jnp.int32, (s_len, s_len), 0)
        c_i = lax.broadcasted_iota(jnp.int32, (s_len, s_len), 1)
        cols = [slice(h * dh, (h + 1) * dh) for h in range(nh)]
        x = [_mm_nt(q_ref[:, cols[h]], kn_ref[:, cols[h]]) - cn_ref[0, h:h + 1, 0:s_len]
             for h in range(nh)]
        x = [jnp.where(c_i <= r_i, x[h], NEG) for h in range(nh)]
        m = [jnp.max(x[h], axis=-1, keepdims=True) for h in range(nh)]
        p = [jnp.exp(x[h] - m[h]) for h in range(nh)]
        pv = [_mm(p[h].astype(BF16), vn_ref[:, cols[h]]) for h in range(nh)]
        for h in range(nh):
            m_s[h] = m[h]
            l_s[h] = jnp.sum(p[h], axis=-1, keepdims=True)
            acc_s[h] = pv[h]

    heads = range(nh)
    x = [_mm(q_ref[:, h * dh:(h + 1) * dh], kc_ref[0, h].astype(BF16)) + suf_ref[0, h:h + 1, :]
         for h in heads]
    m_prev = [m_s[h] for h in heads]
    m_new = [jnp.maximum(m_prev[h], jnp.max(x[h], axis=-1, keepdims=True)) for h in heads]
    p = [jnp.exp(x[h] - m_new[h]) for h in heads]
    pv = [_mm_nt(p[h].astype(BF16), vc_ref[0, h].astype(BF16)) for h in heads]
    for h in heads:
        alpha = jnp.exp(m_prev[h] - m_new[h])
        l_s[h] = alpha * l_s[h] + jnp.sum(p[h], axis=-1, keepdims=True)
        acc_s[h] = alpha * acc_s[h] + pv[h]
        m_s[h] = m_new[h]

    @pl.when(t == pl.num_programs(1) - 1)
    def _():
        o_ref[...] = jnp.concatenate([acc_s[h] / l_s[h] for h in range(nh)], axis=-1).astype(BF16)


def _fox_sample(qb, kb, vb, cn, kc_t, vc_t, layer, suf, s_len, tk):
    _, nb, nh, dh, past = kc_t.shape
    d = nh * dh
    tok = pl.BlockSpec((s_len, d), lambda b, t: (b, 0))
    cache = pl.BlockSpec((None, 1, nh, dh, tk), lambda b, t: (layer, b, 0, 0, t))
    return pl.pallas_call(
        functools.partial(_fox_sample_body, nh=nh, dh=dh, s_len=s_len),
        grid=(nb, past // tk),
        in_specs=[tok, tok, tok, pl.BlockSpec((1, nh, LANES), lambda b, t: (b, 0, 0)),
                  cache, cache, pl.BlockSpec((1, nh, tk), lambda b, t: (b, 0, t))],
        out_specs=tok,
        out_shape=jax.ShapeDtypeStruct((nb * s_len, d), BF16),
        scratch_shapes=[pltpu.VMEM((nh, s_len, 1), F32), pltpu.VMEM((nh, s_len, 1), F32),
                        pltpu.VMEM((nh, s_len, dh), F32)],
        compiler_params=_params(2),
        name="fox_sample",
    )(qb, kb, vb, cn, kc_t, vc_t, suf)


def _largest_tile(n, cap):
    t = min(n, cap)
    while n % t:
        t //= 2
    return t


def kernel(x_prompt, x_sample, state_hgrn, cache_k, cache_v, cache_logf, meta_tokens,
           norm_pre_mix, norm_post_mix, norm_pre_ffn, norm_post_ffn,
           a_w_in, a_lb_logits, a_g_norm, a_w_out, b_w_in, b_f, b_w_out,
           ffn_w_gu, ffn_w_down):
    batch, seq, d = x_prompt.shape
    assert batch == 1, "one prompt stream per step"
    n_dec, s_len, _ = x_sample.shape
    n_meta = meta_tokens.shape[0]
    depth = norm_pre_mix.shape[0]
    _, _, a_heads, a_dk, a_dv = state_hgrn.shape
    _, _, past, b_heads, b_dh = cache_k.shape
    ns = n_dec * s_len
    assert s_len % HGRN_SUB == 0 and n_meta % HGRN_SUB == 0 and ns % n_meta == 0
    assert 2 * b_dh == LANES and a_dk == LANES and a_dv == LANES

    tm = _largest_tile(seq, 512)
    n_aux = ns + n_meta
    meta_blk = ns // n_meta
    rec_r = _largest_tile(seq, 512)
    rec_c = _largest_tile(rec_r, 64)
    tq = _largest_tile(seq, ATTN_TILE)
    tk_cache = _largest_tile(past, 1024)
    npair = b_heads // 2
    npg = ATTN_PAIRS_PER_STEP if npair % ATTN_PAIRS_PER_STEP == 0 else 1

    sm = jax.nn.softmax(a_lb_logits.astype(F32), axis=0)
    lb_all = jnp.cumsum(sm, axis=0) - sm[0]

    xm = x_prompt.reshape(seq, d)
    xa = jnp.concatenate([x_sample.reshape(ns, d), meta_tokens.astype(F32)], axis=0)
    row = lambda w: w.reshape(1, -1).astype(F32)
    cache_kt = jnp.transpose(cache_k, (0, 1, 3, 4, 2))
    cache_vt = jnp.transpose(cache_v, (0, 1, 3, 4, 2))
    cache_lt = jnp.swapaxes(cache_logf.astype(F32), 2, 3)

    st_p, st_s, k_p, v_p, lf_p, k_s, v_s, lf_s = [], [], [], [], [], [], [], []
    for i in range(depth):
        j = i // 2
        if i % 2 == 0:
            lb = lb_all[j]
            lbp = jnp.stack([jnp.log(jnp.maximum(lb, LB_FLOOR)), jnp.log1p(-lb), 1.0 - lb])
            w_in = a_w_in[j].astype(BF16)
            w_out = a_w_out[j].astype(BF16)
            qm, km, gm, vm, gtm = _hgrn_proj(xm, row(norm_pre_mix[i]), w_in, lbp, tm)
            qa, ka, ga, va, gta = _hgrn_proj(xa, row(norm_pre_mix[i]), w_in, lbp, n_aux)
            o_s, s_s = _hgrn_rec(qa, ka, ga, va, state_hgrn[j].astype(F32),
                                 n_seq=n_dec, seq_len=s_len, r=s_len, c=s_len)
            o_m, s_p, o_meta = _hgrn_rec(qm, km, gm, vm,
                                         jnp.zeros((1, a_heads, a_dk, a_dv), F32),
                                         n_seq=1, seq_len=seq, r=rec_r, c=rec_c,
                                         meta=(qa, ka, ga, va, n_meta, meta_blk))
            st_p.append(s_p.astype(state_hgrn.dtype))
            st_s.append(s_s.astype(state_hgrn.dtype))
            o_a = jnp.concatenate([o_s, o_meta], axis=0)
            mix_m = (o_m, gtm, xm, row(a_g_norm[j]))
            mix_a = (o_a, gta, xa, row(a_g_norm[j]))
            head_dim = a_dv
        else:
            w_t = jnp.swapaxes(b_w_in[j], 0, 1)
            wt_qkv = w_t[:3 * d].astype(BF16)
            w_in = jnp.swapaxes(wt_qkv, 0, 1)
            w_f = jnp.pad(b_w_in[j][:, 3 * d:], ((0, 0), (0, LANES - b_heads))).astype(BF16)
            bf_row = jnp.pad(b_f[j].astype(F32), (0, LANES - b_heads)).reshape(1, LANES)
            w_out = b_w_out[j].astype(BF16)
            scale = float(b_dh) ** -0.5
            kt_m, vt_m, lf_m, qtb_m, kb_m, vtb_m, qsq, ksq = _fox_proj_t(
                xm, row(norm_pre_mix[i]), wt_qkv, w_f, bf_row, b_heads, tm, scale * LOG2E)
            kf_a, vf_a, lf_a, qb_a, kb_a, vb_a = _fox_proj(
                xa, row(norm_pre_mix[i]), w_in, w_f, bf_row, b_heads, n_aux)
            lf_t = jnp.concatenate(
                [jnp.pad(lf_a[ns:].T, ((0, 0), (0, LANES - n_meta))), lf_m.T], axis=1)
            c_all = _lane_cumsum(lf_t, False, b_heads)
            kaux = _fox_kbias(c_all[:, LANES:].T, npair, tm, LOG2E)
            kaux_meta = _fox_kbias(c_all[:, :n_meta].T, npair, n_meta, LOG2E)
            k_meta = jnp.swapaxes(kb_a[ns:].reshape(n_meta, npair, LANES), 0, 1)
            jlo = _fox_first_key_block(c_all[:, LANES:], qsq, ksq, tq, 2 * npg)
            o_m, o_meta = _fox_attn(jlo, c_all.reshape(b_heads, 1, -1), qtb_m, kb_m, kaux, vtb_m,
                                    jnp.concatenate([k_meta, kaux_meta], axis=-1), vb_a[ns:].T,
                                    qb_a, kb_a, vb_a, n_meta, meta_blk, b_dh, tq, LOG2E, npg)
            cl_t = cache_lt[j].reshape(n_dec * b_heads, past)
            tr = _largest_tile(n_dec * b_heads, LANES)
            suf = _lane_cumsum(cl_t, True, tr).reshape(n_dec, b_heads, past)
            ln_t = jnp.swapaxes(lf_a[:ns].reshape(n_dec, s_len, b_heads), 1, 2)
            ln_t = jnp.pad(ln_t, ((0, 0), (0, 0), (0, LANES - s_len)))
            cn = _lane_cumsum(ln_t.reshape(n_dec * b_heads, LANES), False, tr)
            o_s = _fox_sample(qb_a, kb_a, vb_a, cn.reshape(n_dec, b_heads, LANES),
                              cache_kt, cache_vt, j, suf, s_len, tk_cache)
            o_a = jnp.concatenate([o_s, o_meta], axis=0)
            mix_m = (o_m, None, xm, None)
            mix_a = (o_a, None, xa, None)
            head_dim = 0
            shp = lambda t, n: t.reshape(-1, n, b_heads, b_dh)
            k_p.append((kf_a[ns:].T, kt_m))
            v_p.append((vf_a[ns:].T, vt_m))
            lf_p.append(jnp.concatenate([lf_a[ns:], lf_m], axis=0).reshape(1, n_meta + seq, b_heads))
            k_s.append(shp(kf_a[:ns], s_len))
            v_s.append(shp(vf_a[:ns], s_len))
            lf_s.append(lf_a[:ns].reshape(n_dec, s_len, b_heads))
        ffn_w = (w_out, row(norm_post_mix[i]), row(norm_pre_ffn[i]), ffn_w_gu[i].astype(BF16),
                 ffn_w_down[i].astype(BF16), row(norm_post_ffn[i]))
        xm = _mix_ffn(*mix_m, *ffn_w, tm, head_dim)
        xa = _mix_ffn(*mix_a, *ffn_w, n_aux, head_dim)

    def prompt_kv(parts):
        t = jnp.concatenate([jnp.stack([m for m, _ in parts]), jnp.stack([x for _, x in parts])],
                            axis=2)
        t = t.reshape(len(parts), b_heads, b_dh, n_meta + seq)
        return jnp.transpose(t, (0, 3, 1, 2))[:, None]

    return (xm.reshape(1, seq, d), xa[:ns].reshape(n_dec, s_len, d),
            jnp.stack(st_p), prompt_kv(k_p), prompt_kv(v_p), jnp.stack(lf_p),
            jnp.stack(st_s), jnp.stack(k_s), jnp.stack(v_s), jnp.stack(lf_s))
```

```python
import functools

import jax
import jax.numpy as jnp
from jax import lax
from jax.experimental import pallas as pl
from jax.experimental.pallas import tpu as pltpu

F32 = jnp.float32
BF16 = jnp.bfloat16
EPS = 1e-6
NEG = -1e30
LB_FLOOR = 1e-30

LANES = 128
VMEM_LIMIT = 56 * 1024 * 1024
HGRN_SUB = 16
HIGHEST = lax.Precision.HIGHEST
LOG2E = 1.4426950408889634


def _params(n_grid):
    return pltpu.CompilerParams(dimension_semantics=("arbitrary",) * n_grid,
                                vmem_limit_bytes=VMEM_LIMIT)


def _resident(shape):
    nd = len(shape)
    return pl.BlockSpec(shape, lambda *_: (0,) * nd, pipeline_mode=pl.Buffered(1))


def _rmsnorm(x, w):
    return x * lax.rsqrt(jnp.mean(x * x, axis=-1, keepdims=True) + EPS) * w


def _sigmoid(x):
    return 1.0 / (1.0 + jnp.exp(-x))


def _log1p_exp(z):
    return jnp.log(1.0 + jnp.exp(z))


def _log_sigmoid(x):
    return jnp.minimum(x, 0.0) - _log1p_exp(-jnp.abs(x))


def _mm(a, b):
    return jnp.dot(a, b, preferred_element_type=F32)


def _mm_nt(a, b):
    return lax.dot_general(a, b, (((1,), (1,)), ((), ())), preferred_element_type=F32)


def _mm_tn(a, b):
    return lax.dot_general(a, b, (((0,), (0,)), ((), ())), preferred_element_type=F32)


def _hgrn_proj_body(x_ref, nw_ref, w_ref, lb_ref, q_ref, k_ref, g_ref, v_ref, gate_ref):
    f = q_ref.shape[1]
    d = v_ref.shape[1]
    xn = _rmsnorm(x_ref[...], nw_ref[...]).astype(BF16)
    q = _mm(xn, w_ref[:, 0:f])
    q_ref[...] = q * _sigmoid(q)
    fl = _mm(xn, w_ref[:, f:2 * f])
    e = jnp.exp(-jnp.abs(fl))
    r = 1.0 / (1.0 + e)
    a = lb_ref[0:1, :]
    b = lb_ref[1:2, :] + (jnp.minimum(fl, 0.0) - jnp.log(1.0 + e))
    g_ref[...] = jnp.maximum(a, b) + _log1p_exp(-jnp.abs(a - b))
    k_ref[...] = lb_ref[2:3, :] * jnp.where(fl >= 0.0, e * r, r)
    v_ref[...] = _mm(xn, w_ref[:, 2 * f:2 * f + d])
    gt = _mm(xn, w_ref[:, 2 * f + d:])
    gate_ref[...] = gt * _sigmoid(gt)


def _hgrn_proj(x, nw, w_bf, lbp, tm):
    n, d = x.shape
    f = lbp.shape[1]
    row = lambda w: pl.BlockSpec((tm, w), lambda i: (i, 0))
    return pl.pallas_call(
        _hgrn_proj_body,
        grid=(n // tm,),
        in_specs=[row(d), _resident((1, d)), _resident(w_bf.shape), _resident(lbp.shape)],
        out_specs=[row(f), row(f), row(f), row(d), row(d)],
        out_shape=[jax.ShapeDtypeStruct((n, w), F32) for w in (f, f, f, d, d)],
        compiler_params=_params(1),
        name="hgrn_proj",
    )(x, nw, w_bf, lbp)


def _hgrn_chunk(q, k, g, v, sts, c):
    nh = len(sts)
    sub, half = HGRN_SUB, HGRN_SUB // 2
    hs = lambda x, h: x[:, h * LANES:(h + 1) * LANES]
    r_i = lax.broadcasted_iota(jnp.int32, (c, c), 0)
    c_i = lax.broadcasted_iota(jnp.int32, (c, c), 1)
    tril = jnp.where(r_i >= c_i, 1.0, 0.0).astype(F32)
    gc = jnp.dot(tril, g, precision=HIGHEST, preferred_element_type=F32) * LOG2E
    g_last = gc[c - 1:c, :]
    qe = (q * jnp.exp2(gc)).astype(BF16)
    o_inter = [_mm_nt(hs(qe, h), sts[h].astype(BF16)) for h in range(nh)]
    kd = (k * jnp.exp2(g_last - gc)).astype(BF16)
    vb = v.astype(BF16)
    st_dec = jnp.exp2(g_last)
    st_new = [sts[h] * hs(st_dec, h) + _mm_tn(hs(vb, h), hs(kd, h)) for h in range(nh)]

    lk = jnp.log2(k) - gc
    half_row = lax.broadcasted_iota(jnp.int32, (half, 1), 0)
    out_rows = []
    for i in range(c // sub):
        r0 = i * sub
        gi, qi, vi, lki = gc[r0:r0 + sub], q[r0:r0 + sub], v[r0:r0 + sub], lk[r0:r0 + sub]
        o_lo = [o_inter[h][r0:r0 + half] for h in range(nh)]
        o_hi = [o_inter[h][r0 + half:r0 + sub] for h in range(nh)]
        if i > 0:
            g_first = gc[r0:r0 + 1]
            qs = (qi * jnp.exp2(gi - g_first)).astype(BF16)
            ks = (k[:r0] * jnp.exp2(g_first - gc[:r0])).astype(BF16)
            att = [_mm_nt(hs(qs, h), hs(ks, h)).astype(BF16) for h in range(nh)]
            off = [_mm(att[h], hs(vb[:r0], h)) for h in range(nh)]
            o_lo = [o_lo[h] + off[h][:half] for h in range(nh)]
            o_hi = [o_hi[h] + off[h][half:] for h in range(nh)]
        for rows, acc, s_range in ((slice(0, half), o_lo, range(half)),
                                   (slice(half, sub), o_hi, range(sub))):
            g_rows, q_rows = gi[rows], qi[rows]
            for s in s_range:
                diff = g_rows + lki[s:s + 1]
                if s >= rows.start:
                    diff = jnp.where(half_row >= s - rows.start, diff, NEG)
                prod = q_rows * jnp.exp2(diff)
                for h in range(nh):
                    col = jnp.sum(hs(prod, h), axis=-1, keepdims=True)
                    acc[h] = acc[h] + col * hs(vi, h)[s:s + 1]
        out_rows.append(jnp.concatenate(
            [jnp.concatenate([o_lo[h], o_hi[h]], axis=0) for h in range(nh)], axis=1))
    o = jnp.concatenate(out_rows, axis=0) if len(out_rows) > 1 else out_rows[0]
    return o, st_new


def _hgrn_rec_body(*refs, c, r, has_meta):
    if has_meta:
        (q_ref, k_ref, g_ref, v_ref, s0_ref, qm_ref, km_ref, gm_ref, vm_ref,
         o_ref, s_ref, om_ref, st_ref) = refs
    else:
        q_ref, k_ref, g_ref, v_ref, s0_ref, o_ref, s_ref, st_ref = refs
    blk = pl.program_id(1)
    nh = st_ref.shape[0]

    @pl.when(blk == 0)
    def _():
        sts = [s0_ref[0, h].T for h in range(nh)]
        if has_meta:
            o_m, sts = _hgrn_chunk(qm_ref[...], km_ref[...], gm_ref[...], vm_ref[...],
                                   sts, qm_ref.shape[0])
            om_ref[...] = o_m
        for h in range(nh):
            st_ref[h] = sts[h]

    def body(j, carry):
        sl = pl.ds(pl.multiple_of(j * c, c), c)
        o, sts = _hgrn_chunk(q_ref[sl, :], k_ref[sl, :], g_ref[sl, :], v_ref[sl, :],
                             [st_ref[h] for h in range(nh)], c)
        o_ref[sl, :] = o
        for h in range(nh):
            st_ref[h] = sts[h]
        return carry

    lax.fori_loop(0, r // c, body, 0)

    @pl.when(blk == pl.num_programs(1) - 1)
    def _():
        for h in range(nh):
            s_ref[0, h] = st_ref[h].T


def _hgrn_rec(q, k, g, v, s0, *, n_seq, seq_len, r, c, meta=None):
    _, h, dk, dv = s0.shape
    nblk = seq_len // r
    tok = lambda w: pl.BlockSpec((r, h * w), lambda b, i: (b * nblk + i, 0))
    st_spec = pl.BlockSpec((1, h, dk, dv), lambda b, i: (b, 0, 0, 0))
    in_specs = [tok(dk), tok(dk), tok(dk), tok(dv), st_spec]
    args = [q, k, g, v, s0]
    out_specs = [tok(dv), st_spec]
    out_shape = [jax.ShapeDtypeStruct((n_seq * seq_len, h * dv), F32),
                 jax.ShapeDtypeStruct(s0.shape, F32)]
    if meta is not None:
        qm, km, gm, vm, n_meta, meta_blk = meta
        mspec = lambda w: pl.BlockSpec((n_meta, h * w), lambda b, i: (meta_blk, 0))
        in_specs += [mspec(dk), mspec(dk), mspec(dk), mspec(dv)]
        args += [qm, km, gm, vm]
        out_specs.append(pl.BlockSpec((n_meta, h * dv), lambda b, i: (0, 0)))
        out_shape.append(jax.ShapeDtypeStruct((n_meta, h * dv), F32))
    return pl.pallas_call(
        functools.partial(_hgrn_rec_body, c=c, r=r, has_meta=meta is not None),
        grid=(n_seq, nblk),
        in_specs=in_specs,
        out_specs=out_specs,
        out_shape=out_shape,
        scratch_shapes=[pltpu.VMEM((h, dv, dk), F32)],
        compiler_params=_params(2),
        name="hgrn_rec",
    )(*args)


def _mix_ffn_body(*refs, head_dim, dff, fc):
    if head_dim:
        (o_ref, gate_ref, x_ref, gn_ref, wo_ref, pm_ref,
         nf_ref, wgu_ref, wd_ref, pf_ref, y_ref) = refs
        o = o_ref[...]
        parts = []
        for h in range(o.shape[1] // head_dim):
            oh = o[:, h * head_dim:(h + 1) * head_dim]
            parts.append(oh * lax.rsqrt(jnp.mean(oh * oh, axis=-1, keepdims=True) + EPS))
        o = (jnp.concatenate(parts, axis=-1) * gn_ref[...] * gate_ref[...]).astype(BF16)
    else:
        o_ref, x_ref, wo_ref, pm_ref, nf_ref, wgu_ref, wd_ref, pf_ref, y_ref = refs
        o = o_ref[...]
    x = x_ref[...] + _rmsnorm(_mm(o, wo_ref[...]), pm_ref[...])
    xn = _rmsnorm(x, nf_ref[...]).astype(BF16)
    acc = jnp.zeros(x.shape, F32)
    for c0 in range(0, dff, fc):
        a = _mm(xn, wgu_ref[:, c0:c0 + fc])
        u = _mm(xn, wgu_ref[:, dff + c0:dff + c0 + fc])
        hid = (a * _sigmoid(a) * u).astype(BF16)
        acc = acc + _mm(hid, wd_ref[c0:c0 + fc, :])
    y_ref[...] = x + _rmsnorm(acc, pf_ref[...])


def _mix_ffn(o, gate, x, gn, wo_bf, pm, nf, wgu_bf, wd_bf, pf, tm, head_dim):
    n, d = x.shape
    dff = wd_bf.shape[0]
    fc = dff // 2 if (dff // 2) % LANES == 0 else dff
    row = pl.BlockSpec((tm, d), lambda i: (i, 0))
    vec = _resident((1, d))
    ffn_specs = [vec, _resident(wgu_bf.shape), _resident(wd_bf.shape), vec]
    if head_dim:
        in_specs = [row, row, row, vec, _resident(wo_bf.shape), vec] + ffn_specs
        args = (o, gate, x, gn, wo_bf, pm, nf, wgu_bf, wd_bf, pf)
    else:
        in_specs = [row, row, _resident(wo_bf.shape), vec] + ffn_specs
        args = (o, x, wo_bf, pm, nf, wgu_bf, wd_bf, pf)
    return pl.pallas_call(
        functools.partial(_mix_ffn_body, head_dim=head_dim, dff=dff, fc=fc),
        grid=(n // tm,),
        in_specs=in_specs,
        out_specs=row,
        out_shape=jax.ShapeDtypeStruct((n, d), F32),
        compiler_params=_params(1),
        name="mix_ffn",
    )(*args)


def _fox_proj_body(x_ref, nw_ref, w_ref, wf_ref, bf_ref, k_ref, v_ref, lf_ref,
                   qb_ref, kb_ref, vb_ref, *, scale):
    d = k_ref.shape[1]
    nh = lf_ref.shape[1]
    xn = _rmsnorm(x_ref[...], nw_ref[...]).astype(BF16)
    qb_ref[...] = (_mm(xn, w_ref[:, 0:d]) * scale).astype(BF16)
    k = _mm(xn, w_ref[:, d:2 * d])
    k_ref[...] = k
    kb_ref[...] = k.astype(BF16)
    v = _mm(xn, w_ref[:, 2 * d:3 * d])
    v_ref[...] = v
    vb_ref[...] = v.astype(BF16)
    lf = _log_sigmoid(_mm(xn, wf_ref[...]) + bf_ref[...])
    lf_ref[...] = lf[:, 0:nh]


def _fox_proj(x, nw, w_bf, wf_bf, bf_row, nh, tm):
    n, d = x.shape
    row = lambda w: pl.BlockSpec((tm, w), lambda i: (i, 0))
    scale = float(d // nh) ** -0.5
    return pl.pallas_call(
        functools.partial(_fox_proj_body, scale=scale),
        grid=(n // tm,),
        in_specs=[row(d), _resident((1, d)), _resident(w_bf.shape), _resident(wf_bf.shape),
                  _resident(bf_row.shape)],
        out_specs=[row(d), row(d), row(nh), row(d), row(d), row(d)],
        out_shape=[jax.ShapeDtypeStruct((n, d), F32), jax.ShapeDtypeStruct((n, d), F32),
                   jax.ShapeDtypeStruct((n, nh), F32), jax.ShapeDtypeStruct((n, d), BF16),
                   jax.ShapeDtypeStruct((n, d), BF16), jax.ShapeDtypeStruct((n, d), BF16)],
        compiler_params=_params(1),
        name="fox_proj",
    )(x, nw, w_bf, wf_bf, bf_row)


def _head_sq_norms(xt, nh):
    dh = xt.shape[0] // nh
    x2 = xt * xt
    return jnp.concatenate([jnp.sum(x2[h * dh:(h + 1) * dh], axis=0, keepdims=True)
                            for h in range(nh)], axis=0)


def _fox_proj_t_body(x_ref, nw_ref, wt_ref, wf_ref, bf_ref, kt_ref, vt_ref, lf_ref,
                     qtb_ref, kb_ref, vtb_ref, qsq_ref, ksq_ref, *, qscale):
    d = kt_ref.shape[0]
    nh = lf_ref.shape[1]
    xn = _rmsnorm(x_ref[...], nw_ref[...]).astype(BF16)
    qtb = (_mm_nt(wt_ref[0:d, :], xn) * qscale).astype(BF16)
    qtb_ref[...] = qtb
    qsq_ref[...] = _head_sq_norms(qtb.astype(F32), nh)
    kt = _mm_nt(wt_ref[d:2 * d, :], xn)
    kt_ref[...] = kt
    ksq_ref[...] = _head_sq_norms(kt.astype(BF16).astype(F32), nh)
    kb_ref[...] = _mm_nt(xn, wt_ref[d:2 * d, :]).astype(BF16)
    vt = _mm_nt(wt_ref[2 * d:3 * d, :], xn)
    vt_ref[...] = vt
    vtb_ref[...] = vt.astype(BF16)
    lf = _log_sigmoid(_mm(xn, wf_ref[...]) + bf_ref[...])
    lf_ref[...] = lf[:, 0:nh]


def _fox_proj_t(x, nw, wt_bf, wf_bf, bf_row, nh, tm, qscale):
    n, d = x.shape
    row = lambda w: pl.BlockSpec((tm, w), lambda i: (i, 0))
    col = pl.BlockSpec((d, tm), lambda i: (0, i))
    sq = pl.BlockSpec((nh, tm), lambda i: (0, i))
    return pl.pallas_call(
        functools.partial(_fox_proj_t_body, qscale=qscale),
        grid=(n // tm,),
        in_specs=[row(d), _resident((1, d)), _resident(wt_bf.shape), _resident(wf_bf.shape),
                  _resident(bf_row.shape)],
        out_specs=[col, col, row(nh), col, row(d), col, sq, sq],
        out_shape=[jax.ShapeDtypeStruct((d, n), F32), jax.ShapeDtypeStruct((d, n), F32),
                   jax.ShapeDtypeStruct((n, nh), F32), jax.ShapeDtypeStruct((d, n), BF16),
                   jax.ShapeDtypeStruct((n, d), BF16), jax.ShapeDtypeStruct((d, n), BF16),
                   jax.ShapeDtypeStruct((nh, n), F32), jax.ShapeDtypeStruct((nh, n), F32)],
        compiler_params=_params(1),
        name="fox_proj_t",
    )(x, nw, wt_bf, wf_bf, bf_row)


BIAS_LANES_PER_HEAD = 6


def _split3(v):
    hi = v.astype(BF16).astype(F32)
    r1 = v - hi
    mid = r1.astype(BF16).astype(F32)
    lo = (r1 - mid).astype(BF16).astype(F32)
    return hi, mid, lo


def _fox_kbias_body(c4_ref, aux_ref, *, cscale, nh):
    tm = c4_ref.shape[0]
    npair = aux_ref.shape[0]
    hi, mid, lo = _split3(c4_ref[...] * (-cscale))
    group = lax.broadcasted_iota(jnp.int32, (tm, 4 * nh), 1) // nh
    src = jnp.where(group == 0, hi, jnp.where(group == 1, mid, jnp.where(group == 2, lo, 1.0)))
    row = lax.broadcasted_iota(jnp.int32, (4 * nh, npair * LANES), 0)
    col = lax.broadcasted_iota(jnp.int32, (4 * nh, npair * LANES), 1)
    lane = col % LANES
    slot = lane % BIAS_LANES_PER_HEAD
    head = 2 * (col // LANES) + lane // BIAS_LANES_PER_HEAD
    pick = (lane < 2 * BIAS_LANES_PER_HEAD) & (row == jnp.minimum(slot, 3) * nh + head)
    out = _mm(src.astype(BF16), jnp.where(pick, 1.0, 0.0).astype(BF16))
    for p in range(npair):
        aux_ref[p] = out[:, p * LANES:(p + 1) * LANES].astype(BF16)


def _fox_kbias(c_rows, npair, tm, cscale):
    n, nh = c_rows.shape
    return pl.pallas_call(
        functools.partial(_fox_kbias_body, cscale=cscale, nh=nh),
        grid=(n // tm,),
        in_specs=[pl.BlockSpec((tm, 4 * nh), lambda i: (i, 0))],
        out_specs=pl.BlockSpec((npair, tm, LANES), lambda i: (0, i, 0)),
        out_shape=jax.ShapeDtypeStruct((npair, n, LANES), BF16),
        compiler_params=_params(1),
        name="fox_kbias",
    )(jnp.tile(c_rows, (1, 4)))


def _lane_cumsum_body(x_ref, y_ref, *, reverse_exclusive):
    rows, n = x_ref.shape
    nb = n // LANES
    t_i = lax.broadcasted_iota(jnp.int32, (LANES, LANES), 0)
    k_i = lax.broadcasted_iota(jnp.int32, (LANES, LANES), 1)
    sel = (t_i > k_i) if reverse_exclusive else (t_i <= k_i)
    tri = jnp.where(sel, 1.0, 0.0).astype(F32)

    def body(i, carry):
        b = (nb - 1 - i) if reverse_exclusive else i
        sl = slice(b * LANES, (b + 1) * LANES)
        xb = x_ref[:, sl]
        y_ref[:, sl] = jnp.dot(xb, tri, precision=HIGHEST, preferred_element_type=F32) + carry
        return carry + jnp.sum(xb, axis=-1, keepdims=True)

    carry = jnp.zeros((rows, 1), F32)
    for i in range(nb):
        carry = body(i, carry)


def _lane_cumsum(x, reverse_exclusive, tr):
    rows, n = x.shape
    spec = pl.BlockSpec((tr, n), lambda i: (i, 0))
    return pl.pallas_call(
        functools.partial(_lane_cumsum_body, reverse_exclusive=reverse_exclusive),
        grid=(rows // tr,),
        in_specs=[spec],
        out_specs=spec,
        out_shape=jax.ShapeDtypeStruct((rows, n), F32),
        compiler_params=_params(1),
        name="lane_cumsum",
    )(x)


ATTN_TILE = 512
ATTN_PAIRS_PER_STEP = 2
ACC_PAD = 16


def _fox_attn_body(jlo_ref, fixed_ref, c_ref, sh_ref, qt_ref, k_ref, ka_ref, vt_ref, kme_ref,
                   vtm_ref, qm_ref, km_ref, vm_ref, o_ref, om_ref, qx_s, m_s, acc_s,
                   *, tq, tk, dh, n_meta, cscale, npg):
    i = pl.program_id(1)
    heads = range(2 * npg)
    pair_lanes = lambda x, pp: x[:, pp * 2 * dh:(pp + 1) * 2 * dh]
    pair_rows = lambda x, pp: x[pp * 2 * dh:(pp + 1) * 2 * dh, :]
    lane = lax.broadcasted_iota(jnp.int32, (1, 2 * dh), 1)
    head_sel = [lane < dh, lane >= dh]

    @pl.when(i == 0)
    def _():
        r_i = lax.broadcasted_iota(jnp.int32, (n_meta, n_meta), 0)
        c_i = lax.broadcasted_iota(jnp.int32, (n_meta, n_meta), 1)
        outs = []
        for pp in range(npg):
            qm, km, vm = (pair_lanes(r[...], pp) for r in (qm_ref, km_ref, vm_ref))
            out = jnp.zeros((n_meta, 2 * dh), F32)
            for a in range(2):
                qa = jnp.where(head_sel[a], qm, jnp.zeros_like(qm))
                x = _mm_nt(qa, km) - c_ref[2 * pp + a, :, 0:n_meta]
                x = jnp.where(c_i <= r_i, x, NEG)
                p = jnp.exp(x - jnp.max(x, axis=-1, keepdims=True))
                oa = _mm(p.astype(BF16), vm) / jnp.sum(p, axis=-1, keepdims=True)
                out = jnp.where(head_sel[a], oa, out)
            outs.append(out)
        om_ref[...] = jnp.concatenate(outs, axis=1).astype(BF16)

    q0 = pl.multiple_of(LANES + i * tq, LANES)
    row = lax.broadcasted_iota(jnp.int32, (2 * dh, tq), 0)
    for hh in heads:
        pp, a = divmod(hh, 2)
        qt = pair_rows(qt_ref[...], pp)
        hi, mid, lo = _split3(c_ref[hh, :, pl.ds(q0, tq)] * cscale - sh_ref[hh, :, 0:1])
        slot = row - a * BIAS_LANES_PER_HEAD
        bias = jnp.where(slot == 3, hi, jnp.where(slot == 4, mid, jnp.where(slot == 5, lo, 1.0)))
        bias = jnp.where((slot >= 0) & (slot < BIAS_LANES_PER_HEAD), bias, 0.0)
        qx_s[hh, 0:2 * dh, :] = jnp.where((row >= a * dh) & (row < (a + 1) * dh), qt,
                                          jnp.zeros_like(qt))
        qx_s[hh, 2 * dh:4 * dh, :] = bias.astype(BF16)
        m_s[hh] = jnp.full((1, tq), NEG, F32)
        acc_s[hh] = jnp.zeros((dh + ACC_PAD, tq), F32)

    def scores(kxs):
        return [_mm(kxs[hh // 2], qx_s[hh]) for hh in heads]

    def update(sts, vts, masked, fixed):
        tkb = vts[0].shape[1]
        ones = jnp.where(lax.broadcasted_iota(jnp.int32, (ACC_PAD, tkb), 0) == 0,
                         1.0, 0.0).astype(BF16)
        for hh in heads:
            pp, a = divmod(hh, 2)
            st = sts[hh]
            if masked:
                k_i = lax.broadcasted_iota(jnp.int32, (tkb, tq), 0)
                q_i = lax.broadcasted_iota(jnp.int32, (tkb, tq), 1)
                st = jnp.where(k_i <= q_i, st, NEG)
            vx = jnp.concatenate([vts[pp][a * dh:(a + 1) * dh, :], ones], axis=0)
            if fixed:
                acc_s[hh] = acc_s[hh] + _mm(vx, jnp.exp2(st).astype(BF16))
            else:
                m_prev = m_s[hh]
                m_new = jnp.maximum(m_prev, jnp.max(st, axis=0, keepdims=True))
                p = jnp.exp2(st - m_new).astype(BF16)
                acc_s[hh] = jnp.exp2(m_prev - m_new) * acc_s[hh] + _mm(vx, p)
                m_s[hh] = m_new

    def key_block(j):
        k0 = pl.multiple_of(j * tk, tk)
        kxs = [jnp.concatenate([pair_lanes(k_ref[pl.ds(k0, tk), :], pp),
                                ka_ref[pp, pl.ds(k0, tk), :]], axis=1) for pp in range(npg)]
        vts = [pair_rows(vt_ref[:, pl.ds(k0, tk)], pp) for pp in range(npg)]
        return kxs, vts

    def run(fixed):
        def body(j, carry):
            kxs, vts = key_block(j)
            update(scores(kxs), vts, False, fixed)
            return carry

        lax.fori_loop(jlo_ref[pl.program_id(0), i], i, body, 0)

        kxs, vts = key_block(i)
        st_diag = scores(kxs)
        st_meta = scores([kme_ref[pp] for pp in range(npg)])
        update(st_diag, vts, True, fixed)
        update(st_meta, [pair_rows(vtm_ref[...], pp) for pp in range(npg)], False, fixed)

    is_fixed = fixed_ref[pl.program_id(0)] != 0
    pl.when(is_fixed)(lambda: run(True))
    pl.when(jnp.logical_not(is_fixed))(lambda: run(False))

    outs = []
    for pp in range(npg):
        halves = []
        for a in range(2):
            acc = acc_s[2 * pp + a]
            halves.append(acc[0:dh] / acc[dh:dh + 1])
        outs.append(jnp.concatenate(halves, axis=0).T)
    o_ref[...] = jnp.concatenate(outs, axis=1).astype(BF16)


def _fox_attn(jlo, fixed, shift3, c3, qt_b, kb, kaux, vt_b, kx_meta, vt_meta, qb_aux, kb_aux,
              vb_aux, n_meta, meta_blk, dh, tq, cscale, npg):
    n, d = kb.shape
    gw = 2 * dh * npg
    ngroups = d // gw
    nl = c3.shape[2]
    mspec = pl.BlockSpec((n_meta, gw), lambda p, i, *_: (meta_blk, p))
    once = dict(pipeline_mode=pl.Buffered(1))
    return pl.pallas_call(
        functools.partial(_fox_attn_body, tq=tq, tk=tq, dh=dh, n_meta=n_meta, cscale=cscale,
                          npg=npg),
        grid_spec=pltpu.PrefetchScalarGridSpec(
            num_scalar_prefetch=2,
            grid=(ngroups, n // tq),
            in_specs=[pl.BlockSpec((2 * npg, 1, nl), lambda p, i, *_: (p, 0, 0)),
                      pl.BlockSpec((2 * npg, 1, LANES), lambda p, i, *_: (p, 0, 0)),
                      pl.BlockSpec((gw, tq), lambda p, i, *_: (p, i)),
                      pl.BlockSpec((n, gw), lambda p, i, *_: (0, p), **once),
                      pl.BlockSpec((npg, n, LANES), lambda p, i, *_: (p, 0, 0), **once),
                      pl.BlockSpec((gw, n), lambda p, i, *_: (p, 0), **once),
                      pl.BlockSpec((npg, n_meta, 4 * dh), lambda p, i, *_: (p, 0, 0)),
                      pl.BlockSpec((gw, n_meta), lambda p, i, *_: (p, 0)),
                      mspec, mspec, mspec],
            out_specs=[pl.BlockSpec((tq, gw), lambda p, i, *_: (i, p)),
                       pl.BlockSpec((n_meta, gw), lambda p, i, *_: (0, p))],
            scratch_shapes=[pltpu.VMEM((2 * npg, 4 * dh, tq), BF16),
                            pltpu.VMEM((2 * npg, 1, tq), F32),
                            pltpu.VMEM((2 * npg, dh + ACC_PAD, tq), F32)]),
        out_shape=[jax.ShapeDtypeStruct((n, d), BF16), jax.ShapeDtypeStruct((n_meta, d), BF16)],
        compiler_params=_params(2),
        name="fox_attn",
    )(jlo, fixed, c3, shift3, qt_b, kb, kaux, vt_b, kx_meta, vt_meta, qb_aux, kb_aux, vb_aux)


PRUNE_NATS = 37.0
NORM_SLACK = 1.01


FIXED_SHIFT_MAX_SPAN = 60.0


def _fox_score_shift(qsq, ksq, group):
    nh = qsq.shape[0]
    u = jnp.sqrt(jnp.max(qsq, axis=1) * jnp.max(ksq, axis=1)) * NORM_SLACK
    fixed = jnp.all((2.0 * u <= FIXED_SHIFT_MAX_SPAN).reshape(nh // group, group), axis=1)
    shift = jnp.where(jnp.repeat(fixed, group), u, 0.0)
    return fixed.astype(jnp.int32), jnp.broadcast_to(shift[:, None, None], (nh, 1, LANES))


def _fox_first_key_block(c_main, qsq, ksq, tq, group):
    nh, n = c_main.shape
    nq = n // tq
    u = jnp.sqrt(jnp.max(qsq, axis=1) * jnp.max(ksq, axis=1)) * (NORM_SLACK / LOG2E)
    c_first = c_main[:, 0::tq]
    c_last = c_main[:, tq - 1::tq]
    bound = 2.0 * u[:, None, None] + c_first[:, :, None] - c_last[:, None, :]
    j_lt_i = jnp.arange(nq)[None, :] < jnp.arange(nq)[:, None]
    skip = (bound <= -PRUNE_NATS) & j_lt_i[None]
    jlo = jnp.sum(skip, axis=2).astype(jnp.int32)
    return jnp.min(jlo.reshape(nh // group, group, nq), axis=1)


def _fox_sample_body(q_ref, kn_ref, vn_ref, cn_ref, kc_ref, vc_ref, suf_ref, o_ref,
                     m_s, l_s, acc_s, *, nh, dh, s_len):
    t = pl.program_id(1)

    @pl.when(t == 0)
    def _():
        r_i = lax.broadcasted_iota(jnp.int32, (s_len, s_len), 0)
        c_i = lax.broadcasted_iota(jnp.int32, (s_len, s_len), 1)
        cols = [slice(h * dh, (h + 1) * dh) for h in range(nh)]
        x = [_mm_nt(q_ref[:, cols[h]], kn_ref[:, cols[h]]) - cn_ref[0, h:h + 1, 0:s_len]
             for h in range(nh)]
        x = [jnp.where(c_i <= r_i, x[h], NEG) for h in range(nh)]
        m = [jnp.max(x[h], axis=-1, keepdims=True) for h in range(nh)]
        p = [jnp.exp(x[h] - m[h]) for h in range(nh)]
        pv = [_mm(p[h].astype(BF16), vn_ref[:, cols[h]]) for h in range(nh)]
        for h in range(nh):
            m_s[h] = m[h]
            l_s[h] = jnp.sum(p[h], axis=-1, keepdims=True)
            acc_s[h] = pv[h]

    heads = range(nh)
    x = [_mm(q_ref[:, h * dh:(h + 1) * dh], kc_ref[0, h].astype(BF16)) + suf_ref[0, h:h + 1, :]
         for h in heads]
    m_prev = [m_s[h] for h in heads]
    m_new = [jnp.maximum(m_prev[h], jnp.max(x[h], axis=-1, keepdims=True)) for h in heads]
    p = [jnp.exp(x[h] - m_new[h]) for h in heads]
    pv = [_mm_nt(p[h].astype(BF16), vc_ref[0, h].astype(BF16)) for h in heads]
    for h in heads:
        alpha = jnp.exp(m_prev[h] - m_new[h])
        l_s[h] = alpha * l_s[h] + jnp.sum(p[h], axis=-1, keepdims=True)
        acc_s[h] = alpha * acc_s[h] + pv[h]
        m_s[h] = m_new[h]

    @pl.when(t == pl.num_programs(1) - 1)
    def _():
        o_ref[...] = jnp.concatenate([acc_s[h] / l_s[h] for h in range(nh)], axis=-1).astype(BF16)


def _fox_sample(qb, kb, vb, cn, kc_t, vc_t, layer, suf, s_len, tk):
    _, nb, nh, dh, past = kc_t.shape
    d = nh * dh
    tok = pl.BlockSpec((s_len, d), lambda b, t: (b, 0))
    cache = pl.BlockSpec((None, 1, nh, dh, tk), lambda b, t: (layer, b, 0, 0, t))
    return pl.pallas_call(
        functools.partial(_fox_sample_body, nh=nh, dh=dh, s_len=s_len),
        grid=(nb, past // tk),
        in_specs=[tok, tok, tok, pl.BlockSpec((1, nh, LANES), lambda b, t: (b, 0, 0)),
                  cache, cache, pl.BlockSpec((1, nh, tk), lambda b, t: (b, 0, t))],
        out_specs=tok,
        out_shape=jax.ShapeDtypeStruct((nb * s_len, d), BF16),
        scratch_shapes=[pltpu.VMEM((nh, s_len, 1), F32), pltpu.VMEM((nh, s_len, 1), F32),
                        pltpu.VMEM((nh, s_len, dh), F32)],
        compiler_params=_params(2),
        name="fox_sample",
    )(qb, kb, vb, cn, kc_t, vc_t, suf)


def _largest_tile(n, cap):
    t = min(n, cap)
    while n % t:
        t //= 2
    return t


def kernel(x_prompt, x_sample, state_hgrn, cache_k, cache_v, cache_logf, meta_tokens,
           norm_pre_mix, norm_post_mix, norm_pre_ffn, norm_post_ffn,
           a_w_in, a_lb_logits, a_g_norm, a_w_out, b_w_in, b_f, b_w_out,
           ffn_w_gu, ffn_w_down):
    batch, seq, d = x_prompt.shape
    assert batch == 1, "one prompt stream per step"
    n_dec, s_len, _ = x_sample.shape
    n_meta = meta_tokens.shape[0]
    depth = norm_pre_mix.shape[0]
    _, _, a_heads, a_dk, a_dv = state_hgrn.shape
    _, _, past, b_heads, b_dh = cache_k.shape
    ns = n_dec * s_len
    assert s_len % HGRN_SUB == 0 and n_meta % HGRN_SUB == 0 and ns % n_meta == 0
    assert 2 * b_dh == LANES and a_dk == LANES and a_dv == LANES

    tm = _largest_tile(seq, 512)
    n_aux = ns + n_meta
    meta_blk = ns // n_meta
    rec_r = _largest_tile(seq, 512)
    rec_c = _largest_tile(rec_r, 64)
    tq = _largest_tile(seq, ATTN_TILE)
    tk_cache = _largest_tile(past, 1024)
    npair = b_heads // 2
    npg = ATTN_PAIRS_PER_STEP if npair % ATTN_PAIRS_PER_STEP == 0 else 1

    sm = jax.nn.softmax(a_lb_logits.astype(F32), axis=0)
    lb_all = jnp.cumsum(sm, axis=0) - sm[0]

    xm = x_prompt.reshape(seq, d)
    xa = jnp.concatenate([x_sample.reshape(ns, d), meta_tokens.astype(F32)], axis=0)
    row = lambda w: w.reshape(1, -1).astype(F32)
    cache_kt = jnp.transpose(cache_k, (0, 1, 3, 4, 2))
    cache_vt = jnp.transpose(cache_v, (0, 1, 3, 4, 2))
    cache_lt = jnp.swapaxes(cache_logf.astype(F32), 2, 3)

    st_p, st_s, k_p, v_p, lf_p, k_s, v_s, lf_s = [], [], [], [], [], [], [], []
    for i in range(depth):
        j = i // 2
        if i % 2 == 0:
            lb = lb_all[j]
            lbp = jnp.stack([jnp.log(jnp.maximum(lb, LB_FLOOR)), jnp.log1p(-lb), 1.0 - lb])
            w_in = a_w_in[j].astype(BF16)
            w_out = a_w_out[j].astype(BF16)
            qm, km, gm, vm, gtm = _hgrn_proj(xm, row(norm_pre_mix[i]), w_in, lbp, tm)
            qa, ka, ga, va, gta = _hgrn_proj(xa, row(norm_pre_mix[i]), w_in, lbp, n_aux)
            o_s, s_s = _hgrn_rec(qa, ka, ga, va, state_hgrn[j].astype(F32),
                                 n_seq=n_dec, seq_len=s_len, r=s_len, c=s_len)
            o_m, s_p, o_meta = _hgrn_rec(qm, km, gm, vm,
                                         jnp.zeros((1, a_heads, a_dk, a_dv), F32),
                                         n_seq=1, seq_len=seq, r=rec_r, c=rec_c,
                                         meta=(qa, ka, ga, va, n_meta, meta_blk))
            st_p.append(s_p.astype(state_hgrn.dtype))
            st_s.append(s_s.astype(state_hgrn.dtype))
            o_a = jnp.concatenate([o_s, o_meta], axis=0)
            mix_m = (o_m, gtm, xm, row(a_g_norm[j]))
            mix_a = (o_a, gta, xa, row(a_g_norm[j]))
            head_dim = a_dv
        else:
            w_t = jnp.swapaxes(b_w_in[j], 0, 1)
            wt_qkv = w_t[:3 * d].astype(BF16)
            w_in = jnp.swapaxes(wt_qkv, 0, 1)
            w_f = jnp.pad(b_w_in[j][:, 3 * d:], ((0, 0), (0, LANES - b_heads))).astype(BF16)
            bf_row = jnp.pad(b_f[j].astype(F32), (0, LANES - b_heads)).reshape(1, LANES)
            w_out = b_w_out[j].astype(BF16)
            scale = float(b_dh) ** -0.5
            kt_m, vt_m, lf_m, qtb_m, kb_m, vtb_m, qsq, ksq = _fox_proj_t(
                xm, row(norm_pre_mix[i]), wt_qkv, w_f, bf_row, b_heads, tm, scale * LOG2E)
            kf_a, vf_a, lf_a, qb_a, kb_a, vb_a = _fox_proj(
                xa, row(norm_pre_mix[i]), w_in, w_f, bf_row, b_heads, n_aux)
            lf_t = jnp.concatenate(
                [jnp.pad(lf_a[ns:].T, ((0, 0), (0, LANES - n_meta))), lf_m.T], axis=1)
            c_all = _lane_cumsum(lf_t, False, b_heads)
            kaux = _fox_kbias(c_all[:, LANES:].T, npair, tm, LOG2E)
            kaux_meta = _fox_kbias(c_all[:, :n_meta].T, npair, n_meta, LOG2E)
            k_meta = jnp.swapaxes(kb_a[ns:].reshape(n_meta, npair, LANES), 0, 1)
            jlo = _fox_first_key_block(c_all[:, LANES:], qsq, ksq, tq, 2 * npg)
            fixed, shift3 = _fox_score_shift(qsq, ksq, 2 * npg)
            o_m, o_meta = _fox_attn(jlo, fixed, shift3, c_all.reshape(b_heads, 1, -1),
                                    qtb_m, kb_m, kaux, vtb_m,
                                    jnp.concatenate([k_meta, kaux_meta], axis=-1), vb_a[ns:].T,
                                    qb_a, kb_a, vb_a, n_meta, meta_blk, b_dh, tq, LOG2E, npg)
            cl_t = cache_lt[j].reshape(n_dec * b_heads, past)
            tr = _largest_tile(n_dec * b_heads, LANES)
            suf = _lane_cumsum(cl_t, True, tr).reshape(n_dec, b_heads, past)
            ln_t = jnp.swapaxes(lf_a[:ns].reshape(n_dec, s_len, b_heads), 1, 2)
            ln_t = jnp.pad(ln_t, ((0, 0), (0, 0), (0, LANES - s_len)))
            cn = _lane_cumsum(ln_t.reshape(n_dec * b_heads, LANES), False, tr)
            o_s = _fox_sample(qb_a, kb_a, vb_a, cn.reshape(n_dec, b_heads, LANES),
                              cache_kt, cache_vt, j, suf, s_len, tk_cache)
            o_a = jnp.concatenate([o_s, o_meta], axis=0)
            mix_m = (o_m, None, xm, None)
            mix_a = (o_a, None, xa, None)
            head_dim = 0
            shp = lambda t, n: t.reshape(-1, n, b_heads, b_dh)
            k_p.append((kf_a[ns:].T, kt_m))
            v_p.append((vf_a[ns:].T, vt_m))
            lf_p.append(jnp.concatenate([lf_a[ns:], lf_m], axis=0).reshape(1, n_meta + seq, b_heads))
            k_s.append(shp(kf_a[:ns], s_len))
            v_s.append(shp(vf_a[:ns], s_len))
            lf_s.append(lf_a[:ns].reshape(n_dec, s_len, b_heads))
        ffn_w = (w_out, row(norm_post_mix[i]), row(norm_pre_ffn[i]), ffn_w_gu[i].astype(BF16),
                 ffn_w_down[i].astype(BF16), row(norm_post_ffn[i]))
        xm = _mix_ffn(*mix_m, *ffn_w, tm, head_dim)
        xa = _mix_ffn(*mix_a, *ffn_w, n_aux, head_dim)

    def prompt_kv(parts):
        t = jnp.concatenate([jnp.stack([m for m, _ in parts]), jnp.stack([x for _, x in parts])],
                            axis=2)
        t = t.reshape(len(parts), b_heads, b_dh, n_meta + seq)
        return jnp.transpose(t, (0, 3, 1, 2))[:, None]

    return (xm.reshape(1, seq, d), xa[:ns].reshape(n_dec, s_len, d),
            jnp.stack(st_p), prompt_kv(k_p), prompt_kv(v_p), jnp.stack(lf_p),
            jnp.stack(st_s), jnp.stack(k_s), jnp.stack(v_s), jnp.stack(lf_s))
```

```python
import functools

import jax
import jax.numpy as jnp
from jax import lax
from jax.experimental import pallas as pl
from jax.experimental.pallas import tpu as pltpu

F32 = jnp.float32
BF16 = jnp.bfloat16
EPS = 1e-6
NEG = -1e30
LB_FLOOR = 1e-30

LANES = 128
VMEM_LIMIT = 56 * 1024 * 1024
HGRN_SUB = 16
HIGHEST = lax.Precision.HIGHEST
LOG2E = 1.4426950408889634


def _params(n_grid):
    return pltpu.CompilerParams(dimension_semantics=("arbitrary",) * n_grid,
                                vmem_limit_bytes=VMEM_LIMIT)


def _resident(shape):
    nd = len(shape)
    return pl.BlockSpec(shape, lambda *_: (0,) * nd, pipeline_mode=pl.Buffered(1))


def _rmsnorm(x, w):
    return x * lax.rsqrt(jnp.mean(x * x, axis=-1, keepdims=True) + EPS) * w


def _sigmoid(x):
    return 1.0 / (1.0 + jnp.exp(-x))


def _log1p_exp(z):
    return jnp.log(1.0 + jnp.exp(z))


def _log_sigmoid(x):
    return jnp.minimum(x, 0.0) - _log1p_exp(-jnp.abs(x))


def _mm(a, b):
    return jnp.dot(a, b, preferred_element_type=F32)


def _mm_nt(a, b):
    return lax.dot_general(a, b, (((1,), (1,)), ((), ())), preferred_element_type=F32)


def _mm_tn(a, b):
    return lax.dot_general(a, b, (((0,), (0,)), ((), ())), preferred_element_type=F32)


def _hgrn_proj_body(x_ref, nw_ref, w_ref, lb_ref, q_ref, k_ref, g_ref, v_ref, gate_ref):
    f = q_ref.shape[1]
    d = v_ref.shape[1]
    xn = _rmsnorm(x_ref[...], nw_ref[...]).astype(BF16)
    q = _mm(xn, w_ref[:, 0:f])
    q_ref[...] = q * _sigmoid(q)
    fl = _mm(xn, w_ref[:, f:2 * f])
    e = jnp.exp(-jnp.abs(fl))
    r = 1.0 / (1.0 + e)
    a = lb_ref[0:1, :]
    b = lb_ref[1:2, :] + (jnp.minimum(fl, 0.0) - jnp.log(1.0 + e))
    g_ref[...] = jnp.maximum(a, b) + _log1p_exp(-jnp.abs(a - b))
    k_ref[...] = lb_ref[2:3, :] * jnp.where(fl >= 0.0, e * r, r)
    v_ref[...] = _mm(xn, w_ref[:, 2 * f:2 * f + d])
    gt = _mm(xn, w_ref[:, 2 * f + d:])
    gate_ref[...] = gt * _sigmoid(gt)


def _hgrn_proj(x, nw, w_bf, lbp, tm):
    n, d = x.shape
    f = lbp.shape[1]
    row = lambda w: pl.BlockSpec((tm, w), lambda i: (i, 0))
    return pl.pallas_call(
        _hgrn_proj_body,
        grid=(n // tm,),
        in_specs=[row(d), _resident((1, d)), _resident(w_bf.shape), _resident(lbp.shape)],
        out_specs=[row(f), row(f), row(f), row(d), row(d)],
        out_shape=[jax.ShapeDtypeStruct((n, w), F32) for w in (f, f, f, d, d)],
        compiler_params=_params(1),
        name="hgrn_proj",
    )(x, nw, w_bf, lbp)


def _hgrn_chunk(q, k, g, v, sts, c):
    nh = len(sts)
    sub, half = HGRN_SUB, HGRN_SUB // 2
    hs = lambda x, h: x[:, h * LANES:(h + 1) * LANES]
    r_i = lax.broadcasted_iota(jnp.int32, (c, c), 0)
    c_i = lax.broadcasted_iota(jnp.int32, (c, c), 1)
    tril = jnp.where(r_i >= c_i, 1.0, 0.0).astype(F32)
    gc = jnp.dot(tril, g, precision=HIGHEST, preferred_element_type=F32) * LOG2E
    g_last = gc[c - 1:c, :]
    qe = (q * jnp.exp2(gc)).astype(BF16)
    o_inter = [_mm_nt(hs(qe, h), sts[h].astype(BF16)) for h in range(nh)]
    kd = (k * jnp.exp2(g_last - gc)).astype(BF16)
    vb = v.astype(BF16)
    st_dec = jnp.exp2(g_last)
    st_new = [sts[h] * hs(st_dec, h) + _mm_tn(hs(vb, h), hs(kd, h)) for h in range(nh)]

    lk = jnp.log2(k) - gc
    half_row = lax.broadcasted_iota(jnp.int32, (half, 1), 0)
    out_rows = []
    for i in range(c // sub):
        r0 = i * sub
        gi, qi, vi, lki = gc[r0:r0 + sub], q[r0:r0 + sub], v[r0:r0 + sub], lk[r0:r0 + sub]
        o_lo = [o_inter[h][r0:r0 + half] for h in range(nh)]
        o_hi = [o_inter[h][r0 + half:r0 + sub] for h in range(nh)]
        if i > 0:
            g_first = gc[r0:r0 + 1]
            qs = (qi * jnp.exp2(gi - g_first)).astype(BF16)
            ks = (k[:r0] * jnp.exp2(g_first - gc[:r0])).astype(BF16)
            att = [_mm_nt(hs(qs, h), hs(ks, h)).astype(BF16) for h in range(nh)]
            off = [_mm(att[h], hs(vb[:r0], h)) for h in range(nh)]
            o_lo = [o_lo[h] + off[h][:half] for h in range(nh)]
            o_hi = [o_hi[h] + off[h][half:] for h in range(nh)]
        for rows, acc, s_range in ((slice(0, half), o_lo, range(half)),
                                   (slice(half, sub), o_hi, range(sub))):
            g_rows, q_rows = gi[rows], qi[rows]
            for s in s_range:
                diff = g_rows + lki[s:s + 1]
                if s >= rows.start:
                    diff = jnp.where(half_row >= s - rows.start, diff, NEG)
                prod = q_rows * jnp.exp2(diff)
                for h in range(nh):
                    col = jnp.sum(hs(prod, h), axis=-1, keepdims=True)
                    acc[h] = acc[h] + col * hs(vi, h)[s:s + 1]
        out_rows.append(jnp.concatenate(
            [jnp.concatenate([o_lo[h], o_hi[h]], axis=0) for h in range(nh)], axis=1))
    o = jnp.concatenate(out_rows, axis=0) if len(out_rows) > 1 else out_rows[0]
    return o, st_new


def _hgrn_rec_body(*refs, c, r, has_meta):
    if has_meta:
        (q_ref, k_ref, g_ref, v_ref, s0_ref, qm_ref, km_ref, gm_ref, vm_ref,
         o_ref, s_ref, om_ref, st_ref) = refs
    else:
        q_ref, k_ref, g_ref, v_ref, s0_ref, o_ref, s_ref, st_ref = refs
    blk = pl.program_id(1)
    nh = st_ref.shape[0]

    @pl.when(blk == 0)
    def _():
        sts = [s0_ref[0, h].T for h in range(nh)]
        if has_meta:
            o_m, sts = _hgrn_chunk(qm_ref[...], km_ref[...], gm_ref[...], vm_ref[...],
                                   sts, qm_ref.shape[0])
            om_ref[...] = o_m
        for h in range(nh):
            st_ref[h] = sts[h]

    def body(j, carry):
        sl = pl.ds(pl.multiple_of(j * c, c), c)
        o, sts = _hgrn_chunk(q_ref[sl, :], k_ref[sl, :], g_ref[sl, :], v_ref[sl, :],
                             [st_ref[h] for h in range(nh)], c)
        o_ref[sl, :] = o
        for h in range(nh):
            st_ref[h] = sts[h]
        return carry

    lax.fori_loop(0, r // c, body, 0)

    @pl.when(blk == pl.num_programs(1) - 1)
    def _():
        for h in range(nh):
            s_ref[0, h] = st_ref[h].T


def _hgrn_rec(q, k, g, v, s0, *, n_seq, seq_len, r, c, meta=None):
    _, h, dk, dv = s0.shape
    nblk = seq_len // r
    tok = lambda w: pl.BlockSpec((r, h * w), lambda b, i: (b * nblk + i, 0))
    st_spec = pl.BlockSpec((1, h, dk, dv), lambda b, i: (b, 0, 0, 0))
    in_specs = [tok(dk), tok(dk), tok(dk), tok(dv), st_spec]
    args = [q, k, g, v, s0]
    out_specs = [tok(dv), st_spec]
    out_shape = [jax.ShapeDtypeStruct((n_seq * seq_len, h * dv), F32),
                 jax.ShapeDtypeStruct(s0.shape, F32)]
    if meta is not None:
        qm, km, gm, vm, n_meta, meta_blk = meta
        mspec = lambda w: pl.BlockSpec((n_meta, h * w), lambda b, i: (meta_blk, 0))
        in_specs += [mspec(dk), mspec(dk), mspec(dk), mspec(dv)]
        args += [qm, km, gm, vm]
        out_specs.append(pl.BlockSpec((n_meta, h * dv), lambda b, i: (0, 0)))
        out_shape.append(jax.ShapeDtypeStruct((n_meta, h * dv), F32))
    return pl.pallas_call(
        functools.partial(_hgrn_rec_body, c=c, r=r, has_meta=meta is not None),
        grid=(n_seq, nblk),
        in_specs=in_specs,
        out_specs=out_specs,
        out_shape=out_shape,
        scratch_shapes=[pltpu.VMEM((h, dv, dk), F32)],
        compiler_params=_params(2),
        name="hgrn_rec",
    )(*args)


def _mix_ffn_body(*refs, head_dim, dff, fc):
    if head_dim:
        (o_ref, gate_ref, x_ref, gn_ref, wo_ref, pm_ref,
         nf_ref, wgu_ref, wd_ref, pf_ref, y_ref) = refs
        o = o_ref[...]
        parts = []
        for h in range(o.shape[1] // head_dim):
            oh = o[:, h * head_dim:(h + 1) * head_dim]
            parts.append(oh * lax.rsqrt(jnp.mean(oh * oh, axis=-1, keepdims=True) + EPS))
        o = (jnp.concatenate(parts, axis=-1) * gn_ref[...] * gate_ref[...]).astype(BF16)
    else:
        o_ref, x_ref, wo_ref, pm_ref, nf_ref, wgu_ref, wd_ref, pf_ref, y_ref = refs
        o = o_ref[...]
    x = x_ref[...] + _rmsnorm(_mm(o, wo_ref[...]), pm_ref[...])
    xn = _rmsnorm(x, nf_ref[...]).astype(BF16)
    acc = jnp.zeros(x.shape, F32)
    for c0 in range(0, dff, fc):
        a = _mm(xn, wgu_ref[:, c0:c0 + fc])
        u = _mm(xn, wgu_ref[:, dff + c0:dff + c0 + fc])
        hid = (a * _sigmoid(a) * u).astype(BF16)
        acc = acc + _mm(hid, wd_ref[c0:c0 + fc, :])
    y_ref[...] = x + _rmsnorm(acc, pf_ref[...])


def _mix_ffn(o, gate, x, gn, wo_bf, pm, nf, wgu_bf, wd_bf, pf, tm, head_dim):
    n, d = x.shape
    dff = wd_bf.shape[0]
    fc = dff // 2 if (dff // 2) % LANES == 0 else dff
    row = pl.BlockSpec((tm, d), lambda i: (i, 0))
    vec = _resident((1, d))
    ffn_specs = [vec, _resident(wgu_bf.shape), _resident(wd_bf.shape), vec]
    if head_dim:
        in_specs = [row, row, row, vec, _resident(wo_bf.shape), vec] + ffn_specs
        args = (o, gate, x, gn, wo_bf, pm, nf, wgu_bf, wd_bf, pf)
    else:
        in_specs = [row, row, _resident(wo_bf.shape), vec] + ffn_specs
        args = (o, x, wo_bf, pm, nf, wgu_bf, wd_bf, pf)
    return pl.pallas_call(
        functools.partial(_mix_ffn_body, head_dim=head_dim, dff=dff, fc=fc),
        grid=(n // tm,),
        in_specs=in_specs,
        out_specs=row,
        out_shape=jax.ShapeDtypeStruct((n, d), F32),
        compiler_params=_params(1),
        name="mix_ffn",
    )(*args)


def _fox_proj_body(x_ref, nw_ref, w_ref, wf_ref, bf_ref, k_ref, v_ref, lf_ref,
                   qb_ref, kb_ref, vb_ref, *, scale):
    d = k_ref.shape[1]
    nh = lf_ref.shape[1]
    xn = _rmsnorm(x_ref[...], nw_ref[...]).astype(BF16)
    qb_ref[...] = (_mm(xn, w_ref[:, 0:d]) * scale).astype(BF16)
    k = _mm(xn, w_ref[:, d:2 * d])
    k_ref[...] = k
    kb_ref[...] = k.astype(BF16)
    v = _mm(xn, w_ref[:, 2 * d:3 * d])
    v_ref[...] = v
    vb_ref[...] = v.astype(BF16)
    lf = _log_sigmoid(_mm(xn, wf_ref[...]) + bf_ref[...])
    lf_ref[...] = lf[:, 0:nh]


def _fox_proj(x, nw, w_bf, wf_bf, bf_row, nh, tm):
    n, d = x.shape
    row = lambda w: pl.BlockSpec((tm, w), lambda i: (i, 0))
    scale = float(d // nh) ** -0.5
    return pl.pallas_call(
        functools.partial(_fox_proj_body, scale=scale),
        grid=(n // tm,),
        in_specs=[row(d), _resident((1, d)), _resident(w_bf.shape), _resident(wf_bf.shape),
                  _resident(bf_row.shape)],
        out_specs=[row(d), row(d), row(nh), row(d), row(d), row(d)],
        out_shape=[jax.ShapeDtypeStruct((n, d), F32), jax.ShapeDtypeStruct((n, d), F32),
                   jax.ShapeDtypeStruct((n, nh), F32), jax.ShapeDtypeStruct((n, d), BF16),
                   jax.ShapeDtypeStruct((n, d), BF16), jax.ShapeDtypeStruct((n, d), BF16)],
        compiler_params=_params(1),
        name="fox_proj",
    )(x, nw, w_bf, wf_bf, bf_row)


def _head_sq_norms(xt, nh):
    dh = xt.shape[0] // nh
    x2 = xt * xt
    return jnp.concatenate([jnp.sum(x2[h * dh:(h + 1) * dh], axis=0, keepdims=True)
                            for h in range(nh)], axis=0)


def _fox_proj_t_body(x_ref, nw_ref, wt_ref, wf_ref, bf_ref, kt_ref, vt_ref, lf_ref,
                     qtb_ref, kb_ref, vtb_ref, qsq_ref, ksq_ref, *, qscale):
    d = kt_ref.shape[0]
    nh = lf_ref.shape[1]
    xn = _rmsnorm(x_ref[...], nw_ref[...]).astype(BF16)
    qtb = (_mm_nt(wt_ref[0:d, :], xn) * qscale).astype(BF16)
    qtb_ref[...] = qtb
    qsq_ref[...] = _head_sq_norms(qtb.astype(F32), nh)
    kt = _mm_nt(wt_ref[d:2 * d, :], xn)
    kt_ref[...] = kt
    ksq_ref[...] = _head_sq_norms(kt.astype(BF16).astype(F32), nh)
    kb_ref[...] = kt.T.astype(BF16)
    vt = _mm_nt(wt_ref[2 * d:3 * d, :], xn)
    vt_ref[...] = vt
    vtb_ref[...] = vt.astype(BF16)
    lf = _log_sigmoid(_mm(xn, wf_ref[...]) + bf_ref[...])
    lf_ref[...] = lf[:, 0:nh]


def _fox_proj_t(x, nw, wt_bf, wf_bf, bf_row, nh, tm, qscale):
    n, d = x.shape
    row = lambda w: pl.BlockSpec((tm, w), lambda i: (i, 0))
    col = pl.BlockSpec((d, tm), lambda i: (0, i))
    sq = pl.BlockSpec((nh, tm), lambda i: (0, i))
    return pl.pallas_call(
        functools.partial(_fox_proj_t_body, qscale=qscale),
        grid=(n // tm,),
        in_specs=[row(d), _resident((1, d)), _resident(wt_bf.shape), _resident(wf_bf.shape),
                  _resident(bf_row.shape)],
        out_specs=[col, col, row(nh), col, row(d), col, sq, sq],
        out_shape=[jax.ShapeDtypeStruct((d, n), F32), jax.ShapeDtypeStruct((d, n), F32),
                   jax.ShapeDtypeStruct((n, nh), F32), jax.ShapeDtypeStruct((d, n), BF16),
                   jax.ShapeDtypeStruct((n, d), BF16), jax.ShapeDtypeStruct((d, n), BF16),
                   jax.ShapeDtypeStruct((nh, n), F32), jax.ShapeDtypeStruct((nh, n), F32)],
        compiler_params=_params(1),
        name="fox_proj_t",
    )(x, nw, wt_bf, wf_bf, bf_row)


BIAS_LANES_PER_HEAD = 6


def _split3(v):
    hi = v.astype(BF16).astype(F32)
    r1 = v - hi
    mid = r1.astype(BF16).astype(F32)
    lo = (r1 - mid).astype(BF16).astype(F32)
    return hi, mid, lo


def _fox_kbias_body(c4_ref, aux_ref, *, cscale, nh):
    tm = c4_ref.shape[0]
    npair = aux_ref.shape[0]
    hi, mid, lo = _split3(c4_ref[...] * (-cscale))
    group = lax.broadcasted_iota(jnp.int32, (tm, 4 * nh), 1) // nh
    src = jnp.where(group == 0, hi, jnp.where(group == 1, mid, jnp.where(group == 2, lo, 1.0)))
    row = lax.broadcasted_iota(jnp.int32, (4 * nh, npair * LANES), 0)
    col = lax.broadcasted_iota(jnp.int32, (4 * nh, npair * LANES), 1)
    lane = col % LANES
    slot = lane % BIAS_LANES_PER_HEAD
    head = 2 * (col // LANES) + lane // BIAS_LANES_PER_HEAD
    pick = (lane < 2 * BIAS_LANES_PER_HEAD) & (row == jnp.minimum(slot, 3) * nh + head)
    out = _mm(src.astype(BF16), jnp.where(pick, 1.0, 0.0).astype(BF16))
    for p in range(npair):
        aux_ref[p] = out[:, p * LANES:(p + 1) * LANES].astype(BF16)


def _fox_kbias(c_rows, npair, tm, cscale):
    n, nh = c_rows.shape
    return pl.pallas_call(
        functools.partial(_fox_kbias_body, cscale=cscale, nh=nh),
        grid=(n // tm,),
        in_specs=[pl.BlockSpec((tm, 4 * nh), lambda i: (i, 0))],
        out_specs=pl.BlockSpec((npair, tm, LANES), lambda i: (0, i, 0)),
        out_shape=jax.ShapeDtypeStruct((npair, n, LANES), BF16),
        compiler_params=_params(1),
        name="fox_kbias",
    )(jnp.tile(c_rows, (1, 4)))


def _lane_cumsum_body(x_ref, y_ref, *, reverse_exclusive):
    rows, n = x_ref.shape
    nb = n // LANES
    t_i = lax.broadcasted_iota(jnp.int32, (LANES, LANES), 0)
    k_i = lax.broadcasted_iota(jnp.int32, (LANES, LANES), 1)
    sel = (t_i > k_i) if reverse_exclusive else (t_i <= k_i)
    tri = jnp.where(sel, 1.0, 0.0).astype(F32)

    def body(i, carry):
        b = (nb - 1 - i) if reverse_exclusive else i
        sl = slice(b * LANES, (b + 1) * LANES)
        xb = x_ref[:, sl]
        y_ref[:, sl] = jnp.dot(xb, tri, precision=HIGHEST, preferred_element_type=F32) + carry
        return carry + jnp.sum(xb, axis=-1, keepdims=True)

    carry = jnp.zeros((rows, 1), F32)
    for i in range(nb):
        carry = body(i, carry)


def _lane_cumsum(x, reverse_exclusive, tr):
    rows, n = x.shape
    spec = pl.BlockSpec((tr, n), lambda i: (i, 0))
    return pl.pallas_call(
        functools.partial(_lane_cumsum_body, reverse_exclusive=reverse_exclusive),
        grid=(rows // tr,),
        in_specs=[spec],
        out_specs=spec,
        out_shape=jax.ShapeDtypeStruct((rows, n), F32),
        compiler_params=_params(1),
        name="lane_cumsum",
    )(x)


ATTN_TILE = 512
ATTN_PAIRS_PER_STEP = 2
ACC_PAD = 16


def _fox_attn_body(jlo_ref, fixed_ref, c_ref, sh_ref, qt_ref, k_ref, ka_ref, vt_ref, kme_ref,
                   vtm_ref, qm_ref, km_ref, vm_ref, o_ref, om_ref, qx_s, m_s, acc_s,
                   *, tq, tk, dh, n_meta, cscale, npg):
    i = pl.program_id(1)
    heads = range(2 * npg)
    pair_lanes = lambda x, pp: x[:, pp * 2 * dh:(pp + 1) * 2 * dh]
    pair_rows = lambda x, pp: x[pp * 2 * dh:(pp + 1) * 2 * dh, :]
    lane = lax.broadcasted_iota(jnp.int32, (1, 2 * dh), 1)
    head_sel = [lane < dh, lane >= dh]

    @pl.when(i == 0)
    def _():
        r_i = lax.broadcasted_iota(jnp.int32, (n_meta, n_meta), 0)
        c_i = lax.broadcasted_iota(jnp.int32, (n_meta, n_meta), 1)
        outs = []
        for pp in range(npg):
            qm, km, vm = (pair_lanes(r[...], pp) for r in (qm_ref, km_ref, vm_ref))
            out = jnp.zeros((n_meta, 2 * dh), F32)
            for a in range(2):
                qa = jnp.where(head_sel[a], qm, jnp.zeros_like(qm))
                x = _mm_nt(qa, km) - c_ref[2 * pp + a, :, 0:n_meta]
                x = jnp.where(c_i <= r_i, x, NEG)
                p = jnp.exp(x - jnp.max(x, axis=-1, keepdims=True))
                oa = _mm(p.astype(BF16), vm) / jnp.sum(p, axis=-1, keepdims=True)
                out = jnp.where(head_sel[a], oa, out)
            outs.append(out)
        om_ref[...] = jnp.concatenate(outs, axis=1).astype(BF16)

    q0 = pl.multiple_of(LANES + i * tq, LANES)
    row = lax.broadcasted_iota(jnp.int32, (2 * dh, tq), 0)
    for hh in heads:
        pp, a = divmod(hh, 2)
        qt = pair_rows(qt_ref[...], pp)
        hi, mid, lo = _split3(c_ref[hh, :, pl.ds(q0, tq)] * cscale - sh_ref[hh, :, 0:1])
        slot = row - a * BIAS_LANES_PER_HEAD
        bias = jnp.where(slot == 3, hi, jnp.where(slot == 4, mid, jnp.where(slot == 5, lo, 1.0)))
        bias = jnp.where((slot >= 0) & (slot < BIAS_LANES_PER_HEAD), bias, 0.0)
        qx_s[hh, 0:2 * dh, :] = jnp.where((row >= a * dh) & (row < (a + 1) * dh), qt,
                                          jnp.zeros_like(qt))
        qx_s[hh, 2 * dh:4 * dh, :] = bias.astype(BF16)
        m_s[hh] = jnp.full((1, tq), NEG, F32)
        acc_s[hh] = jnp.zeros((dh + ACC_PAD, tq), F32)

    def scores(kxs):
        return [_mm(kxs[hh // 2], qx_s[hh]) for hh in heads]

    def update(sts, vts, masked, fixed, q_lo=0):
        tkb, tqb = sts[0].shape
        qs = slice(q_lo, q_lo + tqb)
        ones = jnp.where(lax.broadcasted_iota(jnp.int32, (ACC_PAD, tkb), 0) == 0,
                         1.0, 0.0).astype(BF16)
        for hh in heads:
            pp, a = divmod(hh, 2)
            st = sts[hh]
            if masked:
                k_i = lax.broadcasted_iota(jnp.int32, (tkb, tqb), 0)
                q_i = lax.broadcasted_iota(jnp.int32, (tkb, tqb), 1)
                st = jnp.where(k_i <= q_i, st, NEG)
            vx = jnp.concatenate([vts[pp][a * dh:(a + 1) * dh, :], ones], axis=0)
            if fixed:
                acc_s[hh, :, qs] = acc_s[hh, :, qs] + _mm(vx, jnp.exp2(st).astype(BF16))
            else:
                m_prev = m_s[hh, :, qs]
                m_new = jnp.maximum(m_prev, jnp.max(st, axis=0, keepdims=True))
                p = jnp.exp2(st - m_new).astype(BF16)
                acc_s[hh, :, qs] = jnp.exp2(m_prev - m_new) * acc_s[hh, :, qs] + _mm(vx, p)
                m_s[hh, :, qs] = m_new

    def key_block(j):
        k0 = pl.multiple_of(j * tk, tk)
        kxs = [jnp.concatenate([pair_lanes(k_ref[pl.ds(k0, tk), :], pp),
                                ka_ref[pp, pl.ds(k0, tk), :]], axis=1) for pp in range(npg)]
        vts = [pair_rows(vt_ref[:, pl.ds(k0, tk)], pp) for pp in range(npg)]
        return kxs, vts

    def run(fixed):
        def body(j, carry):
            kxs, vts = key_block(j)
            update(scores(kxs), vts, False, fixed)
            return carry

        lax.fori_loop(jlo_ref[pl.program_id(0), i], i, body, 0)

        kxs, vts = key_block(i)
        hk = tk // 2
        st_a = [_mm(kxs[hh // 2][:hk], qx_s[hh]) for hh in heads]
        st_b = [_mm(kxs[hh // 2][hk:], qx_s[hh, :, hk:]) for hh in heads]
        st_meta = scores([kme_ref[pp] for pp in range(npg)])
        update(st_a, [v[:, :hk] for v in vts], True, fixed)
        update(st_b, [v[:, hk:] for v in vts], True, fixed, q_lo=hk)
        update(st_meta, [pair_rows(vtm_ref[...], pp) for pp in range(npg)], False, fixed)

    is_fixed = fixed_ref[pl.program_id(0)] != 0
    pl.when(is_fixed)(lambda: run(True))
    pl.when(jnp.logical_not(is_fixed))(lambda: run(False))

    outs = []
    for pp in range(npg):
        halves = []
        for a in range(2):
            acc = acc_s[2 * pp + a]
            halves.append(acc[0:dh] / acc[dh:dh + 1])
        outs.append(jnp.concatenate(halves, axis=0).T)
    o_ref[...] = jnp.concatenate(outs, axis=1).astype(BF16)


def _fox_attn(jlo, fixed, shift3, c3, qt_b, kb, kaux, vt_b, kx_meta, vt_meta, qb_aux, kb_aux,
              vb_aux, n_meta, meta_blk, dh, tq, cscale, npg):
    n, d = kb.shape
    gw = 2 * dh * npg
    ngroups = d // gw
    nl = c3.shape[2]
    mspec = pl.BlockSpec((n_meta, gw), lambda p, i, *_: (meta_blk, p))
    once = dict(pipeline_mode=pl.Buffered(1))
    return pl.pallas_call(
        functools.partial(_fox_attn_body, tq=tq, tk=tq, dh=dh, n_meta=n_meta, cscale=cscale,
                          npg=npg),
        grid_spec=pltpu.PrefetchScalarGridSpec(
            num_scalar_prefetch=2,
            grid=(ngroups, n // tq),
            in_specs=[pl.BlockSpec((2 * npg, 1, nl), lambda p, i, *_: (p, 0, 0)),
                      pl.BlockSpec((2 * npg, 1, LANES), lambda p, i, *_: (p, 0, 0)),
                      pl.BlockSpec((gw, tq), lambda p, i, *_: (p, i)),
                      pl.BlockSpec((n, gw), lambda p, i, *_: (0, p), **once),
                      pl.BlockSpec((npg, n, LANES), lambda p, i, *_: (p, 0, 0), **once),
                      pl.BlockSpec((gw, n), lambda p, i, *_: (p, 0), **once),
                      pl.BlockSpec((npg, n_meta, 4 * dh), lambda p, i, *_: (p, 0, 0)),
                      pl.BlockSpec((gw, n_meta), lambda p, i, *_: (p, 0)),
                      mspec, mspec, mspec],
            out_specs=[pl.BlockSpec((tq, gw), lambda p, i, *_: (i, p)),
                       pl.BlockSpec((n_meta, gw), lambda p, i, *_: (0, p))],
            scratch_shapes=[pltpu.VMEM((2 * npg, 4 * dh, tq), BF16),
                            pltpu.VMEM((2 * npg, 1, tq), F32),
                            pltpu.VMEM((2 * npg, dh + ACC_PAD, tq), F32)]),
        out_shape=[jax.ShapeDtypeStruct((n, d), BF16), jax.ShapeDtypeStruct((n_meta, d), BF16)],
        compiler_params=_params(2),
        name="fox_attn",
    )(jlo, fixed, c3, shift3, qt_b, kb, kaux, vt_b, kx_meta, vt_meta, qb_aux, kb_aux, vb_aux)


PRUNE_NATS = 37.0
NORM_SLACK = 1.01


FIXED_SHIFT_MAX_SPAN = 60.0


def _fox_score_shift(qsq, ksq, group):
    nh = qsq.shape[0]
    u = jnp.sqrt(jnp.max(qsq, axis=1) * jnp.max(ksq, axis=1)) * NORM_SLACK
    fixed = jnp.all((2.0 * u <= FIXED_SHIFT_MAX_SPAN).reshape(nh // group, group), axis=1)
    shift = jnp.where(jnp.repeat(fixed, group), u, 0.0)
    return fixed.astype(jnp.int32), jnp.broadcast_to(shift[:, None, None], (nh, 1, LANES))


def _fox_first_key_block(c_main, qsq, ksq, tq, group):
    nh, n = c_main.shape
    nq = n // tq
    u = jnp.sqrt(jnp.max(qsq, axis=1) * jnp.max(ksq, axis=1)) * (NORM_SLACK / LOG2E)
    c_first = c_main[:, 0::tq]
    c_last = c_main[:, tq - 1::tq]
    bound = 2.0 * u[:, None, None] + c_first[:, :, None] - c_last[:, None, :]
    j_lt_i = jnp.arange(nq)[None, :] < jnp.arange(nq)[:, None]
    skip = (bound <= -PRUNE_NATS) & j_lt_i[None]
    jlo = jnp.sum(skip, axis=2).astype(jnp.int32)
    return jnp.min(jlo.reshape(nh // group, group, nq), axis=1)


def _fox_sample_body(q_ref, kn_ref, vn_ref, cn_ref, kc_ref, vc_ref, suf_ref, o_ref,
                     m_s, l_s, acc_s, *, nh, dh, s_len):
    t = pl.program_id(1)

    @pl.when(t == 0)
    def _():
        r_i = lax.broadcasted_iota(jnp.int32, (s_len, s_len), 0)
        c_i = lax.broadcasted_iota(jnp.int32, (s_len, s_len), 1)
        cols = [slice(h * dh, (h + 1) * dh) for h in range(nh)]
        x = [_mm_nt(q_ref[:, cols[h]], kn_ref[:, cols[h]]) - cn_ref[0, h:h + 1, 0:s_len]
             for h in range(nh)]
        x = [jnp.where(c_i <= r_i, x[h], NEG) for h in range(nh)]
        m = [jnp.max(x[h], axis=-1, keepdims=True) for h in range(nh)]
        p = [jnp.exp(x[h] - m[h]) for h in range(nh)]
        pv = [_mm(p[h].astype(BF16), vn_ref[:, cols[h]]) for h in range(nh)]
        for h in range(nh):
            m_s[h] = m[h]
            l_s[h] = jnp.sum(p[h], axis=-1, keepdims=True)
            acc_s[h] = pv[h]

    heads = range(nh)
    x = [_mm(q_ref[:, h * dh:(h + 1) * dh], kc_ref[0, h].astype(BF16)) + suf_ref[0, h:h + 1, :]
         for h in heads]
    m_prev = [m_s[h] for h in heads]
    m_new = [jnp.maximum(m_prev[h], jnp.max(x[h], axis=-1, keepdims=True)) for h in heads]
    p = [jnp.exp(x[h] - m_new[h]) for h in heads]
    pv = [_mm_nt(p[h].astype(BF16), vc_ref[0, h].astype(BF16)) for h in heads]
    for h in heads:
        alpha = jnp.exp(m_prev[h] - m_new[h])
        l_s[h] = alpha * l_s[h] + jnp.sum(p[h], axis=-1, keepdims=True)
        acc_s[h] = alpha * acc_s[h] + pv[h]
        m_s[h] = m_new[h]

    @pl.when(t == pl.num_programs(1) - 1)
    def _():
        o_ref[...] = jnp.concatenate([acc_s[h] / l_s[h] for h in range(nh)], axis=-1).astype(BF16)


def _fox_sample(qb, kb, vb, cn, kc_t, vc_t, layer, suf, s_len, tk):
    _, nb, nh, dh, past = kc_t.shape
    d = nh * dh
    tok = pl.BlockSpec((s_len, d), lambda b, t: (b, 0))
    cache = pl.BlockSpec((None, 1, nh, dh, tk), lambda b, t: (layer, b, 0, 0, t))
    return pl.pallas_call(
        functools.partial(_fox_sample_body, nh=nh, dh=dh, s_len=s_len),
        grid=(nb, past // tk),
        in_specs=[tok, tok, tok, pl.BlockSpec((1, nh, LANES), lambda b, t: (b, 0, 0)),
                  cache, cache, pl.BlockSpec((1, nh, tk), lambda b, t: (b, 0, t))],
        out_specs=tok,
        out_shape=jax.ShapeDtypeStruct((nb * s_len, d), BF16),
        scratch_shapes=[pltpu.VMEM((nh, s_len, 1), F32), pltpu.VMEM((nh, s_len, 1), F32),
                        pltpu.VMEM((nh, s_len, dh), F32)],
        compiler_params=_params(2),
        name="fox_sample",
    )(qb, kb, vb, cn, kc_t, vc_t, suf)


def _largest_tile(n, cap):
    t = min(n, cap)
    while n % t:
        t //= 2
    return t


def kernel(x_prompt, x_sample, state_hgrn, cache_k, cache_v, cache_logf, meta_tokens,
           norm_pre_mix, norm_post_mix, norm_pre_ffn, norm_post_ffn,
           a_w_in, a_lb_logits, a_g_norm, a_w_out, b_w_in, b_f, b_w_out,
           ffn_w_gu, ffn_w_down):
    batch, seq, d = x_prompt.shape
    assert batch == 1, "one prompt stream per step"
    n_dec, s_len, _ = x_sample.shape
    n_meta = meta_tokens.shape[0]
    depth = norm_pre_mix.shape[0]
    _, _, a_heads, a_dk, a_dv = state_hgrn.shape
    _, _, past, b_heads, b_dh = cache_k.shape
    ns = n_dec * s_len
    assert s_len % HGRN_SUB == 0 and n_meta % HGRN_SUB == 0 and ns % n_meta == 0
    assert 2 * b_dh == LANES and a_dk == LANES and a_dv == LANES

    tm = _largest_tile(seq, 512)
    n_aux = ns + n_meta
    meta_blk = ns // n_meta
    rec_r = _largest_tile(seq, 512)
    rec_c = _largest_tile(rec_r, 64)
    tq = _largest_tile(seq, ATTN_TILE)
    tk_cache = _largest_tile(past, 1024)
    npair = b_heads // 2
    npg = ATTN_PAIRS_PER_STEP if npair % ATTN_PAIRS_PER_STEP == 0 else 1

    sm = jax.nn.softmax(a_lb_logits.astype(F32), axis=0)
    lb_all = jnp.cumsum(sm, axis=0) - sm[0]

    xm = x_prompt.reshape(seq, d)
    xa = jnp.concatenate([x_sample.reshape(ns, d), meta_tokens.astype(F32)], axis=0)
    row = lambda w: w.reshape(1, -1).astype(F32)
    cache_kt = jnp.transpose(cache_k, (0, 1, 3, 4, 2))
    cache_vt = jnp.transpose(cache_v, (0, 1, 3, 4, 2))
    cache_lt = jnp.swapaxes(cache_logf.astype(F32), 2, 3)

    st_p, st_s, k_p, v_p, lf_p, k_s, v_s, lf_s = [], [], [], [], [], [], [], []
    for i in range(depth):
        j = i // 2
        if i % 2 == 0:
            lb = lb_all[j]
            lbp = jnp.stack([jnp.log(jnp.maximum(lb, LB_FLOOR)), jnp.log1p(-lb), 1.0 - lb])
            w_in = a_w_in[j].astype(BF16)
            w_out = a_w_out[j].astype(BF16)
            qm, km, gm, vm, gtm = _hgrn_proj(xm, row(norm_pre_mix[i]), w_in, lbp, tm)
            qa, ka, ga, va, gta = _hgrn_proj(xa, row(norm_pre_mix[i]), w_in, lbp, n_aux)
            o_s, s_s = _hgrn_rec(qa, ka, ga, va, state_hgrn[j].astype(F32),
                                 n_seq=n_dec, seq_len=s_len, r=s_len, c=s_len)
            o_m, s_p, o_meta = _hgrn_rec(qm, km, gm, vm,
                                         jnp.zeros((1, a_heads, a_dk, a_dv), F32),
                                         n_seq=1, seq_len=seq, r=rec_r, c=rec_c,
                                         meta=(qa, ka, ga, va, n_meta, meta_blk))
            st_p.append(s_p.astype(state_hgrn.dtype))
            st_s.append(s_s.astype(state_hgrn.dtype))
            o_a = jnp.concatenate([o_s, o_meta], axis=0)
            mix_m = (o_m, gtm, xm, row(a_g_norm[j]))
            mix_a = (o_a, gta, xa, row(a_g_norm[j]))
            head_dim = a_dv
        else:
            w_t = jnp.swapaxes(b_w_in[j], 0, 1)
            wt_qkv = w_t[:3 * d].astype(BF16)
            w_in = jnp.swapaxes(wt_qkv, 0, 1)
            w_f = jnp.pad(b_w_in[j][:, 3 * d:], ((0, 0), (0, LANES - b_heads))).astype(BF16)
            bf_row = jnp.pad(b_f[j].astype(F32), (0, LANES - b_heads)).reshape(1, LANES)
            w_out = b_w_out[j].astype(BF16)
            scale = float(b_dh) ** -0.5
            kt_m, vt_m, lf_m, qtb_m, kb_m, vtb_m, qsq, ksq = _fox_proj_t(
                xm, row(norm_pre_mix[i]), wt_qkv, w_f, bf_row, b_heads, tm, scale * LOG2E)
            kf_a, vf_a, lf_a, qb_a, kb_a, vb_a = _fox_proj(
                xa, row(norm_pre_mix[i]), w_in, w_f, bf_row, b_heads, n_aux)
            lf_t = jnp.concatenate(
                [jnp.pad(lf_a[ns:].T, ((0, 0), (0, LANES - n_meta))), lf_m.T], axis=1)
            c_all = _lane_cumsum(lf_t, False, b_heads)
            kaux = _fox_kbias(c_all[:, LANES:].T, npair, tm, LOG2E)
            kaux_meta = _fox_kbias(c_all[:, :n_meta].T, npair, n_meta, LOG2E)
            k_meta = jnp.swapaxes(kb_a[ns:].reshape(n_meta, npair, LANES), 0, 1)
            jlo = _fox_first_key_block(c_all[:, LANES:], qsq, ksq, tq, 2 * npg)
            fixed, shift3 = _fox_score_shift(qsq, ksq, 2 * npg)
            o_m, o_meta = _fox_attn(jlo, fixed, shift3, c_all.reshape(b_heads, 1, -1),
                                    qtb_m, kb_m, kaux, vtb_m,
                                    jnp.concatenate([k_meta, kaux_meta], axis=-1), vb_a[ns:].T,
                                    qb_a, kb_a, vb_a, n_meta, meta_blk, b_dh, tq, LOG2E, npg)
            cl_t = cache_lt[j].reshape(n_dec * b_heads, past)
            tr = _largest_tile(n_dec * b_heads, LANES)
            suf = _lane_cumsum(cl_t, True, tr).reshape(n_dec, b_heads, past)
            ln_t = jnp.swapaxes(lf_a[:ns].reshape(n_dec, s_len, b_heads), 1, 2)
            ln_t = jnp.pad(ln_t, ((0, 0), (0, 0), (0, LANES - s_len)))
            cn = _lane_cumsum(ln_t.reshape(n_dec * b_heads, LANES), False, tr)
            o_s = _fox_sample(qb_a, kb_a, vb_a, cn.reshape(n_dec, b_heads, LANES),
                              cache_kt, cache_vt, j, suf, s_len, tk_cache)
            o_a = jnp.concatenate([o_s, o_meta], axis=0)
            mix_m = (o_m, None, xm, None)
            mix_a = (o_a, None, xa, None)
            head_dim = 0
            shp = lambda t, n: t.reshape(-1, n, b_heads, b_dh)
            k_p.append((kf_a[ns:].T, kt_m))
            v_p.append((vf_a[ns:].T, vt_m))
            lf_p.append(jnp.concatenate([lf_a[ns:], lf_m], axis=0).reshape(1, n_meta + seq, b_heads))
            k_s.append(shp(kf_a[:ns], s_len))
            v_s.append(shp(vf_a[:ns], s_len))
            lf_s.append(lf_a[:ns].reshape(n_dec, s_len, b_heads))
        ffn_w = (w_out, row(norm_post_mix[i]), row(norm_pre_ffn[i]), ffn_w_gu[i].astype(BF16),
                 ffn_w_down[i].astype(BF16), row(norm_post_ffn[i]))
        xm = _mix_ffn(*mix_m, *ffn_w, tm, head_dim)
        xa = _mix_ffn(*mix_a, *ffn_w, n_aux, head_dim)

    def prompt_kv(parts):
        t = jnp.concatenate([jnp.stack([m for m, _ in parts]), jnp.stack([x for _, x in parts])],
                            axis=2)
        t = t.reshape(len(parts), b_heads, b_dh, n_meta + seq)
        return jnp.transpose(t, (0, 3, 1, 2))[:, None]

    return (xm.reshape(1, seq, d), xa[:ns].reshape(n_dec, s_len, d),
            jnp.stack(st_p), prompt_kv(k_p), prompt_kv(v_p), jnp.stack(lf_p),
            jnp.stack(st_s), jnp.stack(k_s), jnp.stack(v_s), jnp.stack(lf_s))
```

```python
import functools

import jax
import jax.numpy as jnp
from jax import lax
from jax.experimental import pallas as pl
from jax.experimental.pallas import tpu as pltpu

F32 = jnp.float32
BF16 = jnp.bfloat16
EPS = 1e-6
NEG = -1e30
LB_FLOOR = 1e-30

LANES = 128
VMEM_LIMIT = 56 * 1024 * 1024
HGRN_SUB = 16
HIGHEST = lax.Precision.HIGHEST
LOG2E = 1.4426950408889634


def _params(n_grid):
    return pltpu.CompilerParams(dimension_semantics=("arbitrary",) * n_grid,
                                vmem_limit_bytes=VMEM_LIMIT)


def _resident(shape):
    nd = len(shape)
    return pl.BlockSpec(shape, lambda *_: (0,) * nd, pipeline_mode=pl.Buffered(1))


def _layer_of(stack, layer, rows=None):
    _, r, c = stack.shape
    return pl.BlockSpec((None, rows or r, c), lambda *_: (layer, 0, 0),
                        pipeline_mode=pl.Buffered(1))


def _rmsnorm(x, w):
    return x * lax.rsqrt(jnp.mean(x * x, axis=-1, keepdims=True) + EPS) * w


def _sigmoid(x):
    return 1.0 / (1.0 + jnp.exp(-x))


def _log1p_exp(z):
    return jnp.log(1.0 + jnp.exp(z))


def _log_sigmoid(x):
    return jnp.minimum(x, 0.0) - _log1p_exp(-jnp.abs(x))


def _mm(a, b):
    return jnp.dot(a, b, preferred_element_type=F32)


def _mm_nt(a, b):
    return lax.dot_general(a, b, (((1,), (1,)), ((), ())), preferred_element_type=F32)


def _mm_tn(a, b):
    return lax.dot_general(a, b, (((0,), (0,)), ((), ())), preferred_element_type=F32)


def _hgrn_proj_body(x_ref, nw_ref, w_ref, lb_ref, q_ref, k_ref, g_ref, v_ref, gate_ref):
    f = q_ref.shape[1]
    d = v_ref.shape[1]
    xn = _rmsnorm(x_ref[...], nw_ref[...]).astype(BF16)
    q = _mm(xn, w_ref[:, 0:f])
    q_ref[...] = q * _sigmoid(q)
    fl = _mm(xn, w_ref[:, f:2 * f])
    e = jnp.exp(-jnp.abs(fl))
    r = 1.0 / (1.0 + e)
    a = lb_ref[0:1, :]
    b = lb_ref[1:2, :] + (jnp.minimum(fl, 0.0) - jnp.log(1.0 + e))
    g_ref[...] = jnp.maximum(a, b) + _log1p_exp(-jnp.abs(a - b))
    k_ref[...] = lb_ref[2:3, :] * jnp.where(fl >= 0.0, e * r, r)
    v_ref[...] = _mm(xn, w_ref[:, 2 * f:2 * f + d])
    gt = _mm(xn, w_ref[:, 2 * f + d:])
    gate_ref[...] = gt * _sigmoid(gt)


def _hgrn_proj(x, nw, w_stack, layer, lbp, tm):
    n, d = x.shape
    f = lbp.shape[1]
    row = lambda w: pl.BlockSpec((tm, w), lambda i: (i, 0))
    return pl.pallas_call(
        _hgrn_proj_body,
        grid=(n // tm,),
        in_specs=[row(d), _resident((1, d)), _layer_of(w_stack, layer), _resident(lbp.shape)],
        out_specs=[row(f), row(f), row(f), row(d), row(d)],
        out_shape=[jax.ShapeDtypeStruct((n, w), F32) for w in (f, f, f, d, d)],
        compiler_params=_params(1),
        name="hgrn_proj",
    )(x, nw, w_stack, lbp)


def _hgrn_chunk(q, k, g, v, sts, c):
    nh = len(sts)
    sub, half = HGRN_SUB, HGRN_SUB // 2
    hs = lambda x, h: x[:, h * LANES:(h + 1) * LANES]
    r_i = lax.broadcasted_iota(jnp.int32, (c, c), 0)
    c_i = lax.broadcasted_iota(jnp.int32, (c, c), 1)
    tril = jnp.where(r_i >= c_i, 1.0, 0.0).astype(F32)
    gc = jnp.dot(tril, g, precision=HIGHEST, preferred_element_type=F32) * LOG2E
    g_last = gc[c - 1:c, :]
    qe = (q * jnp.exp2(gc)).astype(BF16)
    o_inter = [_mm_nt(hs(qe, h), sts[h].astype(BF16)) for h in range(nh)]
    kd = (k * jnp.exp2(g_last - gc)).astype(BF16)
    vb = v.astype(BF16)
    st_dec = jnp.exp2(g_last)
    st_new = [sts[h] * hs(st_dec, h) + _mm_tn(hs(vb, h), hs(kd, h)) for h in range(nh)]

    lk = jnp.log2(k) - gc
    half_row = lax.broadcasted_iota(jnp.int32, (half, 1), 0)
    out_rows = []
    for i in range(c // sub):
        r0 = i * sub
        gi, qi, vi, lki = gc[r0:r0 + sub], q[r0:r0 + sub], v[r0:r0 + sub], lk[r0:r0 + sub]
        o_lo = [o_inter[h][r0:r0 + half] for h in range(nh)]
        o_hi = [o_inter[h][r0 + half:r0 + sub] for h in range(nh)]
        if i > 0:
            g_first = gc[r0:r0 + 1]
            qs = (qi * jnp.exp2(gi - g_first)).astype(BF16)
            ks = (k[:r0] * jnp.exp2(g_first - gc[:r0])).astype(BF16)
            att = [_mm_nt(hs(qs, h), hs(ks, h)).astype(BF16) for h in range(nh)]
            off = [_mm(att[h], hs(vb[:r0], h)) for h in range(nh)]
            o_lo = [o_lo[h] + off[h][:half] for h in range(nh)]
            o_hi = [o_hi[h] + off[h][half:] for h in range(nh)]
        for rows, acc, s_range in ((slice(0, half), o_lo, range(half)),
                                   (slice(half, sub), o_hi, range(sub))):
            g_rows, q_rows = gi[rows], qi[rows]
            for s in s_range:
                diff = g_rows + lki[s:s + 1]
                if s >= rows.start:
                    diff = jnp.where(half_row >= s - rows.start, diff, NEG)
                prod = q_rows * jnp.exp2(diff)
                for h in range(nh):
                    col = jnp.sum(hs(prod, h), axis=-1, keepdims=True)
                    acc[h] = acc[h] + col * hs(vi, h)[s:s + 1]
        out_rows.append(jnp.concatenate(
            [jnp.concatenate([o_lo[h], o_hi[h]], axis=0) for h in range(nh)], axis=1))
    o = jnp.concatenate(out_rows, axis=0) if len(out_rows) > 1 else out_rows[0]
    return o, st_new


def _hgrn_rec_body(*refs, c, r, has_meta):
    if has_meta:
        (q_ref, k_ref, g_ref, v_ref, s0_ref, qm_ref, km_ref, gm_ref, vm_ref,
         o_ref, s_ref, om_ref, st_ref) = refs
    else:
        q_ref, k_ref, g_ref, v_ref, s0_ref, o_ref, s_ref, st_ref = refs
    blk = pl.program_id(1)
    nh = st_ref.shape[0]

    @pl.when(blk == 0)
    def _():
        sts = [s0_ref[0, h].T for h in range(nh)]
        if has_meta:
            o_m, sts = _hgrn_chunk(qm_ref[...], km_ref[...], gm_ref[...], vm_ref[...],
                                   sts, qm_ref.shape[0])
            om_ref[...] = o_m
        for h in range(nh):
            st_ref[h] = sts[h]

    def body(j, carry):
        sl = pl.ds(pl.multiple_of(j * c, c), c)
        o, sts = _hgrn_chunk(q_ref[sl, :], k_ref[sl, :], g_ref[sl, :], v_ref[sl, :],
                             [st_ref[h] for h in range(nh)], c)
        o_ref[sl, :] = o
        for h in range(nh):
            st_ref[h] = sts[h]
        return carry

    lax.fori_loop(0, r // c, body, 0)

    @pl.when(blk == pl.num_programs(1) - 1)
    def _():
        for h in range(nh):
            s_ref[0, h] = st_ref[h].T


def _hgrn_rec(q, k, g, v, s0, *, n_seq, seq_len, r, c, meta=None, layer=None):
    h, dk, dv = s0.shape[-3:]
    nblk = seq_len // r
    tok = lambda w: pl.BlockSpec((r, h * w), lambda b, i: (b * nblk + i, 0))
    st_spec = pl.BlockSpec((1, h, dk, dv), lambda b, i: (b, 0, 0, 0))
    s0_spec = st_spec if layer is None else pl.BlockSpec(
        (None, 1, h, dk, dv), lambda b, i: (layer, b, 0, 0, 0))
    in_specs = [tok(dk), tok(dk), tok(dk), tok(dv), s0_spec]
    args = [q, k, g, v, s0]
    out_specs = [tok(dv), st_spec]
    out_shape = [jax.ShapeDtypeStruct((n_seq * seq_len, h * dv), F32),
                 jax.ShapeDtypeStruct((n_seq, h, dk, dv), F32)]
    if meta is not None:
        qm, km, gm, vm, n_meta, meta_blk = meta
        mspec = lambda w: pl.BlockSpec((n_meta, h * w), lambda b, i: (meta_blk, 0))
        in_specs += [mspec(dk), mspec(dk), mspec(dk), mspec(dv)]
        args += [qm, km, gm, vm]
        out_specs.append(pl.BlockSpec((n_meta, h * dv), lambda b, i: (0, 0)))
        out_shape.append(jax.ShapeDtypeStruct((n_meta, h * dv), F32))
    return pl.pallas_call(
        functools.partial(_hgrn_rec_body, c=c, r=r, has_meta=meta is not None),
        grid=(n_seq, nblk),
        in_specs=in_specs,
        out_specs=out_specs,
        out_shape=out_shape,
        scratch_shapes=[pltpu.VMEM((h, dv, dk), F32)],
        compiler_params=_params(2),
        name="hgrn_rec",
    )(*args)


def _mix_ffn_body(*refs, head_dim, dff, fc):
    if head_dim:
        (o_ref, gate_ref, x_ref, gn_ref, wo_ref, pm_ref,
         nf_ref, wgu_ref, wd_ref, pf_ref, y_ref) = refs
        o = o_ref[...]
        parts = []
        for h in range(o.shape[1] // head_dim):
            oh = o[:, h * head_dim:(h + 1) * head_dim]
            parts.append(oh * lax.rsqrt(jnp.mean(oh * oh, axis=-1, keepdims=True) + EPS))
        o = (jnp.concatenate(parts, axis=-1) * gn_ref[...] * gate_ref[...]).astype(BF16)
    else:
        o_ref, x_ref, wo_ref, pm_ref, nf_ref, wgu_ref, wd_ref, pf_ref, y_ref = refs
        o = o_ref[...]
    x = x_ref[...] + _rmsnorm(_mm(o, wo_ref[...]), pm_ref[...])
    xn = _rmsnorm(x, nf_ref[...]).astype(BF16)
    acc = jnp.zeros(x.shape, F32)
    for c0 in range(0, dff, fc):
        a = _mm(xn, wgu_ref[:, c0:c0 + fc])
        u = _mm(xn, wgu_ref[:, dff + c0:dff + c0 + fc])
        hid = (a * _sigmoid(a) * u).astype(BF16)
        acc = acc + _mm(hid, wd_ref[c0:c0 + fc, :])
    y_ref[...] = x + _rmsnorm(acc, pf_ref[...])


def _mix_ffn(o, gate, x, gn, wo_stack, mix_layer, pm, nf, wgu_stack, wd_stack, ffn_layer, pf,
             tm, head_dim):
    n, d = x.shape
    dff = wd_stack.shape[1]
    fc = dff // 2 if (dff // 2) % LANES == 0 else dff
    row = pl.BlockSpec((tm, d), lambda i: (i, 0))
    vec = _resident((1, d))
    ffn_specs = [vec, _layer_of(wgu_stack, ffn_layer), _layer_of(wd_stack, ffn_layer), vec]
    wo_spec = _layer_of(wo_stack, mix_layer)
    if head_dim:
        in_specs = [row, row, row, vec, wo_spec, vec] + ffn_specs
        args = (o, gate, x, gn, wo_stack, pm, nf, wgu_stack, wd_stack, pf)
    else:
        in_specs = [row, row, wo_spec, vec] + ffn_specs
        args = (o, x, wo_stack, pm, nf, wgu_stack, wd_stack, pf)
    return pl.pallas_call(
        functools.partial(_mix_ffn_body, head_dim=head_dim, dff=dff, fc=fc),
        grid=(n // tm,),
        in_specs=in_specs,
        out_specs=row,
        out_shape=jax.ShapeDtypeStruct((n, d), F32),
        compiler_params=_params(1),
        name="mix_ffn",
    )(*args)


def _fox_proj_body(x_ref, nw_ref, w_ref, wf_ref, bf_ref, k_ref, v_ref, lf_ref,
                   qb_ref, kb_ref, vb_ref, *, scale):
    d = k_ref.shape[1]
    nh = lf_ref.shape[1]
    xn = _rmsnorm(x_ref[...], nw_ref[...]).astype(BF16)
    qb_ref[...] = (_mm(xn, w_ref[:, 0:d]) * scale).astype(BF16)
    k = _mm(xn, w_ref[:, d:2 * d])
    k_ref[...] = k
    kb_ref[...] = k.astype(BF16)
    v = _mm(xn, w_ref[:, 2 * d:3 * d])
    v_ref[...] = v
    vb_ref[...] = v.astype(BF16)
    lf = _log_sigmoid(_mm(xn, wf_ref[...]) + bf_ref[...])
    lf_ref[...] = lf[:, 0:nh]


def _fox_proj(x, nw, w_bf, wf_bf, bf_row, nh, tm):
    n, d = x.shape
    row = lambda w: pl.BlockSpec((tm, w), lambda i: (i, 0))
    scale = float(d // nh) ** -0.5
    return pl.pallas_call(
        functools.partial(_fox_proj_body, scale=scale),
        grid=(n // tm,),
        in_specs=[row(d), _resident((1, d)), _resident(w_bf.shape), _resident(wf_bf.shape),
                  _resident(bf_row.shape)],
        out_specs=[row(d), row(d), row(nh), row(d), row(d), row(d)],
        out_shape=[jax.ShapeDtypeStruct((n, d), F32), jax.ShapeDtypeStruct((n, d), F32),
                   jax.ShapeDtypeStruct((n, nh), F32), jax.ShapeDtypeStruct((n, d), BF16),
                   jax.ShapeDtypeStruct((n, d), BF16), jax.ShapeDtypeStruct((n, d), BF16)],
        compiler_params=_params(1),
        name="fox_proj",
    )(x, nw, w_bf, wf_bf, bf_row)


def _head_sq_norms(xt, nh):
    dh = xt.shape[0] // nh
    x2 = xt * xt
    return jnp.concatenate([jnp.sum(x2[h * dh:(h + 1) * dh], axis=0, keepdims=True)
                            for h in range(nh)], axis=0)


def _fox_proj_t_body(x_ref, nw_ref, wt_ref, wf_ref, bf_ref, kt_ref, vt_ref, lf_ref,
                     qtb_ref, kb_ref, vtb_ref, qsq_ref, ksq_ref, *, qscale):
    d = kt_ref.shape[0]
    nh = lf_ref.shape[1]
    xn = _rmsnorm(x_ref[...], nw_ref[...]).astype(BF16)
    qtb = (_mm_nt(wt_ref[0:d, :], xn) * qscale).astype(BF16)
    qtb_ref[...] = qtb
    qsq_ref[...] = _head_sq_norms(qtb.astype(F32), nh)
    kt = _mm_nt(wt_ref[d:2 * d, :], xn)
    kt_ref[...] = kt
    ksq_ref[...] = _head_sq_norms(kt.astype(BF16).astype(F32), nh)
    kb_ref[...] = kt.T.astype(BF16)
    vt = _mm_nt(wt_ref[2 * d:3 * d, :], xn)
    vt_ref[...] = vt
    vtb_ref[...] = vt.astype(BF16)
    lf = _log_sigmoid(_mm(xn, wf_ref[...]) + bf_ref[...])
    lf_ref[...] = lf[:, 0:nh]


def _fox_proj_t(x, nw, wt_stack, layer, wf_bf, bf_row, nh, tm, qscale):
    n, d = x.shape
    row = lambda w: pl.BlockSpec((tm, w), lambda i: (i, 0))
    col = pl.BlockSpec((d, tm), lambda i: (0, i))
    sq = pl.BlockSpec((nh, tm), lambda i: (0, i))
    return pl.pallas_call(
        functools.partial(_fox_proj_t_body, qscale=qscale),
        grid=(n // tm,),
        in_specs=[row(d), _resident((1, d)), _layer_of(wt_stack, layer, 3 * d),
                  _resident(wf_bf.shape), _resident(bf_row.shape)],
        out_specs=[col, col, row(nh), col, row(d), col, sq, sq],
        out_shape=[jax.ShapeDtypeStruct((d, n), F32), jax.ShapeDtypeStruct((d, n), F32),
                   jax.ShapeDtypeStruct((n, nh), F32), jax.ShapeDtypeStruct((d, n), BF16),
                   jax.ShapeDtypeStruct((n, d), BF16), jax.ShapeDtypeStruct((d, n), BF16),
                   jax.ShapeDtypeStruct((nh, n), F32), jax.ShapeDtypeStruct((nh, n), F32)],
        compiler_params=_params(1),
        name="fox_proj_t",
    )(x, nw, wt_stack, wf_bf, bf_row)


BIAS_LANES_PER_HEAD = 6


def _split3(v):
    hi = v.astype(BF16).astype(F32)
    r1 = v - hi
    mid = r1.astype(BF16).astype(F32)
    lo = (r1 - mid).astype(BF16).astype(F32)
    return hi, mid, lo


def _fox_kbias_body(c4_ref, aux_ref, *, cscale, nh):
    tm = c4_ref.shape[0]
    npair = aux_ref.shape[0]
    hi, mid, lo = _split3(c4_ref[...] * (-cscale))
    group = lax.broadcasted_iota(jnp.int32, (tm, 4 * nh), 1) // nh
    src = jnp.where(group == 0, hi, jnp.where(group == 1, mid, jnp.where(group == 2, lo, 1.0)))
    row = lax.broadcasted_iota(jnp.int32, (4 * nh, npair * LANES), 0)
    col = lax.broadcasted_iota(jnp.int32, (4 * nh, npair * LANES), 1)
    lane = col % LANES
    slot = lane % BIAS_LANES_PER_HEAD
    head = 2 * (col // LANES) + lane // BIAS_LANES_PER_HEAD
    pick = (lane < 2 * BIAS_LANES_PER_HEAD) & (row == jnp.minimum(slot, 3) * nh + head)
    out = _mm(src.astype(BF16), jnp.where(pick, 1.0, 0.0).astype(BF16))
    for p in range(npair):
        aux_ref[p] = out[:, p * LANES:(p + 1) * LANES].astype(BF16)


def _fox_kbias(c_rows, npair, tm, cscale):
    n, nh = c_rows.shape
    return pl.pallas_call(
        functools.partial(_fox_kbias_body, cscale=cscale, nh=nh),
        grid=(n // tm,),
        in_specs=[pl.BlockSpec((tm, 4 * nh), lambda i: (i, 0))],
        out_specs=pl.BlockSpec((npair, tm, LANES), lambda i: (0, i, 0)),
        out_shape=jax.ShapeDtypeStruct((npair, n, LANES), BF16),
        compiler_params=_params(1),
        name="fox_kbias",
    )(jnp.tile(c_rows, (1, 4)))


def _lane_cumsum_body(x_ref, y_ref, *, reverse_exclusive):
    rows, n = x_ref.shape
    nb = n // LANES
    t_i = lax.broadcasted_iota(jnp.int32, (LANES, LANES), 0)
    k_i = lax.broadcasted_iota(jnp.int32, (LANES, LANES), 1)
    sel = (t_i > k_i) if reverse_exclusive else (t_i <= k_i)
    tri = jnp.where(sel, 1.0, 0.0).astype(F32)

    def body(i, carry):
        b = (nb - 1 - i) if reverse_exclusive else i
        sl = slice(b * LANES, (b + 1) * LANES)
        xb = x_ref[:, sl]
        y_ref[:, sl] = jnp.dot(xb, tri, precision=HIGHEST, preferred_element_type=F32) + carry
        return carry + jnp.sum(xb, axis=-1, keepdims=True)

    carry = jnp.zeros((rows, 1), F32)
    for i in range(nb):
        carry = body(i, carry)


def _lane_cumsum(x, reverse_exclusive, tr):
    rows, n = x.shape
    spec = pl.BlockSpec((tr, n), lambda i: (i, 0))
    return pl.pallas_call(
        functools.partial(_lane_cumsum_body, reverse_exclusive=reverse_exclusive),
        grid=(rows // tr,),
        in_specs=[spec],
        out_specs=spec,
        out_shape=jax.ShapeDtypeStruct((rows, n), F32),
        compiler_params=_params(1),
        name="lane_cumsum",
    )(x)


ATTN_TILE = 512
ATTN_PAIRS_PER_STEP = 2
ACC_PAD = 16


def _fox_attn_body(jlo_ref, fixed_ref, c_ref, sh_ref, qt_ref, k_ref, ka_ref, vt_ref, kme_ref,
                   vtm_ref, qm_ref, km_ref, vm_ref, o_ref, om_ref, qx_s, m_s, acc_s,
                   *, tq, tk, dh, n_meta, cscale, npg):
    i = pl.program_id(1)
    heads = range(2 * npg)
    pair_lanes = lambda x, pp: x[:, pp * 2 * dh:(pp + 1) * 2 * dh]
    pair_rows = lambda x, pp: x[pp * 2 * dh:(pp + 1) * 2 * dh, :]
    lane = lax.broadcasted_iota(jnp.int32, (1, 2 * dh), 1)
    head_sel = [lane < dh, lane >= dh]

    @pl.when(i == 0)
    def _():
        r_i = lax.broadcasted_iota(jnp.int32, (n_meta, n_meta), 0)
        c_i = lax.broadcasted_iota(jnp.int32, (n_meta, n_meta), 1)
        outs = []
        for pp in range(npg):
            qm, km, vm = (pair_lanes(r[...], pp) for r in (qm_ref, km_ref, vm_ref))
            out = jnp.zeros((n_meta, 2 * dh), F32)
            for a in range(2):
                qa = jnp.where(head_sel[a], qm, jnp.zeros_like(qm))
                x = _mm_nt(qa, km) - c_ref[2 * pp + a, :, 0:n_meta]
                x = jnp.where(c_i <= r_i, x, NEG)
                p = jnp.exp(x - jnp.max(x, axis=-1, keepdims=True))
                oa = _mm(p.astype(BF16), vm) / jnp.sum(p, axis=-1, keepdims=True)
                out = jnp.where(head_sel[a], oa, out)
            outs.append(out)
        om_ref[...] = jnp.concatenate(outs, axis=1).astype(BF16)

    q0 = pl.multiple_of(LANES + i * tq, LANES)
    row = lax.broadcasted_iota(jnp.int32, (2 * dh, tq), 0)
    for hh in heads:
        pp, a = divmod(hh, 2)
        qt = pair_rows(qt_ref[...], pp)
        hi, mid, lo = _split3(c_ref[hh, :, pl.ds(q0, tq)] * cscale - sh_ref[hh, :, 0:1])
        slot = row - a * BIAS_LANES_PER_HEAD
        bias = jnp.where(slot == 3, hi, jnp.where(slot == 4, mid, jnp.where(slot == 5, lo, 1.0)))
        bias = jnp.where((slot >= 0) & (slot < BIAS_LANES_PER_HEAD), bias, 0.0)
        qx_s[hh, 0:2 * dh, :] = jnp.where((row >= a * dh) & (row < (a + 1) * dh), qt,
                                          jnp.zeros_like(qt))
        qx_s[hh, 2 * dh:4 * dh, :] = bias.astype(BF16)
        m_s[hh] = jnp.full((1, tq), NEG, F32)
        acc_s[hh] = jnp.zeros((dh + ACC_PAD, tq), F32)

    def scores(kxs):
        return [_mm(kxs[hh // 2], qx_s[hh]) for hh in heads]

    def update(sts, vts, masked, fixed, q_lo=0):
        tkb, tqb = sts[0].shape
        qs = slice(q_lo, q_lo + tqb)
        ones = jnp.where(lax.broadcasted_iota(jnp.int32, (ACC_PAD, tkb), 0) == 0,
                         1.0, 0.0).astype(BF16)
        for hh in heads:
            pp, a = divmod(hh, 2)
            st = sts[hh]
            if masked:
                k_i = lax.broadcasted_iota(jnp.int32, (tkb, tqb), 0)
                q_i = lax.broadcasted_iota(jnp.int32, (tkb, tqb), 1)
                st = jnp.where(k_i <= q_i, st, NEG)
            vx = jnp.concatenate([vts[pp][a * dh:(a + 1) * dh, :], ones], axis=0)
            if fixed:
                acc_s[hh, :, qs] = acc_s[hh, :, qs] + _mm(vx, jnp.exp2(st).astype(BF16))
            else:
                m_prev = m_s[hh, :, qs]
                m_new = jnp.maximum(m_prev, jnp.max(st, axis=0, keepdims=True))
                p = jnp.exp2(st - m_new).astype(BF16)
                acc_s[hh, :, qs] = jnp.exp2(m_prev - m_new) * acc_s[hh, :, qs] + _mm(vx, p)
                m_s[hh, :, qs] = m_new

    def key_block(j):
        k0 = pl.multiple_of(j * tk, tk)
        kxs = [jnp.concatenate([pair_lanes(k_ref[pl.ds(k0, tk), :], pp),
                                ka_ref[pp, pl.ds(k0, tk), :]], axis=1) for pp in range(npg)]
        vts = [pair_rows(vt_ref[:, pl.ds(k0, tk)], pp) for pp in range(npg)]
        return kxs, vts

    def run(fixed):
        def body(j, carry):
            kxs, vts = key_block(j)
            update(scores(kxs), vts, False, fixed)
            return carry

        lax.fori_loop(jlo_ref[pl.program_id(0), i], i, body, 0)

        kxs, vts = key_block(i)
        hk = tk // 2
        st_a = [_mm(kxs[hh // 2][:hk], qx_s[hh]) for hh in heads]
        st_b = [_mm(kxs[hh // 2][hk:], qx_s[hh, :, hk:]) for hh in heads]
        st_meta = scores([kme_ref[pp] for pp in range(npg)])
        update(st_a, [v[:, :hk] for v in vts], True, fixed)
        update(st_b, [v[:, hk:] for v in vts], True, fixed, q_lo=hk)
        update(st_meta, [pair_rows(vtm_ref[...], pp) for pp in range(npg)], False, fixed)

    is_fixed = fixed_ref[pl.program_id(0)] != 0
    pl.when(is_fixed)(lambda: run(True))
    pl.when(jnp.logical_not(is_fixed))(lambda: run(False))

    outs = []
    for pp in range(npg):
        halves = []
        for a in range(2):
            acc = acc_s[2 * pp + a]
            halves.append(acc[0:dh] / acc[dh:dh + 1])
        outs.append(jnp.concatenate(halves, axis=0).T)
    o_ref[...] = jnp.concatenate(outs, axis=1).astype(BF16)


def _fox_attn(jlo, fixed, shift3, c3, qt_b, kb, kaux, vt_b, kx_meta, vt_meta, qb_aux, kb_aux,
              vb_aux, n_meta, meta_blk, dh, tq, cscale, npg):
    n, d = kb.shape
    gw = 2 * dh * npg
    ngroups = d // gw
    nl = c3.shape[2]
    mspec = pl.BlockSpec((n_meta, gw), lambda p, i, *_: (meta_blk, p))
    once = dict(pipeline_mode=pl.Buffered(1))
    return pl.pallas_call(
        functools.partial(_fox_attn_body, tq=tq, tk=tq, dh=dh, n_meta=n_meta, cscale=cscale,
                          npg=npg),
        grid_spec=pltpu.PrefetchScalarGridSpec(
            num_scalar_prefetch=2,
            grid=(ngroups, n // tq),
            in_specs=[pl.BlockSpec((2 * npg, 1, nl), lambda p, i, *_: (p, 0, 0)),
                      pl.BlockSpec((2 * npg, 1, LANES), lambda p, i, *_: (p, 0, 0)),
                      pl.BlockSpec((gw, tq), lambda p, i, *_: (p, i)),
                      pl.BlockSpec((n, gw), lambda p, i, *_: (0, p), **once),
                      pl.BlockSpec((npg, n, LANES), lambda p, i, *_: (p, 0, 0), **once),
                      pl.BlockSpec((gw, n), lambda p, i, *_: (p, 0), **once),
                      pl.BlockSpec((npg, n_meta, 4 * dh), lambda p, i, *_: (p, 0, 0)),
                      pl.BlockSpec((gw, n_meta), lambda p, i, *_: (p, 0)),
                      mspec, mspec, mspec],
            out_specs=[pl.BlockSpec((tq, gw), lambda p, i, *_: (i, p)),
                       pl.BlockSpec((n_meta, gw), lambda p, i, *_: (0, p))],
            scratch_shapes=[pltpu.VMEM((2 * npg, 4 * dh, tq), BF16),
                            pltpu.VMEM((2 * npg, 1, tq), F32),
                            pltpu.VMEM((2 * npg, dh + ACC_PAD, tq), F32)]),
        out_shape=[jax.ShapeDtypeStruct((n, d), BF16), jax.ShapeDtypeStruct((n_meta, d), BF16)],
        compiler_params=_params(2),
        name="fox_attn",
    )(jlo, fixed, c3, shift3, qt_b, kb, kaux, vt_b, kx_meta, vt_meta, qb_aux, kb_aux, vb_aux)


PRUNE_NATS = 37.0
NORM_SLACK = 1.01


FIXED_SHIFT_MAX_SPAN = 60.0


def _fox_score_shift(qsq, ksq, group):
    nh = qsq.shape[0]
    u = jnp.sqrt(jnp.max(qsq, axis=1) * jnp.max(ksq, axis=1)) * NORM_SLACK
    fixed = jnp.all((2.0 * u <= FIXED_SHIFT_MAX_SPAN).reshape(nh // group, group), axis=1)
    shift = jnp.where(jnp.repeat(fixed, group), u, 0.0)
    return fixed.astype(jnp.int32), jnp.broadcast_to(shift[:, None, None], (nh, 1, LANES))


def _fox_first_key_block(c_main, qsq, ksq, tq, group):
    nh, n = c_main.shape
    nq = n // tq
    u = jnp.sqrt(jnp.max(qsq, axis=1) * jnp.max(ksq, axis=1)) * (NORM_SLACK / LOG2E)
    c_first = c_main[:, 0::tq]
    c_last = c_main[:, tq - 1::tq]
    bound = 2.0 * u[:, None, None] + c_first[:, :, None] - c_last[:, None, :]
    j_lt_i = jnp.arange(nq)[None, :] < jnp.arange(nq)[:, None]
    skip = (bound <= -PRUNE_NATS) & j_lt_i[None]
    jlo = jnp.sum(skip, axis=2).astype(jnp.int32)
    return jnp.min(jlo.reshape(nh // group, group, nq), axis=1)


def _fox_sample_body(q_ref, kn_ref, vn_ref, cn_ref, kc_ref, vc_ref, suf_ref, o_ref,
                     m_s, l_s, acc_s, *, nh, dh, s_len):
    t = pl.program_id(1)

    @pl.when(t == 0)
    def _():
        r_i = lax.broadcasted_iota(jnp.int32, (s_len, s_len), 0)
        c_i = lax.broadcasted_iota(jnp.int32, (s_len, s_len), 1)
        cols = [slice(h * dh, (h + 1) * dh) for h in range(nh)]
        x = [_mm_nt(q_ref[:, cols[h]], kn_ref[:, cols[h]]) - cn_ref[0, h:h + 1, 0:s_len]
             for h in range(nh)]
        x = [jnp.where(c_i <= r_i, x[h], NEG) for h in range(nh)]
        m = [jnp.max(x[h], axis=-1, keepdims=True) for h in range(nh)]
        p = [jnp.exp(x[h] - m[h]) for h in range(nh)]
        pv = [_mm(p[h].astype(BF16), vn_ref[:, cols[h]]) for h in range(nh)]
        for h in range(nh):
            m_s[h] = m[h]
            l_s[h] = jnp.sum(p[h], axis=-1, keepdims=True)
            acc_s[h] = pv[h]

    heads = range(nh)
    x = [_mm(q_ref[:, h * dh:(h + 1) * dh], kc_ref[0, h].astype(BF16)) + suf_ref[0, h:h + 1, :]
         for h in heads]
    m_prev = [m_s[h] for h in heads]
    m_new = [jnp.maximum(m_prev[h], jnp.max(x[h], axis=-1, keepdims=True)) for h in heads]
    p = [jnp.exp(x[h] - m_new[h]) for h in heads]
    pv = [_mm_nt(p[h].astype(BF16), vc_ref[0, h].astype(BF16)) for h in heads]
    for h in heads:
        alpha = jnp.exp(m_prev[h] - m_new[h])
        l_s[h] = alpha * l_s[h] + jnp.sum(p[h], axis=-1, keepdims=True)
        acc_s[h] = alpha * acc_s[h] + pv[h]
        m_s[h] = m_new[h]

    @pl.when(t == pl.num_programs(1) - 1)
    def _():
        o_ref[...] = jnp.concatenate([acc_s[h] / l_s[h] for h in range(nh)], axis=-1).astype(BF16)


def _fox_sample(qb, kb, vb, cn, kc_t, vc_t, layer, suf, s_len, tk):
    _, nb, nh, dh, past = kc_t.shape
    d = nh * dh
    tok = pl.BlockSpec((s_len, d), lambda b, t: (b, 0))
    cache = pl.BlockSpec((None, 1, nh, dh, tk), lambda b, t: (layer, b, 0, 0, t))
    return pl.pallas_call(
        functools.partial(_fox_sample_body, nh=nh, dh=dh, s_len=s_len),
        grid=(nb, past // tk),
        in_specs=[tok, tok, tok, pl.BlockSpec((1, nh, LANES), lambda b, t: (b, 0, 0)),
                  cache, cache, pl.BlockSpec((1, nh, tk), lambda b, t: (b, 0, t))],
        out_specs=tok,
        out_shape=jax.ShapeDtypeStruct((nb * s_len, d), BF16),
        scratch_shapes=[pltpu.VMEM((nh, s_len, 1), F32), pltpu.VMEM((nh, s_len, 1), F32),
                        pltpu.VMEM((nh, s_len, dh), F32)],
        compiler_params=_params(2),
        name="fox_sample",
    )(qb, kb, vb, cn, kc_t, vc_t, suf)


def _largest_tile(n, cap):
    t = min(n, cap)
    while n % t:
        t //= 2
    return t


def kernel(x_prompt, x_sample, state_hgrn, cache_k, cache_v, cache_logf, meta_tokens,
           norm_pre_mix, norm_post_mix, norm_pre_ffn, norm_post_ffn,
           a_w_in, a_lb_logits, a_g_norm, a_w_out, b_w_in, b_f, b_w_out,
           ffn_w_gu, ffn_w_down):
    batch, seq, d = x_prompt.shape
    assert batch == 1, "one prompt stream per step"
    n_dec, s_len, _ = x_sample.shape
    n_meta = meta_tokens.shape[0]
    depth = norm_pre_mix.shape[0]
    _, _, a_heads, a_dk, a_dv = state_hgrn.shape
    _, _, past, b_heads, b_dh = cache_k.shape
    ns = n_dec * s_len
    assert s_len % HGRN_SUB == 0 and n_meta % HGRN_SUB == 0 and ns % n_meta == 0
    assert 2 * b_dh == LANES and a_dk == LANES and a_dv == LANES

    tm = _largest_tile(seq, 512)
    n_aux = ns + n_meta
    meta_blk = ns // n_meta
    rec_r = _largest_tile(seq, 512)
    rec_c = _largest_tile(rec_r, 64)
    tq = _largest_tile(seq, ATTN_TILE)
    tk_cache = _largest_tile(past, 1024)
    npair = b_heads // 2
    npg = ATTN_PAIRS_PER_STEP if npair % ATTN_PAIRS_PER_STEP == 0 else 1

    sm = jax.nn.softmax(a_lb_logits.astype(F32), axis=0)
    lb_all = jnp.cumsum(sm, axis=0) - sm[0]

    xm = x_prompt.reshape(seq, d)
    xa = jnp.concatenate([x_sample.reshape(ns, d), meta_tokens.astype(F32)], axis=0)
    row = lambda w: w.reshape(1, -1).astype(F32)
    cache_kt = jnp.transpose(cache_k, (0, 1, 3, 4, 2))
    cache_vt = jnp.transpose(cache_v, (0, 1, 3, 4, 2))
    cache_lt = jnp.swapaxes(cache_logf.astype(F32), 2, 3)

    a_w_in_bf, a_w_out_bf, b_w_out_bf = (w.astype(BF16) for w in (a_w_in, a_w_out, b_w_out))
    b_wt_bf = jnp.swapaxes(b_w_in, 1, 2).astype(BF16)
    ffn_w_gu_bf, ffn_w_down_bf = ffn_w_gu.astype(BF16), ffn_w_down.astype(BF16)
    state_f32 = state_hgrn.astype(F32)

    st_p, st_s, k_p, v_p, lf_p, k_s, v_s, lf_s = [], [], [], [], [], [], [], []
    for i in range(depth):
        j = i // 2
        if i % 2 == 0:
            lb = lb_all[j]
            lbp = jnp.stack([jnp.log(jnp.maximum(lb, LB_FLOOR)), jnp.log1p(-lb), 1.0 - lb])
            w_out = a_w_out_bf
            qm, km, gm, vm, gtm = _hgrn_proj(xm, row(norm_pre_mix[i]), a_w_in_bf, j, lbp, tm)
            qa, ka, ga, va, gta = _hgrn_proj(xa, row(norm_pre_mix[i]), a_w_in_bf, j, lbp, n_aux)
            o_s, s_s = _hgrn_rec(qa, ka, ga, va, state_f32,
                                 n_seq=n_dec, seq_len=s_len, r=s_len, c=s_len, layer=j)
            o_m, s_p, o_meta = _hgrn_rec(qm, km, gm, vm,
                                         jnp.zeros((1, a_heads, a_dk, a_dv), F32),
                                         n_seq=1, seq_len=seq, r=rec_r, c=rec_c,
                                         meta=(qa, ka, ga, va, n_meta, meta_blk))
            st_p.append(s_p.astype(state_hgrn.dtype))
            st_s.append(s_s.astype(state_hgrn.dtype))
            o_a = jnp.concatenate([o_s, o_meta], axis=0)
            mix_m = (o_m, gtm, xm, row(a_g_norm[j]))
            mix_a = (o_a, gta, xa, row(a_g_norm[j]))
            head_dim = a_dv
        else:
            w_in = jnp.swapaxes(b_wt_bf[j, :3 * d], 0, 1)
            w_f = jnp.pad(b_w_in[j][:, 3 * d:], ((0, 0), (0, LANES - b_heads))).astype(BF16)
            bf_row = jnp.pad(b_f[j].astype(F32), (0, LANES - b_heads)).reshape(1, LANES)
            w_out = b_w_out_bf
            scale = float(b_dh) ** -0.5
            kt_m, vt_m, lf_m, qtb_m, kb_m, vtb_m, qsq, ksq = _fox_proj_t(
                xm, row(norm_pre_mix[i]), b_wt_bf, j, w_f, bf_row, b_heads, tm, scale * LOG2E)
            kf_a, vf_a, lf_a, qb_a, kb_a, vb_a = _fox_proj(
                xa, row(norm_pre_mix[i]), w_in, w_f, bf_row, b_heads, n_aux)
            lf_t = jnp.concatenate(
                [jnp.pad(lf_a[ns:].T, ((0, 0), (0, LANES - n_meta))), lf_m.T], axis=1)
            c_all = _lane_cumsum(lf_t, False, b_heads)
            kaux = _fox_kbias(c_all[:, LANES:].T, npair, tm, LOG2E)
            kaux_meta = _fox_kbias(c_all[:, :n_meta].T, npair, n_meta, LOG2E)
            k_meta = jnp.swapaxes(kb_a[ns:].reshape(n_meta, npair, LANES), 0, 1)
            jlo = _fox_first_key_block(c_all[:, LANES:], qsq, ksq, tq, 2 * npg)
            fixed, shift3 = _fox_score_shift(qsq, ksq, 2 * npg)
            o_m, o_meta = _fox_attn(jlo, fixed, shift3, c_all.reshape(b_heads, 1, -1),
                                    qtb_m, kb_m, kaux, vtb_m,
                                    jnp.concatenate([k_meta, kaux_meta], axis=-1), vb_a[ns:].T,
                                    qb_a, kb_a, vb_a, n_meta, meta_blk, b_dh, tq, LOG2E, npg)
            cl_t = cache_lt[j].reshape(n_dec * b_heads, past)
            tr = _largest_tile(n_dec * b_heads, LANES)
            suf = _lane_cumsum(cl_t, True, tr).reshape(n_dec, b_heads, past)
            ln_t = jnp.swapaxes(lf_a[:ns].reshape(n_dec, s_len, b_heads), 1, 2)
            ln_t = jnp.pad(ln_t, ((0, 0), (0, 0), (0, LANES - s_len)))
            cn = _lane_cumsum(ln_t.reshape(n_dec * b_heads, LANES), False, tr)
            o_s = _fox_sample(qb_a, kb_a, vb_a, cn.reshape(n_dec, b_heads, LANES),
                              cache_kt, cache_vt, j, suf, s_len, tk_cache)
            o_a = jnp.concatenate([o_s, o_meta], axis=0)
            mix_m = (o_m, None, xm, None)
            mix_a = (o_a, None, xa, None)
            head_dim = 0
            shp = lambda t, n: t.reshape(-1, n, b_heads, b_dh)
            k_p.append((kf_a[ns:].T, kt_m))
            v_p.append((vf_a[ns:].T, vt_m))
            lf_p.append(jnp.concatenate([lf_a[ns:], lf_m], axis=0).reshape(1, n_meta + seq, b_heads))
            k_s.append(shp(kf_a[:ns], s_len))
            v_s.append(shp(vf_a[:ns], s_len))
            lf_s.append(lf_a[:ns].reshape(n_dec, s_len, b_heads))
        ffn_w = (w_out, j, row(norm_post_mix[i]), row(norm_pre_ffn[i]), ffn_w_gu_bf,
                 ffn_w_down_bf, i, row(norm_post_ffn[i]))
        xm = _mix_ffn(*mix_m, *ffn_w, tm, head_dim)
        xa = _mix_ffn(*mix_a, *ffn_w, n_aux, head_dim)

    def prompt_kv(parts):
        t = jnp.concatenate([jnp.stack([m for m, _ in parts]), jnp.stack([x for _, x in parts])],
                            axis=2)
        t = t.reshape(len(parts), b_heads, b_dh, n_meta + seq)
        return jnp.transpose(t, (0, 3, 1, 2))[:, None]

    return (xm.reshape(1, seq, d), xa[:ns].reshape(n_dec, s_len, d),
            jnp.stack(st_p), prompt_kv(k_p), prompt_kv(v_p), jnp.stack(lf_p),
            jnp.stack(st_s), jnp.stack(k_s), jnp.stack(v_s), jnp.stack(lf_s))
```

```python
import functools

import jax
import jax.numpy as jnp
from jax import lax
from jax.experimental import pallas as pl
from jax.experimental.pallas import tpu as pltpu

F32 = jnp.float32
BF16 = jnp.bfloat16
EPS = 1e-6
NEG = -1e30
LB_FLOOR = 1e-30

LANES = 128
VMEM_LIMIT = 56 * 1024 * 1024
HGRN_SUB = 16
HIGHEST = lax.Precision.HIGHEST
LOG2E = 1.4426950408889634


def _params(n_grid):
    return pltpu.CompilerParams(dimension_semantics=("arbitrary",) * n_grid,
                                vmem_limit_bytes=VMEM_LIMIT)


def _resident(shape):
    nd = len(shape)
    return pl.BlockSpec(shape, lambda *_: (0,) * nd, pipeline_mode=pl.Buffered(1))


def _layer_of(stack, layer, rows=None):
    _, r, c = stack.shape
    return pl.BlockSpec((None, rows or r, c), lambda *_: (layer, 0, 0),
                        pipeline_mode=pl.Buffered(1))


def _rmsnorm(x, w):
    return x * lax.rsqrt(jnp.mean(x * x, axis=-1, keepdims=True) + EPS) * w


def _sigmoid(x):
    return 1.0 / (1.0 + jnp.exp(-x))


def _log1p_exp(z):
    return jnp.log(1.0 + jnp.exp(z))


def _log_sigmoid(x):
    return jnp.minimum(x, 0.0) - _log1p_exp(-jnp.abs(x))


def _mm(a, b):
    return jnp.dot(a, b, preferred_element_type=F32)


def _mm_nt(a, b):
    return lax.dot_general(a, b, (((1,), (1,)), ((), ())), preferred_element_type=F32)


def _mm_tn(a, b):
    return lax.dot_general(a, b, (((0,), (0,)), ((), ())), preferred_element_type=F32)


def _hgrn_proj_body(x_ref, nw_ref, w_ref, lb_ref, q_ref, k_ref, g_ref, v_ref, gate_ref):
    f = q_ref.shape[1]
    d = v_ref.shape[1]
    xn = _rmsnorm(x_ref[...], nw_ref[...]).astype(BF16)
    q = _mm(xn, w_ref[:, 0:f])
    q_ref[...] = q * _sigmoid(q)
    fl = _mm(xn, w_ref[:, f:2 * f])
    e = jnp.exp(-jnp.abs(fl))
    r = 1.0 / (1.0 + e)
    a = lb_ref[0:1, :]
    b = lb_ref[1:2, :] + (jnp.minimum(fl, 0.0) - jnp.log(1.0 + e))
    g_ref[...] = jnp.maximum(a, b) + _log1p_exp(-jnp.abs(a - b))
    k_ref[...] = lb_ref[2:3, :] * jnp.where(fl >= 0.0, e * r, r)
    v_ref[...] = _mm(xn, w_ref[:, 2 * f:2 * f + d])
    gt = _mm(xn, w_ref[:, 2 * f + d:])
    gate_ref[...] = gt * _sigmoid(gt)


def _hgrn_proj(x, nw, w_stack, layer, lbp, tm):
    n, d = x.shape
    f = lbp.shape[1]
    row = lambda w: pl.BlockSpec((tm, w), lambda i: (i, 0))
    return pl.pallas_call(
        _hgrn_proj_body,
        grid=(n // tm,),
        in_specs=[row(d), _resident((1, d)), _layer_of(w_stack, layer), _resident(lbp.shape)],
        out_specs=[row(f), row(f), row(f), row(d), row(d)],
        out_shape=[jax.ShapeDtypeStruct((n, w), F32) for w in (f, f, f, d, d)],
        compiler_params=_params(1),
        name="hgrn_proj",
    )(x, nw, w_stack, lbp)


def _hgrn_chunk(q, k, g, v, sts, c):
    nh = len(sts)
    sub, half = HGRN_SUB, HGRN_SUB // 2
    hs = lambda x, h: x[:, h * LANES:(h + 1) * LANES]
    r_i = lax.broadcasted_iota(jnp.int32, (c, c), 0)
    c_i = lax.broadcasted_iota(jnp.int32, (c, c), 1)
    tril = jnp.where(r_i >= c_i, 1.0, 0.0).astype(F32)
    gc = jnp.dot(tril, g, precision=HIGHEST, preferred_element_type=F32) * LOG2E
    g_last = gc[c - 1:c, :]
    qe = (q * jnp.exp2(gc)).astype(BF16)
    o_inter = [_mm_nt(hs(qe, h), sts[h].astype(BF16)) for h in range(nh)]
    kd = (k * jnp.exp2(g_last - gc)).astype(BF16)
    vb = v.astype(BF16)
    st_dec = jnp.exp2(g_last)
    st_new = [sts[h] * hs(st_dec, h) + _mm_tn(hs(vb, h), hs(kd, h)) for h in range(nh)]

    lk = jnp.log2(k) - gc
    half_row = lax.broadcasted_iota(jnp.int32, (half, 1), 0)
    out_rows = []
    for i in range(c // sub):
        r0 = i * sub
        gi, qi, vi, lki = gc[r0:r0 + sub], q[r0:r0 + sub], v[r0:r0 + sub], lk[r0:r0 + sub]
        o_lo = [o_inter[h][r0:r0 + half] for h in range(nh)]
        o_hi = [o_inter[h][r0 + half:r0 + sub] for h in range(nh)]
        if i > 0:
            g_first = gc[r0:r0 + 1]
            qs = (qi * jnp.exp2(gi - g_first)).astype(BF16)
            ks = (k[:r0] * jnp.exp2(g_first - gc[:r0])).astype(BF16)
            att = [_mm_nt(hs(qs, h), hs(ks, h)).astype(BF16) for h in range(nh)]
            off = [_mm(att[h], hs(vb[:r0], h)) for h in range(nh)]
            o_lo = [o_lo[h] + off[h][:half] for h in range(nh)]
            o_hi = [o_hi[h] + off[h][half:] for h in range(nh)]
        for rows, acc, s_range in ((slice(0, half), o_lo, range(half)),
                                   (slice(half, sub), o_hi, range(sub))):
            g_rows, q_rows = gi[rows], qi[rows]
            for s in s_range:
                diff = g_rows + lki[s:s + 1]
                if s >= rows.start:
                    diff = jnp.where(half_row >= s - rows.start, diff, NEG)
                prod = q_rows * jnp.exp2(diff)
                for h in range(nh):
                    col = jnp.sum(hs(prod, h), axis=-1, keepdims=True)
                    acc[h] = acc[h] + col * hs(vi, h)[s:s + 1]
        out_rows.append(jnp.concatenate(
            [jnp.concatenate([o_lo[h], o_hi[h]], axis=0) for h in range(nh)], axis=1))
    o = jnp.concatenate(out_rows, axis=0) if len(out_rows) > 1 else out_rows[0]
    return o, st_new


def _hgrn_rec_body(*refs, c, r, has_meta):
    if has_meta:
        (q_ref, k_ref, g_ref, v_ref, s0_ref, qm_ref, km_ref, gm_ref, vm_ref,
         o_ref, s_ref, om_ref, st_ref) = refs
    else:
        q_ref, k_ref, g_ref, v_ref, s0_ref, o_ref, s_ref, st_ref = refs
    blk = pl.program_id(1)
    nh = st_ref.shape[0]

    @pl.when(blk == 0)
    def _():
        sts = [s0_ref[0, h].T for h in range(nh)]
        if has_meta:
            o_m, sts = _hgrn_chunk(qm_ref[...], km_ref[...], gm_ref[...], vm_ref[...],
                                   sts, qm_ref.shape[0])
            om_ref[...] = o_m
        for h in range(nh):
            st_ref[h] = sts[h]

    def body(j, carry):
        sl = pl.ds(pl.multiple_of(j * c, c), c)
        o, sts = _hgrn_chunk(q_ref[sl, :], k_ref[sl, :], g_ref[sl, :], v_ref[sl, :],
                             [st_ref[h] for h in range(nh)], c)
        o_ref[sl, :] = o
        for h in range(nh):
            st_ref[h] = sts[h]
        return carry

    lax.fori_loop(0, r // c, body, 0)

    @pl.when(blk == pl.num_programs(1) - 1)
    def _():
        for h in range(nh):
            s_ref[0, h] = st_ref[h].T


def _hgrn_rec(q, k, g, v, s0, *, n_seq, seq_len, r, c, meta=None, layer=None):
    h, dk, dv = s0.shape[-3:]
    nblk = seq_len // r
    tok = lambda w: pl.BlockSpec((r, h * w), lambda b, i: (b * nblk + i, 0))
    st_spec = pl.BlockSpec((1, h, dk, dv), lambda b, i: (b, 0, 0, 0))
    s0_spec = st_spec if layer is None else pl.BlockSpec(
        (None, 1, h, dk, dv), lambda b, i: (layer, b, 0, 0, 0))
    in_specs = [tok(dk), tok(dk), tok(dk), tok(dv), s0_spec]
    args = [q, k, g, v, s0]
    out_specs = [tok(dv), st_spec]
    out_shape = [jax.ShapeDtypeStruct((n_seq * seq_len, h * dv), F32),
                 jax.ShapeDtypeStruct((n_seq, h, dk, dv), F32)]
    if meta is not None:
        qm, km, gm, vm, n_meta, meta_blk = meta
        mspec = lambda w: pl.BlockSpec((n_meta, h * w), lambda b, i: (meta_blk, 0))
        in_specs += [mspec(dk), mspec(dk), mspec(dk), mspec(dv)]
        args += [qm, km, gm, vm]
        out_specs.append(pl.BlockSpec((n_meta, h * dv), lambda b, i: (0, 0)))
        out_shape.append(jax.ShapeDtypeStruct((n_meta, h * dv), F32))
    return pl.pallas_call(
        functools.partial(_hgrn_rec_body, c=c, r=r, has_meta=meta is not None),
        grid=(n_seq, nblk),
        in_specs=in_specs,
        out_specs=out_specs,
        out_shape=out_shape,
        scratch_shapes=[pltpu.VMEM((h, dv, dk), F32)],
        compiler_params=_params(2),
        name="hgrn_rec",
    )(*args)


def _mix_ffn_body(*refs, head_dim, dff, fc):
    if head_dim:
        (o_ref, gate_ref, x_ref, gn_ref, wo_ref, pm_ref,
         nf_ref, wgu_ref, wd_ref, pf_ref, y_ref) = refs
        o = o_ref[...]
        parts = []
        for h in range(o.shape[1] // head_dim):
            oh = o[:, h * head_dim:(h + 1) * head_dim]
            parts.append(oh * lax.rsqrt(jnp.mean(oh * oh, axis=-1, keepdims=True) + EPS))
        o = (jnp.concatenate(parts, axis=-1) * gn_ref[...] * gate_ref[...]).astype(BF16)
    else:
        o_ref, x_ref, wo_ref, pm_ref, nf_ref, wgu_ref, wd_ref, pf_ref, y_ref = refs
        o = o_ref[...]
    x = x_ref[...] + _rmsnorm(_mm(o, wo_ref[...]), pm_ref[...])
    xn = _rmsnorm(x, nf_ref[...]).astype(BF16)
    acc = jnp.zeros(x.shape, F32)
    for c0 in range(0, dff, fc):
        a = _mm(xn, wgu_ref[:, c0:c0 + fc])
        u = _mm(xn, wgu_ref[:, dff + c0:dff + c0 + fc])
        hid = (a * _sigmoid(a) * u).astype(BF16)
        acc = acc + _mm(hid, wd_ref[c0:c0 + fc, :])
    y_ref[...] = x + _rmsnorm(acc, pf_ref[...])


def _mix_ffn(o, gate, x, gn, wo_stack, mix_layer, pm, nf, wgu_stack, wd_stack, ffn_layer, pf,
             tm, head_dim):
    n, d = x.shape
    dff = wd_stack.shape[1]
    fc = dff // 2 if (dff // 2) % LANES == 0 else dff
    row = pl.BlockSpec((tm, d), lambda i: (i, 0))
    vec = _resident((1, d))
    ffn_specs = [vec, _layer_of(wgu_stack, ffn_layer), _layer_of(wd_stack, ffn_layer), vec]
    wo_spec = _layer_of(wo_stack, mix_layer)
    if head_dim:
        in_specs = [row, row, row, vec, wo_spec, vec] + ffn_specs
        args = (o, gate, x, gn, wo_stack, pm, nf, wgu_stack, wd_stack, pf)
    else:
        in_specs = [row, row, wo_spec, vec] + ffn_specs
        args = (o, x, wo_stack, pm, nf, wgu_stack, wd_stack, pf)
    return pl.pallas_call(
        functools.partial(_mix_ffn_body, head_dim=head_dim, dff=dff, fc=fc),
        grid=(n // tm,),
        in_specs=in_specs,
        out_specs=row,
        out_shape=jax.ShapeDtypeStruct((n, d), F32),
        compiler_params=_params(1),
        name="mix_ffn",
    )(*args)


def _fox_proj_body(x_ref, nw_ref, wt_ref, wf_ref, bf_ref, k_ref, v_ref, lf_ref,
                   qb_ref, kb_ref, vb_ref, *, scale):
    d = k_ref.shape[1]
    nh = lf_ref.shape[1]
    xn = _rmsnorm(x_ref[...], nw_ref[...]).astype(BF16)
    qb_ref[...] = (_mm_nt(xn, wt_ref[0:d, :]) * scale).astype(BF16)
    k = _mm_nt(xn, wt_ref[d:2 * d, :])
    k_ref[...] = k
    kb_ref[...] = k.astype(BF16)
    v = _mm_nt(xn, wt_ref[2 * d:3 * d, :])
    v_ref[...] = v
    vb_ref[...] = v.astype(BF16)
    lf = _log_sigmoid(_mm(xn, wf_ref[...]) + bf_ref[...])
    lf_ref[...] = lf[:, 0:nh]


def _fox_proj(x, nw, wt_stack, layer, wf_bf, bf_row, nh, tm):
    n, d = x.shape
    row = lambda w: pl.BlockSpec((tm, w), lambda i: (i, 0))
    scale = float(d // nh) ** -0.5
    return pl.pallas_call(
        functools.partial(_fox_proj_body, scale=scale),
        grid=(n // tm,),
        in_specs=[row(d), _resident((1, d)), _layer_of(wt_stack, layer, 3 * d),
                  _resident(wf_bf.shape), _resident(bf_row.shape)],
        out_specs=[row(d), row(d), row(nh), row(d), row(d), row(d)],
        out_shape=[jax.ShapeDtypeStruct((n, d), F32), jax.ShapeDtypeStruct((n, d), F32),
                   jax.ShapeDtypeStruct((n, nh), F32), jax.ShapeDtypeStruct((n, d), BF16),
                   jax.ShapeDtypeStruct((n, d), BF16), jax.ShapeDtypeStruct((n, d), BF16)],
        compiler_params=_params(1),
        name="fox_proj",
    )(x, nw, wt_stack, wf_bf, bf_row)


def _head_sq_norms(xt, nh):
    dh = xt.shape[0] // nh
    x2 = xt * xt
    return jnp.concatenate([jnp.sum(x2[h * dh:(h + 1) * dh], axis=0, keepdims=True)
                            for h in range(nh)], axis=0)


def _fox_proj_t_body(x_ref, nw_ref, wt_ref, wf_ref, bf_ref, kt_ref, vt_ref, lf_ref,
                     qtb_ref, kb_ref, vtb_ref, qsq_ref, ksq_ref, *, qscale):
    d = kt_ref.shape[0]
    nh = lf_ref.shape[1]
    xn = _rmsnorm(x_ref[...], nw_ref[...]).astype(BF16)
    qtb = (_mm_nt(wt_ref[0:d, :], xn) * qscale).astype(BF16)
    qtb_ref[...] = qtb
    qsq_ref[...] = _head_sq_norms(qtb.astype(F32), nh)
    kt = _mm_nt(wt_ref[d:2 * d, :], xn)
    kt_ref[...] = kt
    ksq_ref[...] = _head_sq_norms(kt.astype(BF16).astype(F32), nh)
    kb_ref[...] = kt.T.astype(BF16)
    vt = _mm_nt(wt_ref[2 * d:3 * d, :], xn)
    vt_ref[...] = vt
    vtb_ref[...] = vt.astype(BF16)
    lf = _log_sigmoid(_mm(xn, wf_ref[...]) + bf_ref[...])
    lf_ref[...] = lf[:, 0:nh]


def _fox_proj_t(x, nw, wt_stack, layer, wf_bf, bf_row, nh, tm, qscale):
    n, d = x.shape
    row = lambda w: pl.BlockSpec((tm, w), lambda i: (i, 0))
    col = pl.BlockSpec((d, tm), lambda i: (0, i))
    sq = pl.BlockSpec((nh, tm), lambda i: (0, i))
    return pl.pallas_call(
        functools.partial(_fox_proj_t_body, qscale=qscale),
        grid=(n // tm,),
        in_specs=[row(d), _resident((1, d)), _layer_of(wt_stack, layer, 3 * d),
                  _resident(wf_bf.shape), _resident(bf_row.shape)],
        out_specs=[col, col, row(nh), col, row(d), col, sq, sq],
        out_shape=[jax.ShapeDtypeStruct((d, n), F32), jax.ShapeDtypeStruct((d, n), F32),
                   jax.ShapeDtypeStruct((n, nh), F32), jax.ShapeDtypeStruct((d, n), BF16),
                   jax.ShapeDtypeStruct((n, d), BF16), jax.ShapeDtypeStruct((d, n), BF16),
                   jax.ShapeDtypeStruct((nh, n), F32), jax.ShapeDtypeStruct((nh, n), F32)],
        compiler_params=_params(1),
        name="fox_proj_t",
    )(x, nw, wt_stack, wf_bf, bf_row)


BIAS_LANES_PER_HEAD = 6


def _split3(v):
    hi = v.astype(BF16).astype(F32)
    r1 = v - hi
    mid = r1.astype(BF16).astype(F32)
    lo = (r1 - mid).astype(BF16).astype(F32)
    return hi, mid, lo


def _fox_kbias_body(c4_ref, aux_ref, *, cscale, nh):
    tm = c4_ref.shape[0]
    npair = aux_ref.shape[0]
    hi, mid, lo = _split3(c4_ref[...] * (-cscale))
    group = lax.broadcasted_iota(jnp.int32, (tm, 4 * nh), 1) // nh
    src = jnp.where(group == 0, hi, jnp.where(group == 1, mid, jnp.where(group == 2, lo, 1.0)))
    row = lax.broadcasted_iota(jnp.int32, (4 * nh, npair * LANES), 0)
    col = lax.broadcasted_iota(jnp.int32, (4 * nh, npair * LANES), 1)
    lane = col % LANES
    slot = lane % BIAS_LANES_PER_HEAD
    head = 2 * (col // LANES) + lane // BIAS_LANES_PER_HEAD
    pick = (lane < 2 * BIAS_LANES_PER_HEAD) & (row == jnp.minimum(slot, 3) * nh + head)
    out = _mm(src.astype(BF16), jnp.where(pick, 1.0, 0.0).astype(BF16))
    for p in range(npair):
        aux_ref[p] = out[:, p * LANES:(p + 1) * LANES].astype(BF16)


def _fox_kbias(c_rows, npair, tm, cscale):
    n, nh = c_rows.shape
    return pl.pallas_call(
        functools.partial(_fox_kbias_body, cscale=cscale, nh=nh),
        grid=(n // tm,),
        in_specs=[pl.BlockSpec((tm, 4 * nh), lambda i: (i, 0))],
        out_specs=pl.BlockSpec((npair, tm, LANES), lambda i: (0, i, 0)),
        out_shape=jax.ShapeDtypeStruct((npair, n, LANES), BF16),
        compiler_params=_params(1),
        name="fox_kbias",
    )(jnp.tile(c_rows, (1, 4)))


def _lane_cumsum_body(x_ref, y_ref, *, reverse_exclusive):
    rows, n = x_ref.shape
    nb = n // LANES
    t_i = lax.broadcasted_iota(jnp.int32, (LANES, LANES), 0)
    k_i = lax.broadcasted_iota(jnp.int32, (LANES, LANES), 1)
    sel = (t_i > k_i) if reverse_exclusive else (t_i <= k_i)
    tri = jnp.where(sel, 1.0, 0.0).astype(F32)

    def body(i, carry):
        b = (nb - 1 - i) if reverse_exclusive else i
        sl = slice(b * LANES, (b + 1) * LANES)
        xb = x_ref[:, sl]
        y_ref[:, sl] = jnp.dot(xb, tri, precision=HIGHEST, preferred_element_type=F32) + carry
        return carry + jnp.sum(xb, axis=-1, keepdims=True)

    carry = jnp.zeros((rows, 1), F32)
    for i in range(nb):
        carry = body(i, carry)


def _lane_cumsum(x, reverse_exclusive, tr):
    rows, n = x.shape
    spec = pl.BlockSpec((tr, n), lambda i: (i, 0))
    return pl.pallas_call(
        functools.partial(_lane_cumsum_body, reverse_exclusive=reverse_exclusive),
        grid=(rows // tr,),
        in_specs=[spec],
        out_specs=spec,
        out_shape=jax.ShapeDtypeStruct((rows, n), F32),
        compiler_params=_params(1),
        name="lane_cumsum",
    )(x)


ATTN_TILE = 512
ATTN_PAIRS_PER_STEP = 2
ACC_PAD = 16


def _fox_attn_body(jlo_ref, fixed_ref, c_ref, sh_ref, qt_ref, k_ref, ka_ref, vt_ref, kme_ref,
                   vtm_ref, qm_ref, km_ref, vm_ref, o_ref, om_ref, qx_s, m_s, acc_s,
                   *, tq, tk, dh, n_meta, cscale, npg):
    i = pl.program_id(1)
    heads = range(2 * npg)
    pair_lanes = lambda x, pp: x[:, pp * 2 * dh:(pp + 1) * 2 * dh]
    pair_rows = lambda x, pp: x[pp * 2 * dh:(pp + 1) * 2 * dh, :]
    lane = lax.broadcasted_iota(jnp.int32, (1, 2 * dh), 1)
    head_sel = [lane < dh, lane >= dh]

    @pl.when(i == 0)
    def _():
        r_i = lax.broadcasted_iota(jnp.int32, (n_meta, n_meta), 0)
        c_i = lax.broadcasted_iota(jnp.int32, (n_meta, n_meta), 1)
        outs = []
        for pp in range(npg):
            qm, km, vm = (pair_lanes(r[...], pp) for r in (qm_ref, km_ref, vm_ref))
            out = jnp.zeros((n_meta, 2 * dh), F32)
            for a in range(2):
                qa = jnp.where(head_sel[a], qm, jnp.zeros_like(qm))
                x = _mm_nt(qa, km) - c_ref[2 * pp + a, :, 0:n_meta]
                x = jnp.where(c_i <= r_i, x, NEG)
                p = jnp.exp(x - jnp.max(x, axis=-1, keepdims=True))
                oa = _mm(p.astype(BF16), vm) / jnp.sum(p, axis=-1, keepdims=True)
                out = jnp.where(head_sel[a], oa, out)
            outs.append(out)
        om_ref[...] = jnp.concatenate(outs, axis=1).astype(BF16)

    q0 = pl.multiple_of(LANES + i * tq, LANES)
    row = lax.broadcasted_iota(jnp.int32, (2 * dh, tq), 0)
    for hh in heads:
        pp, a = divmod(hh, 2)
        qt = pair_rows(qt_ref[...], pp)
        hi, mid, lo = _split3(c_ref[hh, :, pl.ds(q0, tq)] * cscale - sh_ref[hh, :, 0:1])
        slot = row - a * BIAS_LANES_PER_HEAD
        bias = jnp.where(slot == 3, hi, jnp.where(slot == 4, mid, jnp.where(slot == 5, lo, 1.0)))
        bias = jnp.where((slot >= 0) & (slot < BIAS_LANES_PER_HEAD), bias, 0.0)
        qx_s[hh, 0:2 * dh, :] = jnp.where((row >= a * dh) & (row < (a + 1) * dh), qt,
                                          jnp.zeros_like(qt))
        qx_s[hh, 2 * dh:4 * dh, :] = bias.astype(BF16)
        m_s[hh] = jnp.full((1, tq), NEG, F32)
        acc_s[hh] = jnp.zeros((dh + ACC_PAD, tq), F32)

    def scores(kxs):
        return [_mm(kxs[hh // 2], qx_s[hh]) for hh in heads]

    def update(sts, vts, masked, fixed, q_lo=0):
        tkb, tqb = sts[0].shape
        qs = slice(q_lo, q_lo + tqb)
        ones = jnp.where(lax.broadcasted_iota(jnp.int32, (ACC_PAD, tkb), 0) == 0,
                         1.0, 0.0).astype(BF16)
        for hh in heads:
            pp, a = divmod(hh, 2)
            st = sts[hh]
            if masked:
                k_i = lax.broadcasted_iota(jnp.int32, (tkb, tqb), 0)
                q_i = lax.broadcasted_iota(jnp.int32, (tkb, tqb), 1)
                st = jnp.where(k_i <= q_i, st, NEG)
            vx = jnp.concatenate([vts[pp][a * dh:(a + 1) * dh, :], ones], axis=0)
            if fixed:
                acc_s[hh, :, qs] = acc_s[hh, :, qs] + _mm(vx, jnp.exp2(st).astype(BF16))
            else:
                m_prev = m_s[hh, :, qs]
                m_new = jnp.maximum(m_prev, jnp.max(st, axis=0, keepdims=True))
                p = jnp.exp2(st - m_new).astype(BF16)
                acc_s[hh, :, qs] = jnp.exp2(m_prev - m_new) * acc_s[hh, :, qs] + _mm(vx, p)
                m_s[hh, :, qs] = m_new

    def key_block(j):
        k0 = pl.multiple_of(j * tk, tk)
        kxs = [jnp.concatenate([pair_lanes(k_ref[pl.ds(k0, tk), :], pp),
                                ka_ref[pp, pl.ds(k0, tk), :]], axis=1) for pp in range(npg)]
        vts = [pair_rows(vt_ref[:, pl.ds(k0, tk)], pp) for pp in range(npg)]
        return kxs, vts

    def run(fixed):
        def body(j, carry):
            kxs, vts = key_block(j)
            update(scores(kxs), vts, False, fixed)
            return carry

        lax.fori_loop(jlo_ref[pl.program_id(0), i], i, body, 0)

        kxs, vts = key_block(i)
        hk = tk // 2
        st_a = [_mm(kxs[hh // 2][:hk], qx_s[hh]) for hh in heads]
        st_b = [_mm(kxs[hh // 2][hk:], qx_s[hh, :, hk:]) for hh in heads]
        st_meta = scores([kme_ref[pp] for pp in range(npg)])
        update(st_a, [v[:, :hk] for v in vts], True, fixed)
        update(st_b, [v[:, hk:] for v in vts], True, fixed, q_lo=hk)
        update(st_meta, [pair_rows(vtm_ref[...], pp) for pp in range(npg)], False, fixed)

    is_fixed = fixed_ref[pl.program_id(0)] != 0
    pl.when(is_fixed)(lambda: run(True))
    pl.when(jnp.logical_not(is_fixed))(lambda: run(False))

    outs = []
    for pp in range(npg):
        halves = []
        for a in range(2):
            acc = acc_s[2 * pp + a]
            halves.append(acc[0:dh] / acc[dh:dh + 1])
        outs.append(jnp.concatenate(halves, axis=0).T)
    o_ref[...] = jnp.concatenate(outs, axis=1).astype(BF16)


def _fox_attn(jlo, fixed, shift3, c3, qt_b, kb, kaux, vt_b, kx_meta, vt_meta, qb_aux, kb_aux,
              vb_aux, n_meta, meta_blk, dh, tq, cscale, npg):
    n, d = kb.shape
    gw = 2 * dh * npg
    ngroups = d // gw
    nl = c3.shape[2]
    mspec = pl.BlockSpec((n_meta, gw), lambda p, i, *_: (meta_blk, p))
    once = dict(pipeline_mode=pl.Buffered(1))
    return pl.pallas_call(
        functools.partial(_fox_attn_body, tq=tq, tk=tq, dh=dh, n_meta=n_meta, cscale=cscale,
                          npg=npg),
        grid_spec=pltpu.PrefetchScalarGridSpec(
            num_scalar_prefetch=2,
            grid=(ngroups, n // tq),
            in_specs=[pl.BlockSpec((2 * npg, 1, nl), lambda p, i, *_: (p, 0, 0)),
                      pl.BlockSpec((2 * npg, 1, LANES), lambda p, i, *_: (p, 0, 0)),
                      pl.BlockSpec((gw, tq), lambda p, i, *_: (p, i)),
                      pl.BlockSpec((n, gw), lambda p, i, *_: (0, p), **once),
                      pl.BlockSpec((npg, n, LANES), lambda p, i, *_: (p, 0, 0), **once),
                      pl.BlockSpec((gw, n), lambda p, i, *_: (p, 0), **once),
                      pl.BlockSpec((npg, n_meta, 4 * dh), lambda p, i, *_: (p, 0, 0)),
                      pl.BlockSpec((gw, n_meta), lambda p, i, *_: (p, 0)),
                      mspec, mspec, mspec],
            out_specs=[pl.BlockSpec((tq, gw), lambda p, i, *_: (i, p)),
                       pl.BlockSpec((n_meta, gw), lambda p, i, *_: (0, p))],
            scratch_shapes=[pltpu.VMEM((2 * npg, 4 * dh, tq), BF16),
                            pltpu.VMEM((2 * npg, 1, tq), F32),
                            pltpu.VMEM((2 * npg, dh + ACC_PAD, tq), F32)]),
        out_shape=[jax.ShapeDtypeStruct((n, d), BF16), jax.ShapeDtypeStruct((n_meta, d), BF16)],
        compiler_params=_params(2),
        name="fox_attn",
    )(jlo, fixed, c3, shift3, qt_b, kb, kaux, vt_b, kx_meta, vt_meta, qb_aux, kb_aux, vb_aux)


PRUNE_NATS = 37.0
NORM_SLACK = 1.01


FIXED_SHIFT_MAX_SPAN = 60.0


def _fox_score_shift(qsq, ksq, group):
    nh = qsq.shape[0]
    u = jnp.sqrt(jnp.max(qsq, axis=1) * jnp.max(ksq, axis=1)) * NORM_SLACK
    fixed = jnp.all((2.0 * u <= FIXED_SHIFT_MAX_SPAN).reshape(nh // group, group), axis=1)
    shift = jnp.where(jnp.repeat(fixed, group), u, 0.0)
    return fixed.astype(jnp.int32), jnp.broadcast_to(shift[:, None, None], (nh, 1, LANES))


def _fox_first_key_block(c_main, qsq, ksq, tq, group):
    nh, n = c_main.shape
    nq = n // tq
    u = jnp.sqrt(jnp.max(qsq, axis=1) * jnp.max(ksq, axis=1)) * (NORM_SLACK / LOG2E)
    c_first = c_main[:, 0::tq]
    c_last = c_main[:, tq - 1::tq]
    bound = 2.0 * u[:, None, None] + c_first[:, :, None] - c_last[:, None, :]
    j_lt_i = jnp.arange(nq)[None, :] < jnp.arange(nq)[:, None]
    skip = (bound <= -PRUNE_NATS) & j_lt_i[None]
    jlo = jnp.sum(skip, axis=2).astype(jnp.int32)
    return jnp.min(jlo.reshape(nh // group, group, nq), axis=1)


def _fox_sample_body(q_ref, kn_ref, vn_ref, cn_ref, kc_ref, vc_ref, suf_ref, o_ref,
                     m_s, l_s, acc_s, *, nh, dh, s_len):
    t = pl.program_id(1)

    @pl.when(t == 0)
    def _():
        r_i = lax.broadcasted_iota(jnp.int32, (s_len, s_len), 0)
        c_i = lax.broadcasted_iota(jnp.int32, (s_len, s_len), 1)
        cols = [slice(h * dh, (h + 1) * dh) for h in range(nh)]
        x = [_mm_nt(q_ref[:, cols[h]], kn_ref[:, cols[h]]) - cn_ref[0, h:h + 1, 0:s_len]
             for h in range(nh)]
        x = [jnp.where(c_i <= r_i, x[h], NEG) for h in range(nh)]
        m = [jnp.max(x[h], axis=-1, keepdims=True) for h in range(nh)]
        p = [jnp.exp(x[h] - m[h]) for h in range(nh)]
        pv = [_mm(p[h].astype(BF16), vn_ref[:, cols[h]]) for h in range(nh)]
        for h in range(nh):
            m_s[h] = m[h]
            l_s[h] = jnp.sum(p[h], axis=-1, keepdims=True)
            acc_s[h] = pv[h]

    heads = range(nh)
    x = [_mm(q_ref[:, h * dh:(h + 1) * dh], kc_ref[0, h].astype(BF16)) + suf_ref[0, h:h + 1, :]
         for h in heads]
    m_prev = [m_s[h] for h in heads]
    m_new = [jnp.maximum(m_prev[h], jnp.max(x[h], axis=-1, keepdims=True)) for h in heads]
    p = [jnp.exp(x[h] - m_new[h]) for h in heads]
    pv = [_mm_nt(p[h].astype(BF16), vc_ref[0, h].astype(BF16)) for h in heads]
    for h in heads:
        alpha = jnp.exp(m_prev[h] - m_new[h])
        l_s[h] = alpha * l_s[h] + jnp.sum(p[h], axis=-1, keepdims=True)
        acc_s[h] = alpha * acc_s[h] + pv[h]
        m_s[h] = m_new[h]

    @pl.when(t == pl.num_programs(1) - 1)
    def _():
        o_ref[...] = jnp.concatenate([acc_s[h] / l_s[h] for h in range(nh)], axis=-1).astype(BF16)


def _fox_sample(qb, kb, vb, cn, kc_t, vc_t, layer, suf, s_len, tk):
    _, nb, nh, dh, past = kc_t.shape
    d = nh * dh
    tok = pl.BlockSpec((s_len, d), lambda b, t: (b, 0))
    cache = pl.BlockSpec((None, 1, nh, dh, tk), lambda b, t: (layer, b, 0, 0, t))
    return pl.pallas_call(
        functools.partial(_fox_sample_body, nh=nh, dh=dh, s_len=s_len),
        grid=(nb, past // tk),
        in_specs=[tok, tok, tok, pl.BlockSpec((1, nh, LANES), lambda b, t: (b, 0, 0)),
                  cache, cache, pl.BlockSpec((1, nh, tk), lambda b, t: (b, 0, t))],
        out_specs=tok,
        out_shape=jax.ShapeDtypeStruct((nb * s_len, d), BF16),
        scratch_shapes=[pltpu.VMEM((nh, s_len, 1), F32), pltpu.VMEM((nh, s_len, 1), F32),
                        pltpu.VMEM((nh, s_len, dh), F32)],
        compiler_params=_params(2),
        name="fox_sample",
    )(qb, kb, vb, cn, kc_t, vc_t, suf)


def _largest_tile(n, cap):
    t = min(n, cap)
    while n % t:
        t //= 2
    return t


def kernel(x_prompt, x_sample, state_hgrn, cache_k, cache_v, cache_logf, meta_tokens,
           norm_pre_mix, norm_post_mix, norm_pre_ffn, norm_post_ffn,
           a_w_in, a_lb_logits, a_g_norm, a_w_out, b_w_in, b_f, b_w_out,
           ffn_w_gu, ffn_w_down):
    batch, seq, d = x_prompt.shape
    assert batch == 1, "one prompt stream per step"
    n_dec, s_len, _ = x_sample.shape
    n_meta = meta_tokens.shape[0]
    depth = norm_pre_mix.shape[0]
    _, _, a_heads, a_dk, a_dv = state_hgrn.shape
    _, _, past, b_heads, b_dh = cache_k.shape
    ns = n_dec * s_len
    assert s_len % HGRN_SUB == 0 and n_meta % HGRN_SUB == 0 and ns % n_meta == 0
    assert 2 * b_dh == LANES and a_dk == LANES and a_dv == LANES

    tm = _largest_tile(seq, 512)
    n_aux = ns + n_meta
    meta_blk = ns // n_meta
    rec_r = _largest_tile(seq, 512)
    rec_c = _largest_tile(rec_r, 64)
    tq = _largest_tile(seq, ATTN_TILE)
    tk_cache = _largest_tile(past, 1024)
    npair = b_heads // 2
    npg = ATTN_PAIRS_PER_STEP if npair % ATTN_PAIRS_PER_STEP == 0 else 1

    sm = jax.nn.softmax(a_lb_logits.astype(F32), axis=0)
    lb_all = jnp.cumsum(sm, axis=0) - sm[0]

    xm = x_prompt.reshape(seq, d)
    xa = jnp.concatenate([x_sample.reshape(ns, d), meta_tokens.astype(F32)], axis=0)
    row = lambda w: w.reshape(1, -1).astype(F32)
    cache_kt = jnp.transpose(cache_k, (0, 1, 3, 4, 2))
    cache_vt = jnp.transpose(cache_v, (0, 1, 3, 4, 2))
    cache_lt = jnp.swapaxes(cache_logf.astype(F32), 2, 3)

    a_w_in_bf, a_w_out_bf, b_w_out_bf = (w.astype(BF16) for w in (a_w_in, a_w_out, b_w_out))
    b_wt_bf = jnp.swapaxes(b_w_in, 1, 2).astype(BF16)
    ffn_w_gu_bf, ffn_w_down_bf = ffn_w_gu.astype(BF16), ffn_w_down.astype(BF16)
    state_f32 = state_hgrn.astype(F32)

    st_p, st_s, k_p, v_p, lf_p, k_s, v_s, lf_s = [], [], [], [], [], [], [], []
    for i in range(depth):
        j = i // 2
        if i % 2 == 0:
            lb = lb_all[j]
            lbp = jnp.stack([jnp.log(jnp.maximum(lb, LB_FLOOR)), jnp.log1p(-lb), 1.0 - lb])
            w_out = a_w_out_bf
            qm, km, gm, vm, gtm = _hgrn_proj(xm, row(norm_pre_mix[i]), a_w_in_bf, j, lbp, tm)
            qa, ka, ga, va, gta = _hgrn_proj(xa, row(norm_pre_mix[i]), a_w_in_bf, j, lbp, n_aux)
            o_s, s_s = _hgrn_rec(qa, ka, ga, va, state_f32,
                                 n_seq=n_dec, seq_len=s_len, r=s_len, c=s_len, layer=j)
            o_m, s_p, o_meta = _hgrn_rec(qm, km, gm, vm,
                                         jnp.zeros((1, a_heads, a_dk, a_dv), F32),
                                         n_seq=1, seq_len=seq, r=rec_r, c=rec_c,
                                         meta=(qa, ka, ga, va, n_meta, meta_blk))
            st_p.append(s_p.astype(state_hgrn.dtype))
            st_s.append(s_s.astype(state_hgrn.dtype))
            o_a = jnp.concatenate([o_s, o_meta], axis=0)
            mix_m = (o_m, gtm, xm, row(a_g_norm[j]))
            mix_a = (o_a, gta, xa, row(a_g_norm[j]))
            head_dim = a_dv
        else:
            w_f = jnp.pad(b_w_in[j][:, 3 * d:], ((0, 0), (0, LANES - b_heads))).astype(BF16)
            bf_row = jnp.pad(b_f[j].astype(F32), (0, LANES - b_heads)).reshape(1, LANES)
            w_out = b_w_out_bf
            scale = float(b_dh) ** -0.5
            kt_m, vt_m, lf_m, qtb_m, kb_m, vtb_m, qsq, ksq = _fox_proj_t(
                xm, row(norm_pre_mix[i]), b_wt_bf, j, w_f, bf_row, b_heads, tm, scale * LOG2E)
            kf_a, vf_a, lf_a, qb_a, kb_a, vb_a = _fox_proj(
                xa, row(norm_pre_mix[i]), b_wt_bf, j, w_f, bf_row, b_heads, n_aux)
            lf_t = jnp.concatenate(
                [jnp.pad(lf_a[ns:].T, ((0, 0), (0, LANES - n_meta))), lf_m.T], axis=1)
            c_all = _lane_cumsum(lf_t, False, b_heads)
            kaux = _fox_kbias(c_all[:, LANES:].T, npair, tm, LOG2E)
            kaux_meta = _fox_kbias(c_all[:, :n_meta].T, npair, n_meta, LOG2E)
            k_meta = jnp.swapaxes(kb_a[ns:].reshape(n_meta, npair, LANES), 0, 1)
            jlo = _fox_first_key_block(c_all[:, LANES:], qsq, ksq, tq, 2 * npg)
            fixed, shift3 = _fox_score_shift(qsq, ksq, 2 * npg)
            o_m, o_meta = _fox_attn(jlo, fixed, shift3, c_all.reshape(b_heads, 1, -1),
                                    qtb_m, kb_m, kaux, vtb_m,
                                    jnp.concatenate([k_meta, kaux_meta], axis=-1), vb_a[ns:].T,
                                    qb_a, kb_a, vb_a, n_meta, meta_blk, b_dh, tq, LOG2E, npg)
            cl_t = cache_lt[j].reshape(n_dec * b_heads, past)
            tr = _largest_tile(n_dec * b_heads, LANES)
            suf = _lane_cumsum(cl_t, True, tr).reshape(n_dec, b_heads, past)
            ln_t = jnp.swapaxes(lf_a[:ns].reshape(n_dec, s_len, b_heads), 1, 2)
            ln_t = jnp.pad(ln_t, ((0, 0), (0, 0), (0, LANES - s_len)))
            cn = _lane_cumsum(ln_t.reshape(n_dec * b_heads, LANES), False, tr)
            o_s = _fox_sample(qb_a, kb_a, vb_a, cn.reshape(n_dec, b_heads, LANES),
                              cache_kt, cache_vt, j, suf, s_len, tk_cache)
            o_a = jnp.concatenate([o_s, o_meta], axis=0)
            mix_m = (o_m, None, xm, None)
            mix_a = (o_a, None, xa, None)
            head_dim = 0
            shp = lambda t, n: t.reshape(-1, n, b_heads, b_dh)
            k_p.append((kf_a[ns:].T, kt_m))
            v_p.append((vf_a[ns:].T, vt_m))
            lf_p.append(jnp.concatenate([lf_a[ns:], lf_m], axis=0).reshape(1, n_meta + seq, b_heads))
            k_s.append(shp(kf_a[:ns], s_len))
            v_s.append(shp(vf_a[:ns], s_len))
            lf_s.append(lf_a[:ns].reshape(n_dec, s_len, b_heads))
        ffn_w = (w_out, j, row(norm_post_mix[i]), row(norm_pre_ffn[i]), ffn_w_gu_bf,
                 ffn_w_down_bf, i, row(norm_post_ffn[i]))
        xm = _mix_ffn(*mix_m, *ffn_w, tm, head_dim)
        xa = _mix_ffn(*mix_a, *ffn_w, n_aux, head_dim)

    def prompt_kv(parts):
        t = jnp.concatenate([jnp.stack([m for m, _ in parts]), jnp.stack([x for _, x in parts])],
                            axis=2)
        t = t.reshape(len(parts), b_heads, b_dh, n_meta + seq)
        return jnp.transpose(t, (0, 3, 1, 2))[:, None]

    return (xm.reshape(1, seq, d), xa[:ns].reshape(n_dec, s_len, d),
            jnp.stack(st_p), prompt_kv(k_p), prompt_kv(v_p), jnp.stack(lf_p),
            jnp.stack(st_s), jnp.stack(k_s), jnp.stack(v_s), jnp.stack(lf_s))
```

```python
import functools

import jax
import jax.numpy as jnp
from jax import lax
from jax.experimental import pallas as pl
from jax.experimental.pallas import tpu as pltpu

F32 = jnp.float32
BF16 = jnp.bfloat16
EPS = 1e-6
NEG = -1e30
LB_FLOOR = 1e-30

LANES = 128
VMEM_LIMIT = 56 * 1024 * 1024
HGRN_SUB = 16
HIGHEST = lax.Precision.HIGHEST
LOG2E = 1.4426950408889634


def _params(n_grid):
    return pltpu.CompilerParams(dimension_semantics=("arbitrary",) * n_grid,
                                vmem_limit_bytes=VMEM_LIMIT)


def _resident(shape):
    nd = len(shape)
    return pl.BlockSpec(shape, lambda *_: (0,) * nd, pipeline_mode=pl.Buffered(1))


def _layer_of(stack, layer, rows=None):
    _, r, c = stack.shape
    return pl.BlockSpec((None, rows or r, c), lambda *_: (layer, 0, 0),
                        pipeline_mode=pl.Buffered(1))


def _rmsnorm(x, w):
    return x * lax.rsqrt(jnp.mean(x * x, axis=-1, keepdims=True) + EPS) * w


def _sigmoid(x):
    return 1.0 / (1.0 + jnp.exp(-x))


def _log1p_exp(z):
    return jnp.log(1.0 + jnp.exp(z))


def _log_sigmoid(x):
    return jnp.minimum(x, 0.0) - _log1p_exp(-jnp.abs(x))


def _mm(a, b):
    return jnp.dot(a, b, preferred_element_type=F32)


def _mm_nt(a, b):
    return lax.dot_general(a, b, (((1,), (1,)), ((), ())), preferred_element_type=F32)


def _mm_tn(a, b):
    return lax.dot_general(a, b, (((0,), (0,)), ((), ())), preferred_element_type=F32)


def _hgrn_proj_body(x_ref, nw_ref, w_ref, lb_ref, q_ref, k_ref, g_ref, v_ref, gate_ref):
    f = q_ref.shape[1]
    d = v_ref.shape[1]
    xn = _rmsnorm(x_ref[...], nw_ref[...]).astype(BF16)
    q = _mm(xn, w_ref[:, 0:f])
    q_ref[...] = q * _sigmoid(q)
    fl = _mm(xn, w_ref[:, f:2 * f])
    e = jnp.exp(-jnp.abs(fl))
    r = 1.0 / (1.0 + e)
    a = lb_ref[0:1, :]
    b = lb_ref[1:2, :] + (jnp.minimum(fl, 0.0) - jnp.log(1.0 + e))
    g_ref[...] = jnp.maximum(a, b) + _log1p_exp(-jnp.abs(a - b))
    k_ref[...] = lb_ref[2:3, :] * jnp.where(fl >= 0.0, e * r, r)
    v_ref[...] = _mm(xn, w_ref[:, 2 * f:2 * f + d])
    gt = _mm(xn, w_ref[:, 2 * f + d:])
    gate_ref[...] = gt * _sigmoid(gt)


def _hgrn_proj(x, nw, w_stack, layer, lbp, tm):
    n, d = x.shape
    f = lbp.shape[1]
    row = lambda w: pl.BlockSpec((tm, w), lambda i: (i, 0))
    return pl.pallas_call(
        _hgrn_proj_body,
        grid=(n // tm,),
        in_specs=[row(d), _resident((1, d)), _layer_of(w_stack, layer), _resident(lbp.shape)],
        out_specs=[row(f), row(f), row(f), row(d), row(d)],
        out_shape=[jax.ShapeDtypeStruct((n, w), F32) for w in (f, f, f, d, d)],
        compiler_params=_params(1),
        name="hgrn_proj",
    )(x, nw, w_stack, lbp)


def _hgrn_chunk(q, k, g, v, sts, c):
    nh = len(sts)
    sub, half = HGRN_SUB, HGRN_SUB // 2
    hs = lambda x, h: x[:, h * LANES:(h + 1) * LANES]
    r_i = lax.broadcasted_iota(jnp.int32, (c, c), 0)
    c_i = lax.broadcasted_iota(jnp.int32, (c, c), 1)
    tril = jnp.where(r_i >= c_i, 1.0, 0.0).astype(F32)
    gc = jnp.dot(tril, g, precision=HIGHEST, preferred_element_type=F32) * LOG2E
    g_last = gc[c - 1:c, :]
    qe = (q * jnp.exp2(gc)).astype(BF16)
    o_inter = [_mm_nt(hs(qe, h), sts[h].astype(BF16)) for h in range(nh)]
    kd = (k * jnp.exp2(g_last - gc)).astype(BF16)
    vb = v.astype(BF16)
    st_dec = jnp.exp2(g_last)
    st_new = [sts[h] * hs(st_dec, h) + _mm_tn(hs(vb, h), hs(kd, h)) for h in range(nh)]

    lk = jnp.log2(k) - gc
    half_row = lax.broadcasted_iota(jnp.int32, (half, 1), 0)
    out_rows = []
    for i in range(c // sub):
        r0 = i * sub
        gi, qi, vi, lki = gc[r0:r0 + sub], q[r0:r0 + sub], v[r0:r0 + sub], lk[r0:r0 + sub]
        o_lo = [o_inter[h][r0:r0 + half] for h in range(nh)]
        o_hi = [o_inter[h][r0 + half:r0 + sub] for h in range(nh)]
        if i > 0:
            g_first = gc[r0:r0 + 1]
            qs = (qi * jnp.exp2(gi - g_first)).astype(BF16)
            ks = (k[:r0] * jnp.exp2(g_first - gc[:r0])).astype(BF16)
            att = [_mm_nt(hs(qs, h), hs(ks, h)).astype(BF16) for h in range(nh)]
            off = [_mm(att[h], hs(vb[:r0], h)) for h in range(nh)]
            o_lo = [o_lo[h] + off[h][:half] for h in range(nh)]
            o_hi = [o_hi[h] + off[h][half:] for h in range(nh)]
        for rows, acc, s_range in ((slice(0, half), o_lo, range(half)),
                                   (slice(half, sub), o_hi, range(sub))):
            g_rows, q_rows = gi[rows], qi[rows]
            for s in s_range:
                diff = g_rows + lki[s:s + 1]
                if s >= rows.start:
                    diff = jnp.where(half_row >= s - rows.start, diff, NEG)
                prod = q_rows * jnp.exp2(diff)
                for h in range(nh):
                    col = jnp.sum(hs(prod, h), axis=-1, keepdims=True)
                    acc[h] = acc[h] + col * hs(vi, h)[s:s + 1]
        out_rows.append(jnp.concatenate(
            [jnp.concatenate([o_lo[h], o_hi[h]], axis=0) for h in range(nh)], axis=1))
    o = jnp.concatenate(out_rows, axis=0) if len(out_rows) > 1 else out_rows[0]
    return o, st_new


def _hgrn_rec_body(*refs, c, r, has_meta):
    if has_meta:
        (q_ref, k_ref, g_ref, v_ref, s0_ref, qm_ref, km_ref, gm_ref, vm_ref,
         o_ref, s_ref, om_ref, st_ref) = refs
    else:
        q_ref, k_ref, g_ref, v_ref, s0_ref, o_ref, s_ref, st_ref = refs
    blk = pl.program_id(1)
    nh = st_ref.shape[0]

    @pl.when(blk == 0)
    def _():
        sts = [s0_ref[0, h].T for h in range(nh)]
        if has_meta:
            o_m, sts = _hgrn_chunk(qm_ref[...], km_ref[...], gm_ref[...], vm_ref[...],
                                   sts, qm_ref.shape[0])
            om_ref[...] = o_m
        for h in range(nh):
            st_ref[h] = sts[h]

    def body(j, carry):
        sl = pl.ds(pl.multiple_of(j * c, c), c)
        o, sts = _hgrn_chunk(q_ref[sl, :], k_ref[sl, :], g_ref[sl, :], v_ref[sl, :],
                             [st_ref[h] for h in range(nh)], c)
        o_ref[sl, :] = o
        for h in range(nh):
            st_ref[h] = sts[h]
        return carry

    lax.fori_loop(0, r // c, body, 0)

    @pl.when(blk == pl.num_programs(1) - 1)
    def _():
        for h in range(nh):
            s_ref[0, h] = st_ref[h].T


def _hgrn_rec(q, k, g, v, s0, *, n_seq, seq_len, r, c, meta=None, layer=None):
    h, dk, dv = s0.shape[-3:]
    nblk = seq_len // r
    tok = lambda w: pl.BlockSpec((r, h * w), lambda b, i: (b * nblk + i, 0))
    st_spec = pl.BlockSpec((1, h, dk, dv), lambda b, i: (b, 0, 0, 0))
    s0_spec = st_spec if layer is None else pl.BlockSpec(
        (None, 1, h, dk, dv), lambda b, i: (layer, b, 0, 0, 0))
    in_specs = [tok(dk), tok(dk), tok(dk), tok(dv), s0_spec]
    args = [q, k, g, v, s0]
    out_specs = [tok(dv), st_spec]
    out_shape = [jax.ShapeDtypeStruct((n_seq * seq_len, h * dv), F32),
                 jax.ShapeDtypeStruct((n_seq, h, dk, dv), F32)]
    if meta is not None:
        qm, km, gm, vm, n_meta, meta_blk = meta
        mspec = lambda w: pl.BlockSpec((n_meta, h * w), lambda b, i: (meta_blk, 0))
        in_specs += [mspec(dk), mspec(dk), mspec(dk), mspec(dv)]
        args += [qm, km, gm, vm]
        out_specs.append(pl.BlockSpec((n_meta, h * dv), lambda b, i: (0, 0)))
        out_shape.append(jax.ShapeDtypeStruct((n_meta, h * dv), F32))
    return pl.pallas_call(
        functools.partial(_hgrn_rec_body, c=c, r=r, has_meta=meta is not None),
        grid=(n_seq, nblk),
        in_specs=in_specs,
        out_specs=out_specs,
        out_shape=out_shape,
        scratch_shapes=[pltpu.VMEM((h, dv, dk), F32)],
        compiler_params=_params(2),
        name="hgrn_rec",
    )(*args)


def _mix_ffn_body(*refs, head_dim, dff, fc):
    if head_dim:
        (o_ref, gate_ref, x_ref, gn_ref, wo_ref, pm_ref,
         nf_ref, wgu_ref, wd_ref, pf_ref, y_ref) = refs
        o = o_ref[...]
        parts = []
        for h in range(o.shape[1] // head_dim):
            oh = o[:, h * head_dim:(h + 1) * head_dim]
            parts.append(oh * lax.rsqrt(jnp.mean(oh * oh, axis=-1, keepdims=True) + EPS))
        o = (jnp.concatenate(parts, axis=-1) * gn_ref[...] * gate_ref[...]).astype(BF16)
    else:
        o_ref, x_ref, wo_ref, pm_ref, nf_ref, wgu_ref, wd_ref, pf_ref, y_ref = refs
        o = o_ref[...]
    x = x_ref[...] + _rmsnorm(_mm(o, wo_ref[...]), pm_ref[...])
    xn = _rmsnorm(x, nf_ref[...]).astype(BF16)
    acc = jnp.zeros(x.shape, F32)
    for c0 in range(0, dff, fc):
        a = _mm(xn, wgu_ref[:, c0:c0 + fc])
        u = _mm(xn, wgu_ref[:, dff + c0:dff + c0 + fc])
        hid = (a * _sigmoid(a) * u).astype(BF16)
        acc = acc + _mm(hid, wd_ref[c0:c0 + fc, :])
    y_ref[...] = x + _rmsnorm(acc, pf_ref[...])


def _mix_ffn(o, gate, x, gn, wo_stack, mix_layer, pm, nf, wgu_stack, wd_stack, ffn_layer, pf,
             tm, head_dim):
    n, d = x.shape
    dff = wd_stack.shape[1]
    fc = dff // 2 if (dff // 2) % LANES == 0 else dff
    row = pl.BlockSpec((tm, d), lambda i: (i, 0))
    vec = _resident((1, d))
    ffn_specs = [vec, _layer_of(wgu_stack, ffn_layer), _layer_of(wd_stack, ffn_layer), vec]
    wo_spec = _layer_of(wo_stack, mix_layer)
    if head_dim:
        in_specs = [row, row, row, vec, wo_spec, vec] + ffn_specs
        args = (o, gate, x, gn, wo_stack, pm, nf, wgu_stack, wd_stack, pf)
    else:
        in_specs = [row, row, wo_spec, vec] + ffn_specs
        args = (o, x, wo_stack, pm, nf, wgu_stack, wd_stack, pf)
    return pl.pallas_call(
        functools.partial(_mix_ffn_body, head_dim=head_dim, dff=dff, fc=fc),
        grid=(n // tm,),
        in_specs=in_specs,
        out_specs=row,
        out_shape=jax.ShapeDtypeStruct((n, d), F32),
        compiler_params=_params(1),
        name="mix_ffn",
    )(*args)


def _fox_proj_body(x_ref, nw_ref, wt_ref, wf_ref, bf_ref, k_ref, v_ref, lf_ref,
                   qb_ref, kb_ref, vb_ref, *, scale):
    d = k_ref.shape[1]
    nh = lf_ref.shape[1]
    xn = _rmsnorm(x_ref[...], nw_ref[...]).astype(BF16)
    qb_ref[...] = (_mm_nt(xn, wt_ref[0:d, :]) * scale).astype(BF16)
    k = _mm_nt(xn, wt_ref[d:2 * d, :])
    k_ref[...] = k
    kb_ref[...] = k.astype(BF16)
    v = _mm_nt(xn, wt_ref[2 * d:3 * d, :])
    v_ref[...] = v
    vb_ref[...] = v.astype(BF16)
    lf = _log_sigmoid(_mm(xn, wf_ref[...]) + bf_ref[...])
    lf_ref[...] = lf[:, 0:nh]


def _fox_proj(x, nw, wt_stack, layer, wf_bf, bf_row, nh, tm):
    n, d = x.shape
    row = lambda w: pl.BlockSpec((tm, w), lambda i: (i, 0))
    scale = float(d // nh) ** -0.5
    return pl.pallas_call(
        functools.partial(_fox_proj_body, scale=scale),
        grid=(n // tm,),
        in_specs=[row(d), _resident((1, d)), _layer_of(wt_stack, layer, 3 * d),
                  _resident(wf_bf.shape), _resident(bf_row.shape)],
        out_specs=[row(d), row(d), row(nh), row(d), row(d), row(d)],
        out_shape=[jax.ShapeDtypeStruct((n, d), F32), jax.ShapeDtypeStruct((n, d), F32),
                   jax.ShapeDtypeStruct((n, nh), F32), jax.ShapeDtypeStruct((n, d), BF16),
                   jax.ShapeDtypeStruct((n, d), BF16), jax.ShapeDtypeStruct((n, d), BF16)],
        compiler_params=_params(1),
        name="fox_proj",
    )(x, nw, wt_stack, wf_bf, bf_row)


def _head_sq_norms(xt, nh):
    dh = xt.shape[0] // nh
    x2 = xt * xt
    return jnp.concatenate([jnp.sum(x2[h * dh:(h + 1) * dh], axis=0, keepdims=True)
                            for h in range(nh)], axis=0)


def _fox_proj_t_body(x_ref, nw_ref, wt_ref, wf_ref, bf_ref, kt_ref, vt_ref, lf_ref,
                     qtb_ref, kb_ref, vtb_ref, qsq_ref, ksq_ref, *, qscale):
    d = kt_ref.shape[0]
    nh = lf_ref.shape[1]
    xn = _rmsnorm(x_ref[...], nw_ref[...]).astype(BF16)
    qtb = (_mm_nt(wt_ref[0:d, :], xn) * qscale).astype(BF16)
    qtb_ref[...] = qtb
    qsq_ref[...] = _head_sq_norms(qtb.astype(F32), nh)
    kt = _mm_nt(wt_ref[d:2 * d, :], xn)
    kt_ref[...] = kt
    ksq_ref[...] = _head_sq_norms(kt.astype(BF16).astype(F32), nh)
    kb_ref[...] = kt.T.astype(BF16)
    vt = _mm_nt(wt_ref[2 * d:3 * d, :], xn)
    vt_ref[...] = vt
    vtb_ref[...] = vt.astype(BF16)
    lf = _log_sigmoid(_mm(xn, wf_ref[...]) + bf_ref[...])
    lf_ref[...] = lf[:, 0:nh]


def _fox_proj_t(x, nw, wt_stack, layer, wf_bf, bf_row, nh, tm, qscale):
    n, d = x.shape
    row = lambda w: pl.BlockSpec((tm, w), lambda i: (i, 0))
    col = pl.BlockSpec((d, tm), lambda i: (0, i))
    sq = pl.BlockSpec((nh, tm), lambda i: (0, i))
    return pl.pallas_call(
        functools.partial(_fox_proj_t_body, qscale=qscale),
        grid=(n // tm,),
        in_specs=[row(d), _resident((1, d)), _layer_of(wt_stack, layer, 3 * d),
                  _resident(wf_bf.shape), _resident(bf_row.shape)],
        out_specs=[col, col, row(nh), col, row(d), col, sq, sq],
        out_shape=[jax.ShapeDtypeStruct((d, n), F32), jax.ShapeDtypeStruct((d, n), F32),
                   jax.ShapeDtypeStruct((n, nh), F32), jax.ShapeDtypeStruct((d, n), BF16),
                   jax.ShapeDtypeStruct((n, d), BF16), jax.ShapeDtypeStruct((d, n), BF16),
                   jax.ShapeDtypeStruct((nh, n), F32), jax.ShapeDtypeStruct((nh, n), F32)],
        compiler_params=_params(1),
        name="fox_proj_t",
    )(x, nw, wt_stack, wf_bf, bf_row)


BIAS_LANES_PER_HEAD = 6


def _split3(v):
    hi = v.astype(BF16).astype(F32)
    r1 = v - hi
    mid = r1.astype(BF16).astype(F32)
    lo = (r1 - mid).astype(BF16).astype(F32)
    return hi, mid, lo


def _fox_kbias_body(c4_ref, aux_ref, *, cscale, nh):
    tm = c4_ref.shape[0]
    npair = aux_ref.shape[0]
    hi, mid, lo = _split3(c4_ref[...] * (-cscale))
    group = lax.broadcasted_iota(jnp.int32, (tm, 4 * nh), 1) // nh
    src = jnp.where(group == 0, hi, jnp.where(group == 1, mid, jnp.where(group == 2, lo, 1.0)))
    row = lax.broadcasted_iota(jnp.int32, (4 * nh, npair * LANES), 0)
    col = lax.broadcasted_iota(jnp.int32, (4 * nh, npair * LANES), 1)
    lane = col % LANES
    slot = lane % BIAS_LANES_PER_HEAD
    head = 2 * (col // LANES) + lane // BIAS_LANES_PER_HEAD
    pick = (lane < 2 * BIAS_LANES_PER_HEAD) & (row == jnp.minimum(slot, 3) * nh + head)
    out = _mm(src.astype(BF16), jnp.where(pick, 1.0, 0.0).astype(BF16))
    for p in range(npair):
        aux_ref[p] = out[:, p * LANES:(p + 1) * LANES].astype(BF16)


def _fox_kbias(c_rows, npair, tm, cscale):
    n, nh = c_rows.shape
    return pl.pallas_call(
        functools.partial(_fox_kbias_body, cscale=cscale, nh=nh),
        grid=(n // tm,),
        in_specs=[pl.BlockSpec((tm, 4 * nh), lambda i: (i, 0))],
        out_specs=pl.BlockSpec((npair, tm, LANES), lambda i: (0, i, 0)),
        out_shape=jax.ShapeDtypeStruct((npair, n, LANES), BF16),
        compiler_params=_params(1),
        name="fox_kbias",
    )(jnp.tile(c_rows, (1, 4)))


def _lane_cumsum_body(x_ref, y_ref, *, reverse_exclusive):
    rows, n = x_ref.shape
    nb = n // LANES
    t_i = lax.broadcasted_iota(jnp.int32, (LANES, LANES), 0)
    k_i = lax.broadcasted_iota(jnp.int32, (LANES, LANES), 1)
    sel = (t_i > k_i) if reverse_exclusive else (t_i <= k_i)
    tri = jnp.where(sel, 1.0, 0.0).astype(F32)

    def body(i, carry):
        b = (nb - 1 - i) if reverse_exclusive else i
        sl = slice(b * LANES, (b + 1) * LANES)
        xb = x_ref[:, sl]
        y_ref[:, sl] = jnp.dot(xb, tri, precision=HIGHEST, preferred_element_type=F32) + carry
        return carry + jnp.sum(xb, axis=-1, keepdims=True)

    carry = jnp.zeros((rows, 1), F32)
    for i in range(nb):
        carry = body(i, carry)


def _lane_cumsum(x, reverse_exclusive, tr):
    rows, n = x.shape
    spec = pl.BlockSpec((tr, n), lambda i: (i, 0))
    return pl.pallas_call(
        functools.partial(_lane_cumsum_body, reverse_exclusive=reverse_exclusive),
        grid=(rows // tr,),
        in_specs=[spec],
        out_specs=spec,
        out_shape=jax.ShapeDtypeStruct((rows, n), F32),
        compiler_params=_params(1),
        name="lane_cumsum",
    )(x)


ATTN_TILE = 512
ATTN_PAIRS_PER_STEP = 2
ACC_PAD = 16


def _fox_attn_body(jlo_ref, fixed_ref, c_ref, sh_ref, qt_ref, k_ref, ka_ref, vt_ref, kme_ref,
                   vtm_ref, qm_ref, km_ref, vm_ref, o_ref, om_ref, qx_s, m_s, acc_s,
                   *, tq, tk, dh, n_meta, cscale, npg):
    i = pl.program_id(1)
    heads = range(2 * npg)
    pair_lanes = lambda x, pp: x[:, pp * 2 * dh:(pp + 1) * 2 * dh]
    pair_rows = lambda x, pp: x[pp * 2 * dh:(pp + 1) * 2 * dh, :]
    lane = lax.broadcasted_iota(jnp.int32, (1, 2 * dh), 1)
    head_sel = [lane < dh, lane >= dh]

    @pl.when(i == 0)
    def _():
        r_i = lax.broadcasted_iota(jnp.int32, (n_meta, n_meta), 0)
        c_i = lax.broadcasted_iota(jnp.int32, (n_meta, n_meta), 1)
        outs = []
        for pp in range(npg):
            qm, km, vm = (pair_lanes(r[...], pp) for r in (qm_ref, km_ref, vm_ref))
            out = jnp.zeros((n_meta, 2 * dh), F32)
            for a in range(2):
                qa = jnp.where(head_sel[a], qm, jnp.zeros_like(qm))
                x = _mm_nt(qa, km) - c_ref[2 * pp + a, :, 0:n_meta]
                x = jnp.where(c_i <= r_i, x, NEG)
                p = jnp.exp(x - jnp.max(x, axis=-1, keepdims=True))
                oa = _mm(p.astype(BF16), vm) / jnp.sum(p, axis=-1, keepdims=True)
                out = jnp.where(head_sel[a], oa, out)
            outs.append(out)
        om_ref[...] = jnp.concatenate(outs, axis=1).astype(BF16)

    q0 = pl.multiple_of(LANES + i * tq, LANES)
    row = lax.broadcasted_iota(jnp.int32, (2 * dh, tq), 0)
    for hh in heads:
        pp, a = divmod(hh, 2)
        qt = pair_rows(qt_ref[...], pp)
        hi, mid, lo = _split3(c_ref[hh, :, pl.ds(q0, tq)] * cscale - sh_ref[hh, :, 0:1])
        slot = row - a * BIAS_LANES_PER_HEAD
        bias = jnp.where(slot == 3, hi, jnp.where(slot == 4, mid, jnp.where(slot == 5, lo, 1.0)))
        bias = jnp.where((slot >= 0) & (slot < BIAS_LANES_PER_HEAD), bias, 0.0)
        qx_s[hh, 0:2 * dh, :] = jnp.where((row >= a * dh) & (row < (a + 1) * dh), qt,
                                          jnp.zeros_like(qt))
        qx_s[hh, 2 * dh:4 * dh, :] = bias.astype(BF16)
        m_s[hh] = jnp.full((1, tq), NEG, F32)
        acc_s[hh] = jnp.zeros((dh + ACC_PAD, tq), F32)

    def scores(kxs):
        return [_mm(kxs[hh // 2], qx_s[hh]) for hh in heads]

    def update(sts, vts, masked, fixed, q_lo=0):
        tkb, tqb = sts[0].shape
        qs = slice(q_lo, q_lo + tqb)
        ones = jnp.where(lax.broadcasted_iota(jnp.int32, (ACC_PAD, tkb), 0) == 0,
                         1.0, 0.0).astype(BF16)
        for hh in heads:
            pp, a = divmod(hh, 2)
            st = sts[hh]
            if masked:
                k_i = lax.broadcasted_iota(jnp.int32, (tkb, tqb), 0)
                q_i = lax.broadcasted_iota(jnp.int32, (tkb, tqb), 1)
                st = jnp.where(k_i <= q_i, st, NEG)
            vx = jnp.concatenate([vts[pp][a * dh:(a + 1) * dh, :], ones], axis=0)
            if fixed:
                acc_s[hh, :, qs] = acc_s[hh, :, qs] + _mm(vx, jnp.exp2(st).astype(BF16))
            else:
                m_prev = m_s[hh, :, qs]
                m_new = jnp.maximum(m_prev, jnp.max(st, axis=0, keepdims=True))
                p = jnp.exp2(st - m_new).astype(BF16)
                acc_s[hh, :, qs] = jnp.exp2(m_prev - m_new) * acc_s[hh, :, qs] + _mm(vx, p)
                m_s[hh, :, qs] = m_new

    def key_block(j):
        k0 = pl.multiple_of(j * tk, tk)
        kxs = [jnp.concatenate([pair_lanes(k_ref[pl.ds(k0, tk), :], pp),
                                ka_ref[pp, pl.ds(k0, tk), :]], axis=1) for pp in range(npg)]
        vts = [pair_rows(vt_ref[:, pl.ds(k0, tk)], pp) for pp in range(npg)]
        return kxs, vts

    def run(fixed):
        def body(j, carry):
            kxs, vts = key_block(j)
            update(scores(kxs), vts, False, fixed)
            return carry

        lax.fori_loop(jlo_ref[pl.program_id(0), i], i, body, 0)

        kxs, vts = key_block(i)
        hk = tk // 2
        st_a = [_mm(kxs[hh // 2][:hk], qx_s[hh]) for hh in heads]
        st_b = [_mm(kxs[hh // 2][hk:], qx_s[hh, :, hk:]) for hh in heads]
        st_meta = scores([kme_ref[pp] for pp in range(npg)])
        update(st_a, [v[:, :hk] for v in vts], True, fixed)
        update(st_b, [v[:, hk:] for v in vts], True, fixed, q_lo=hk)
        update(st_meta, [pair_rows(vtm_ref[...], pp) for pp in range(npg)], False, fixed)

    is_fixed = fixed_ref[pl.program_id(0)] != 0
    pl.when(is_fixed)(lambda: run(True))
    pl.when(jnp.logical_not(is_fixed))(lambda: run(False))

    outs = []
    for pp in range(npg):
        halves = []
        for a in range(2):
            acc = acc_s[2 * pp + a]
            halves.append(acc[0:dh] / acc[dh:dh + 1])
        outs.append(jnp.concatenate(halves, axis=0).T)
    o_ref[...] = jnp.concatenate(outs, axis=1).astype(BF16)


def _fox_attn(jlo, fixed, shift3, c3, qt_b, kb, kaux, vt_b, kx_meta, vt_meta, qb_aux, kb_aux,
              vb_aux, n_meta, meta_blk, dh, tq, cscale, npg):
    n, d = kb.shape
    gw = 2 * dh * npg
    ngroups = d // gw
    nl = c3.shape[2]
    mspec = pl.BlockSpec((n_meta, gw), lambda p, i, *_: (meta_blk, p))
    once = dict(pipeline_mode=pl.Buffered(1))
    return pl.pallas_call(
        functools.partial(_fox_attn_body, tq=tq, tk=tq, dh=dh, n_meta=n_meta, cscale=cscale,
                          npg=npg),
        grid_spec=pltpu.PrefetchScalarGridSpec(
            num_scalar_prefetch=2,
            grid=(ngroups, n // tq),
            in_specs=[pl.BlockSpec((2 * npg, 1, nl), lambda p, i, *_: (p, 0, 0)),
                      pl.BlockSpec((2 * npg, 1, LANES), lambda p, i, *_: (p, 0, 0)),
                      pl.BlockSpec((gw, tq), lambda p, i, *_: (p, i)),
                      pl.BlockSpec((n, gw), lambda p, i, *_: (0, p), **once),
                      pl.BlockSpec((npg, n, LANES), lambda p, i, *_: (p, 0, 0), **once),
                      pl.BlockSpec((gw, n), lambda p, i, *_: (p, 0), **once),
                      pl.BlockSpec((npg, n_meta, 4 * dh), lambda p, i, *_: (p, 0, 0)),
                      pl.BlockSpec((gw, n_meta), lambda p, i, *_: (p, 0)),
                      mspec, mspec, mspec],
            out_specs=[pl.BlockSpec((tq, gw), lambda p, i, *_: (i, p)),
                       pl.BlockSpec((n_meta, gw), lambda p, i, *_: (0, p))],
            scratch_shapes=[pltpu.VMEM((2 * npg, 4 * dh, tq), BF16),
                            pltpu.VMEM((2 * npg, 1, tq), F32),
                            pltpu.VMEM((2 * npg, dh + ACC_PAD, tq), F32)]),
        out_shape=[jax.ShapeDtypeStruct((n, d), BF16), jax.ShapeDtypeStruct((n_meta, d), BF16)],
        compiler_params=_params(2),
        name="fox_attn",
    )(jlo, fixed, c3, shift3, qt_b, kb, kaux, vt_b, kx_meta, vt_meta, qb_aux, kb_aux, vb_aux)


PRUNE_NATS = 37.0
NORM_SLACK = 1.01


FIXED_SHIFT_MAX_SPAN = 60.0


def _fox_score_shift(qsq, ksq, group):
    nh = qsq.shape[0]
    u = jnp.sqrt(jnp.max(qsq, axis=1) * jnp.max(ksq, axis=1)) * NORM_SLACK
    fixed = jnp.all((2.0 * u <= FIXED_SHIFT_MAX_SPAN).reshape(nh // group, group), axis=1)
    shift = jnp.where(jnp.repeat(fixed, group), u, 0.0)
    return fixed.astype(jnp.int32), jnp.broadcast_to(shift[:, None, None], (nh, 1, LANES))


def _fox_first_key_block(c_main, qsq, ksq, tq, group):
    nh, n = c_main.shape
    nq = n // tq
    u = jnp.sqrt(jnp.max(qsq, axis=1) * jnp.max(ksq, axis=1)) * (NORM_SLACK / LOG2E)
    c_first = c_main[:, 0::tq]
    c_last = c_main[:, tq - 1::tq]
    bound = 2.0 * u[:, None, None] + c_first[:, :, None] - c_last[:, None, :]
    j_lt_i = jnp.arange(nq)[None, :] < jnp.arange(nq)[:, None]
    skip = (bound <= -PRUNE_NATS) & j_lt_i[None]
    jlo = jnp.sum(skip, axis=2).astype(jnp.int32)
    return jnp.min(jlo.reshape(nh // group, group, nq), axis=1)


def _fox_sample_body(q_ref, kn_ref, vn_ref, cn_ref, kc_ref, vc_ref, suf_ref, o_ref,
                     m_s, l_s, acc_s, *, nh, dh, s_len):
    t = pl.program_id(1)

    @pl.when(t == 0)
    def _():
        r_i = lax.broadcasted_iota(jnp.int32, (s_len, s_len), 0)
        c_i = lax.broadcasted_iota(jnp.int32, (s_len, s_len), 1)
        cols = [slice(h * dh, (h + 1) * dh) for h in range(nh)]
        x = [_mm_nt(q_ref[:, cols[h]], kn_ref[:, cols[h]]) - cn_ref[0, h:h + 1, 0:s_len]
             for h in range(nh)]
        x = [jnp.where(c_i <= r_i, x[h], NEG) for h in range(nh)]
        m = [jnp.max(x[h], axis=-1, keepdims=True) for h in range(nh)]
        p = [jnp.exp(x[h] - m[h]) for h in range(nh)]
        pv = [_mm(p[h].astype(BF16), vn_ref[:, cols[h]]) for h in range(nh)]
        for h in range(nh):
            m_s[h] = m[h]
            l_s[h] = jnp.sum(p[h], axis=-1, keepdims=True)
            acc_s[h] = pv[h]

    heads = range(nh)
    x = [_mm(q_ref[:, h * dh:(h + 1) * dh], kc_ref[0, h].astype(BF16)) + suf_ref[0, h:h + 1, :]
         for h in heads]
    m_prev = [m_s[h] for h in heads]
    m_new = [jnp.maximum(m_prev[h], jnp.max(x[h], axis=-1, keepdims=True)) for h in heads]
    p = [jnp.exp(x[h] - m_new[h]) for h in heads]
    pv = [_mm_nt(p[h].astype(BF16), vc_ref[0, h].astype(BF16)) for h in heads]
    for h in heads:
        alpha = jnp.exp(m_prev[h] - m_new[h])
        l_s[h] = alpha * l_s[h] + jnp.sum(p[h], axis=-1, keepdims=True)
        acc_s[h] = alpha * acc_s[h] + pv[h]
        m_s[h] = m_new[h]

    @pl.when(t == pl.num_programs(1) - 1)
    def _():
        o_ref[...] = jnp.concatenate([acc_s[h] / l_s[h] for h in range(nh)], axis=-1).astype(BF16)


def _fox_sample(qb, kb, vb, cn, kc_t, vc_t, layer, suf, s_len, tk):
    _, nb, nh, dh, past = kc_t.shape
    d = nh * dh
    tok = pl.BlockSpec((s_len, d), lambda b, t: (b, 0))
    cache = pl.BlockSpec((None, 1, nh, dh, tk), lambda b, t: (layer, b, 0, 0, t))
    return pl.pallas_call(
        functools.partial(_fox_sample_body, nh=nh, dh=dh, s_len=s_len),
        grid=(nb, past // tk),
        in_specs=[tok, tok, tok, pl.BlockSpec((1, nh, LANES), lambda b, t: (b, 0, 0)),
                  cache, cache, pl.BlockSpec((1, nh, tk), lambda b, t: (b, 0, t))],
        out_specs=tok,
        out_shape=jax.ShapeDtypeStruct((nb * s_len, d), BF16),
        scratch_shapes=[pltpu.VMEM((nh, s_len, 1), F32), pltpu.VMEM((nh, s_len, 1), F32),
                        pltpu.VMEM((nh, s_len, dh), F32)],
        compiler_params=_params(2),
        name="fox_sample",
    )(qb, kb, vb, cn, kc_t, vc_t, suf)


def _largest_tile(n, cap):
    t = min(n, cap)
    while n % t:
        t //= 2
    return t


def kernel(x_prompt, x_sample, state_hgrn, cache_k, cache_v, cache_logf, meta_tokens,
           norm_pre_mix, norm_post_mix, norm_pre_ffn, norm_post_ffn,
           a_w_in, a_lb_logits, a_g_norm, a_w_out, b_w_in, b_f, b_w_out,
           ffn_w_gu, ffn_w_down):
    batch, seq, d = x_prompt.shape
    assert batch == 1, "one prompt stream per step"
    n_dec, s_len, _ = x_sample.shape
    n_meta = meta_tokens.shape[0]
    depth = norm_pre_mix.shape[0]
    _, _, a_heads, a_dk, a_dv = state_hgrn.shape
    _, _, past, b_heads, b_dh = cache_k.shape
    ns = n_dec * s_len
    assert s_len % HGRN_SUB == 0 and n_meta % HGRN_SUB == 0 and ns % n_meta == 0
    assert 2 * b_dh == LANES and a_dk == LANES and a_dv == LANES

    tm = _largest_tile(seq, 512)
    n_aux = ns + n_meta
    meta_blk = ns // n_meta
    rec_r = _largest_tile(seq, 512)
    rec_c = _largest_tile(rec_r, 64)
    tq = _largest_tile(seq, ATTN_TILE)
    tk_cache = _largest_tile(past, 2048)
    npair = b_heads // 2
    npg = ATTN_PAIRS_PER_STEP if npair % ATTN_PAIRS_PER_STEP == 0 else 1

    sm = jax.nn.softmax(a_lb_logits.astype(F32), axis=0)
    lb_all = jnp.cumsum(sm, axis=0) - sm[0]

    xm = x_prompt.reshape(seq, d)
    xa = jnp.concatenate([x_sample.reshape(ns, d), meta_tokens.astype(F32)], axis=0)
    row = lambda w: w.reshape(1, -1).astype(F32)
    cache_kt = jnp.transpose(cache_k, (0, 1, 3, 4, 2))
    cache_vt = jnp.transpose(cache_v, (0, 1, 3, 4, 2))
    cache_lt = jnp.swapaxes(cache_logf.astype(F32), 2, 3)

    a_w_in_bf, a_w_out_bf, b_w_out_bf = (w.astype(BF16) for w in (a_w_in, a_w_out, b_w_out))
    b_wt_bf = jnp.swapaxes(b_w_in, 1, 2).astype(BF16)
    ffn_w_gu_bf, ffn_w_down_bf = ffn_w_gu.astype(BF16), ffn_w_down.astype(BF16)
    state_f32 = state_hgrn.astype(F32)

    st_p, st_s, k_p, v_p, lf_p, k_s, v_s, lf_s = [], [], [], [], [], [], [], []
    for i in range(depth):
        j = i // 2
        if i % 2 == 0:
            lb = lb_all[j]
            lbp = jnp.stack([jnp.log(jnp.maximum(lb, LB_FLOOR)), jnp.log1p(-lb), 1.0 - lb])
            w_out = a_w_out_bf
            qm, km, gm, vm, gtm = _hgrn_proj(xm, row(norm_pre_mix[i]), a_w_in_bf, j, lbp, tm)
            qa, ka, ga, va, gta = _hgrn_proj(xa, row(norm_pre_mix[i]), a_w_in_bf, j, lbp, n_aux)
            o_s, s_s = _hgrn_rec(qa, ka, ga, va, state_f32,
                                 n_seq=n_dec, seq_len=s_len, r=s_len, c=s_len, layer=j)
            o_m, s_p, o_meta = _hgrn_rec(qm, km, gm, vm,
                                         jnp.zeros((1, a_heads, a_dk, a_dv), F32),
                                         n_seq=1, seq_len=seq, r=rec_r, c=rec_c,
                                         meta=(qa, ka, ga, va, n_meta, meta_blk))
            st_p.append(s_p.astype(state_hgrn.dtype))
            st_s.append(s_s.astype(state_hgrn.dtype))
            o_a = jnp.concatenate([o_s, o_meta], axis=0)
            mix_m = (o_m, gtm, xm, row(a_g_norm[j]))
            mix_a = (o_a, gta, xa, row(a_g_norm[j]))
            head_dim = a_dv
        else:
            w_f = jnp.pad(b_w_in[j][:, 3 * d:], ((0, 0), (0, LANES - b_heads))).astype(BF16)
            bf_row = jnp.pad(b_f[j].astype(F32), (0, LANES - b_heads)).reshape(1, LANES)
            w_out = b_w_out_bf
            scale = float(b_dh) ** -0.5
            kt_m, vt_m, lf_m, qtb_m, kb_m, vtb_m, qsq, ksq = _fox_proj_t(
                xm, row(norm_pre_mix[i]), b_wt_bf, j, w_f, bf_row, b_heads, tm, scale * LOG2E)
            kf_a, vf_a, lf_a, qb_a, kb_a, vb_a = _fox_proj(
                xa, row(norm_pre_mix[i]), b_wt_bf, j, w_f, bf_row, b_heads, n_aux)
            lf_t = jnp.concatenate(
                [jnp.pad(lf_a[ns:].T, ((0, 0), (0, LANES - n_meta))), lf_m.T], axis=1)
            c_all = _lane_cumsum(lf_t, False, b_heads)
            kaux = _fox_kbias(c_all[:, LANES:].T, npair, tm, LOG2E)
            kaux_meta = _fox_kbias(c_all[:, :n_meta].T, npair, n_meta, LOG2E)
            k_meta = jnp.swapaxes(kb_a[ns:].reshape(n_meta, npair, LANES), 0, 1)
            jlo = _fox_first_key_block(c_all[:, LANES:], qsq, ksq, tq, 2 * npg)
            fixed, shift3 = _fox_score_shift(qsq, ksq, 2 * npg)
            o_m, o_meta = _fox_attn(jlo, fixed, shift3, c_all.reshape(b_heads, 1, -1),
                                    qtb_m, kb_m, kaux, vtb_m,
                                    jnp.concatenate([k_meta, kaux_meta], axis=-1), vb_a[ns:].T,
                                    qb_a, kb_a, vb_a, n_meta, meta_blk, b_dh, tq, LOG2E, npg)
            cl_t = cache_lt[j].reshape(n_dec * b_heads, past)
            tr = _largest_tile(n_dec * b_heads, LANES)
            suf = _lane_cumsum(cl_t, True, tr).reshape(n_dec, b_heads, past)
            ln_t = jnp.swapaxes(lf_a[:ns].reshape(n_dec, s_len, b_heads), 1, 2)
            ln_t = jnp.pad(ln_t, ((0, 0), (0, 0), (0, LANES - s_len)))
            cn = _lane_cumsum(ln_t.reshape(n_dec * b_heads, LANES), False, tr)
            o_s = _fox_sample(qb_a, kb_a, vb_a, cn.reshape(n_dec, b_heads, LANES),
                              cache_kt, cache_vt, j, suf, s_len, tk_cache)
            o_a = jnp.concatenate([o_s, o_meta], axis=0)
            mix_m = (o_m, None, xm, None)
            mix_a = (o_a, None, xa, None)
            head_dim = 0
            shp = lambda t, n: t.reshape(-1, n, b_heads, b_dh)
            k_p.append((kf_a[ns:].T, kt_m))
            v_p.append((vf_a[ns:].T, vt_m))
            lf_p.append(jnp.concatenate([lf_a[ns:], lf_m], axis=0).reshape(1, n_meta + seq, b_heads))
            k_s.append(shp(kf_a[:ns], s_len))
            v_s.append(shp(vf_a[:ns], s_len))
            lf_s.append(lf_a[:ns].reshape(n_dec, s_len, b_heads))
        ffn_w = (w_out, j, row(norm_post_mix[i]), row(norm_pre_ffn[i]), ffn_w_gu_bf,
                 ffn_w_down_bf, i, row(norm_post_ffn[i]))
        xm = _mix_ffn(*mix_m, *ffn_w, tm, head_dim)
        xa = _mix_ffn(*mix_a, *ffn_w, n_aux, head_dim)

    def prompt_kv(parts):
        t = jnp.concatenate([jnp.stack([m for m, _ in parts]), jnp.stack([x for _, x in parts])],
                            axis=2)
        t = t.reshape(len(parts), b_heads, b_dh, n_meta + seq)
        return jnp.transpose(t, (0, 3, 1, 2))[:, None]

    return (xm.reshape(1, seq, d), xa[:ns].reshape(n_dec, s_len, d),
            jnp.stack(st_p), prompt_kv(k_p), prompt_kv(v_p), jnp.stack(lf_p),
            jnp.stack(st_s), jnp.stack(k_s), jnp.stack(v_s), jnp.stack(lf_s))
```
